```python
import jax, jax.numpy as jnp
from jax import lax
import numpy as np

D_MODEL = 1024
BATCH = 8
SEQ = 4096
DEPTH = 1
DEC_BATCH = 128
DEC_SEQ = 1
PAST_LEN = 16384
PAGE_SIZE = 128

MIX_WIDTH = D_MODEL
HEAD_DIM = 64
ATTN_WIDTH = MIX_WIDTH // 2
N_Q_HEADS = ATTN_WIDTH // HEAD_DIM
N_KV_HEADS = 2
GQA_GROUP = N_Q_HEADS // N_KV_HEADS
KV_WIDTH = N_KV_HEADS * HEAD_DIM
WINDOW = 128
GM_WIDTH = MIX_WIDTH - ATTN_WIDTH
GM_GROUPS = 4
GM_DIM = GM_WIDTH // GM_GROUPS
CHUNK = 128
D_FF = 2816
PLE_DIM = 256
IN_COLS = ATTN_WIDTH + 2 * KV_WIDTH + 2 * GM_WIDTH
SPLITS = (ATTN_WIDTH, ATTN_WIDTH + KV_WIDTH, ATTN_WIDTH + 2 * KV_WIDTH,
          ATTN_WIDTH + 2 * KV_WIDTH + GM_WIDTH)
RMS_EPS = 1e-6
LN_EPS = 1e-5
ATTN_SCALE = HEAD_DIM ** -0.5

kernel_name = "hymba_swa_sink_gmlp_macaron_step"


def rms_norm(x, g):
    xf = x.astype(jnp.float32)
    y = xf * lax.rsqrt(jnp.mean(xf * xf, axis=-1, keepdims=True) + RMS_EPS)
    return (y * g.astype(jnp.float32)).astype(x.dtype)


def layer_norm(x, g, b):
    xf = x.astype(jnp.float32)
    mu = jnp.mean(xf, axis=-1, keepdims=True)
    xc = xf - mu
    y = xc * lax.rsqrt(jnp.mean(xc * xc, axis=-1, keepdims=True) + LN_EPS)
    return (y * g.astype(jnp.float32) + b.astype(jnp.float32)).astype(x.dtype)


def swiglu(x, w1, w3, w2):
    return (jax.nn.silu(x @ w1) * (x @ w3)) @ w2


def sink_softmax(scores, mask, sinks):
    s = jnp.where(mask, scores, -jnp.inf)
    sink = sinks.astype(jnp.float32).reshape(N_KV_HEADS, GQA_GROUP, 1, 1)
    m = jnp.maximum(jnp.max(s, axis=-1, keepdims=True), sink)
    e = jnp.exp(s - m)
    return e / (jnp.sum(e, axis=-1, keepdims=True) + jnp.exp(sink - m))


def window_attn_prompt(q, k, v, sinks):
    B, S = q.shape[:2]
    nb = -(-S // WINDOW)
    pad = nb * WINDOW - S
    qb = jnp.pad(q, ((0, 0), (0, pad), (0, 0), (0, 0), (0, 0))).reshape(
        B, nb, WINDOW, N_KV_HEADS, GQA_GROUP, HEAD_DIM)
    kp = jnp.pad(k, ((0, 0), (WINDOW, pad), (0, 0), (0, 0))).reshape(B, nb + 1, WINDOW, N_KV_HEADS, HEAD_DIM)
    vp = jnp.pad(v, ((0, 0), (WINDOW, pad), (0, 0), (0, 0))).reshape(B, nb + 1, WINDOW, N_KV_HEADS, HEAD_DIM)
    kb = jnp.concatenate([kp[:, :-1], kp[:, 1:]], axis=2)
    vb = jnp.concatenate([vp[:, :-1], vp[:, 1:]], axis=2)
    scores = jnp.einsum('bnqhgd,bnshd->bnhgqs', qb, kb,
                        preferred_element_type=jnp.float32) * ATTN_SCALE
    qi = jnp.arange(WINDOW)[:, None]
    kj = jnp.arange(2 * WINDOW)[None, :]
    rel = kj - WINDOW - qi
    band = (rel <= 0) & (rel >= -WINDOW)
    kpos = (jnp.arange(nb)[:, None] - 1) * WINDOW + jnp.arange(2 * WINDOW)[None, :]
    mask = band[None] & (kpos >= 0)[:, None, :]
    probs = sink_softmax(scores, mask[None, :, None, None], sinks)
    out = jnp.einsum('bnhgqs,bnshd->bnqhgd', probs.astype(v.dtype), vb)
    return out.reshape(B, nb * WINDOW, ATTN_WIDTH)[:, :S]


def window_attn_sample(q, k_new, v_new, k_buf, v_buf, sinks):
    Bd, T = q.shape[:2]
    k_all = jnp.concatenate([k_buf, k_new], axis=1)
    v_all = jnp.concatenate([v_buf, v_new], axis=1)
    scores = jnp.einsum('bqhgd,bshd->bhgqs', q, k_all,
                        preferred_element_type=jnp.float32) * ATTN_SCALE
    j = jnp.arange(T)[:, None]
    c = jnp.arange(WINDOW + T)[None, :]
    rel = c - WINDOW - j
    mask = (rel <= 0) & (rel >= -WINDOW)
    probs = sink_softmax(scores, mask, sinks)
    out = jnp.einsum('bhgqs,bshd->bqhgd', probs.astype(v_all.dtype), v_all)
    return out.reshape(Bd, T, ATTN_WIDTH), k_all[:, T:], v_all[:, T:]


def chunk_spatial_mix(v, w_s, b_s):
    B, S = v.shape[:2]
    nc = -(-S // CHUNK)
    pad = nc * CHUNK - S
    vc = jnp.pad(v, ((0, 0), (0, pad), (0, 0), (0, 0))).reshape(B, nc, CHUNK, GM_GROUPS, GM_DIM)
    w = w_s * jnp.tril(jnp.ones((CHUNK, CHUNK), w_s.dtype))
    out = jnp.einsum('gts,bnsgc->bntgc', w, vc) + jnp.transpose(b_s)[:, :, None]
    return out.reshape(B, nc * CHUNK, GM_GROUPS, GM_DIM)[:, :S]


def decoder_layer(x, ple, attend, g_ffn1, w1_ffn1, w3_ffn1, w2_ffn1, g_mix, w_in, b_in,
                  gm_ln_g, gm_ln_b, gm_w_s, gm_b_s, w_out, b_out,
                  g_ffn2, w1_ffn2, w3_ffn2, w2_ffn2, g_ple, w_ple_gate, w_ple_proj):
    B, S = x.shape[:2]
    x = x + 0.5 * swiglu(rms_norm(x, g_ffn1), w1_ffn1, w3_ffn1, w2_ffn1)
    z = rms_norm(x, g_mix) @ w_in + b_in
    q, k, v, u, vg = jnp.split(z, SPLITS, axis=-1)
    q = q.reshape(B, S, N_KV_HEADS, GQA_GROUP, HEAD_DIM)
    k = k.reshape(B, S, N_KV_HEADS, HEAD_DIM)
    v = v.reshape(B, S, N_KV_HEADS, HEAD_DIM)
    attn_out, k_state, v_state = attend(q, k, v)
    vg = layer_norm(jax.nn.gelu(vg), gm_ln_g, gm_ln_b)
    gm_out = jax.nn.gelu(u) * chunk_spatial_mix(
        vg.reshape(B, S, GM_GROUPS, GM_DIM), gm_w_s, gm_b_s).reshape(B, S, GM_WIDTH)
    x = x + jnp.concatenate([attn_out, gm_out], axis=-1) @ w_out + b_out
    x = x + 0.5 * swiglu(rms_norm(x, g_ffn2), w1_ffn2, w3_ffn2, w2_ffn2)
    gate = jax.nn.sigmoid(rms_norm(x, g_ple) @ w_ple_gate)
    x = x + gate * (ple @ w_ple_proj)
    return x, k_state, v_state, vg


def setup_inputs(seed: int = 0) -> dict:
    key = jax.random.key(seed)
    ks = iter(jax.random.split(key, 40))
    f32 = jnp.float32
    L, D, F = DEPTH, D_MODEL, D_FF

    def nrm(shape, scale):
        return jax.random.normal(next(ks), shape, f32) * scale

    def gain(shape):
        return 1.0 + nrm(shape, 0.05)

    return {
        "x_prompt": nrm((BATCH, SEQ, D), 1.0),
        "x_sample": nrm((DEC_BATCH, DEC_SEQ, D), 1.0),
        "cache_k_win": nrm((L, DEC_BATCH, WINDOW, N_KV_HEADS, HEAD_DIM), 1.0),
        "cache_v_win": nrm((L, DEC_BATCH, WINDOW, N_KV_HEADS, HEAD_DIM), 1.0),
        "p_prompt": nrm((L, BATCH, SEQ, PLE_DIM), 1.0),
        "p_sample": nrm((L, DEC_BATCH, DEC_SEQ, PLE_DIM), 1.0),
        "g_ffn1": gain((L, D)),
        "w1_ffn1": nrm((L, D, F), D ** -0.5),
        "w3_ffn1": nrm((L, D, F), D ** -0.5),
        "w2_ffn1": nrm((L, F, D), F ** -0.5),
        "g_mix": gain((L, D)),
        "w_in": nrm((L, D, IN_COLS), D ** -0.5),
        "b_in": nrm((L, IN_COLS), 0.02),
        "attn_sinks": nrm((L, N_Q_HEADS), 0.5),
        "gm_ln_g": gain((L, GM_WIDTH)),
        "gm_ln_b": nrm((L, GM_WIDTH), 0.02),
        "gm_w_s": nrm((L, GM_GROUPS, CHUNK, CHUNK), CHUNK ** -0.5),
        "gm_b_s": 1.0 + nrm((L, GM_GROUPS, CHUNK), 0.1),
        "w_out": nrm((L, MIX_WIDTH, D), MIX_WIDTH ** -0.5),
        "b_out": nrm((L, D), 0.02),
        "g_ffn2": gain((L, D)),
        "w1_ffn2": nrm((L, D, F), D ** -0.5),
        "w3_ffn2": nrm((L, D, F), D ** -0.5),
        "w2_ffn2": nrm((L, F, D), F ** -0.5),
        "g_ple": gain((L, D)),
        "w_ple_gate": nrm((L, D, D), D ** -0.5),
        "w_ple_proj": nrm((L, PLE_DIM, D), PLE_DIM ** -0.5),
        "g_final": gain((D,)),
    }


def reference(x_prompt, x_sample, cache_k_win, cache_v_win, p_prompt, p_sample,
              g_ffn1, w1_ffn1, w3_ffn1, w2_ffn1, g_mix, w_in, b_in, attn_sinks,
              gm_ln_g, gm_ln_b, gm_w_s, gm_b_s, w_out, b_out,
              g_ffn2, w1_ffn2, w3_ffn2, w2_ffn2, g_ple, w_ple_gate, w_ple_proj, g_final):
    xp, xs = x_prompt, x_sample
    kp_list, vp_list, ks_list, vs_list, gp_list, gs_list = [], [], [], [], [], []
    last_chunk_start = ((SEQ - 1) // CHUNK) * CHUNK
    for i in range(DEPTH):
        w = (g_ffn1[i], w1_ffn1[i], w3_ffn1[i], w2_ffn1[i], g_mix[i], w_in[i], b_in[i],
             gm_ln_g[i], gm_ln_b[i], gm_w_s[i], gm_b_s[i], w_out[i], b_out[i],
             g_ffn2[i], w1_ffn2[i], w3_ffn2[i], w2_ffn2[i], g_ple[i], w_ple_gate[i], w_ple_proj[i])
        sinks = attn_sinks[i]
        kb, vb = cache_k_win[i], cache_v_win[i]

        def attend_prompt(q, k, v, sinks=sinks):
            return window_attn_prompt(q, k, v, sinks), k[:, -WINDOW:], v[:, -WINDOW:]

        def attend_sample(q, k, v, sinks=sinks, kb=kb, vb=vb):
            return window_attn_sample(q, k, v, kb, vb, sinks)

        xp, kp_new, vp_new, gvp = decoder_layer(xp, p_prompt[i], attend_prompt, *w)
        xs, ks_new, vs_new, gvs = decoder_layer(xs, p_sample[i], attend_sample, *w)
        kp_list.append(kp_new)
        vp_list.append(vp_new)
        ks_list.append(ks_new)
        vs_list.append(vs_new)
        gp_list.append(gvp[:, last_chunk_start:])
        gs_list.append(gvs)
    y_prompt = rms_norm(xp, g_final)
    y_sample = rms_norm(xs, g_final)
    k_win_prompt = jnp.stack(kp_list)
    v_win_prompt = jnp.stack(vp_list)
    k_win_sample = jnp.stack(ks_list)
    v_win_sample = jnp.stack(vs_list)
    gm_v_prompt = jnp.stack(gp_list)
    gm_v_sample = jnp.stack(gs_list)
    return (y_prompt, y_sample, k_win_prompt, v_win_prompt, k_win_sample, v_win_sample, gm_v_prompt, gm_v_sample)
```

```python
import functools
import math

import jax
import jax.numpy as jnp
from jax import lax
from jax.experimental import pallas as pl
from jax.experimental.pallas import tpu as pltpu

F32 = jnp.float32
BF16 = jnp.bfloat16

D_MODEL = 1024
HEAD_DIM = 64
N_Q_HEADS = 8
N_KV_HEADS = 2
ATTN_WIDTH = N_Q_HEADS * HEAD_DIM
KV_WIDTH = N_KV_HEADS * HEAD_DIM
WINDOW = 128
GM_WIDTH = 512
GM_GROUPS = 4
GM_DIM = 128
CHUNK = 128
D_FF = 2816
PLE_DIM = 256
RMS_EPS = 1e-6
LN_EPS = 1e-5
ATTN_SCALE = HEAD_DIM ** -0.5

LANES = 128
FF_CHUNK = 256
N_FF_CHUNKS = D_FF // FF_CHUNK
VMEM_LIMIT_BYTES = 56 * 1024 * 1024

_GELU_C = math.sqrt(2.0 / math.pi)


def _rms(x, g):
    return x * lax.rsqrt(jnp.mean(x * x, axis=-1, keepdims=True) + RMS_EPS) * g


def _sigmoid(x):
    return 1.0 / (1.0 + jnp.exp(-x))


def _gelu(x):
    return 0.5 * x * (1.0 + jnp.tanh(_GELU_C * (x + 0.044715 * (x * x * x))))


def _dot(a, b):
    return jnp.dot(a, b, preferred_element_type=F32)


def _dot_nt(a, b):
    return lax.dot_general(a, b, (((1,), (1,)), ((), ())), preferred_element_type=F32)


def _swiglu(h, w1_ref, w3_ref, w2_ref):
    acc = jnp.zeros((h.shape[0], D_MODEL), F32)
    for c in range(N_FF_CHUNKS):
        cols = slice(c * FF_CHUNK, (c + 1) * FF_CHUNK)
        a = _dot(h, w1_ref[:, cols])
        b = _dot(h, w3_ref[:, cols])
        act = (a * _sigmoid(a) * b).astype(BF16)
        acc = acc + _dot(act, w2_ref[cols, :])
    return acc


def _in_stage_kernel(q_width, x_ref, g1_ref, w1_ref, w3_ref, w2_ref, gm_ref, wq_ref, wr_ref,
                     bq_ref, br_ref, x1_ref, q_ref, kv_ref, uvg_ref):
    del q_width
    x = x_ref[...]
    h = _rms(x, g1_ref[...]).astype(BF16)
    x1 = x + 0.5 * _swiglu(h, w1_ref, w3_ref, w2_ref)
    x1_ref[...] = x1
    h2 = _rms(x1, gm_ref[...]).astype(BF16)
    q_ref[...] = ((_dot(h2, wq_ref[...]) + bq_ref[...]) * ATTN_SCALE).astype(BF16)
    kv_ref[...] = _dot(h2, wr_ref[:, :2 * KV_WIDTH]) + br_ref[:, :2 * KV_WIDTH]
    uvg_ref[...] = _dot(h2, wr_ref[:, 2 * KV_WIDTH:]) + br_ref[:, 2 * KV_WIDTH:]


def _resident(shape):
    zeros = (0,) * len(shape)
    return pl.BlockSpec(shape, lambda *_: zeros, pipeline_mode=pl.Buffered(1))


def _in_stage(x, g1, w1, w3, w2, gmix, wq, wr, bq, br, tm):
    b, s, _ = x.shape
    qw = wq.shape[1]
    rw = 2 * GM_WIDTH

    def tile(width):
        return pl.BlockSpec((None, tm, width), lambda i, j: (i, j, 0))

    return pl.pallas_call(
        functools.partial(_in_stage_kernel, qw),
        out_shape=(jax.ShapeDtypeStruct((b, s, D_MODEL), F32),
                   jax.ShapeDtypeStruct((b, s, qw), BF16),
                   jax.ShapeDtypeStruct((b, s, 2 * KV_WIDTH), F32),
                   jax.ShapeDtypeStruct((b, s, rw), F32)),
        grid=(b, s // tm),
        in_specs=[tile(D_MODEL), _resident(g1.shape), _resident(w1.shape), _resident(w3.shape),
                  _resident(w2.shape), _resident(gmix.shape), _resident(wq.shape),
                  _resident(wr.shape), _resident(bq.shape), _resident(br.shape)],
        out_specs=(tile(D_MODEL), tile(qw), tile(2 * KV_WIDTH), tile(rw)),
        compiler_params=pltpu.CompilerParams(
            dimension_semantics=("arbitrary", "arbitrary"), vmem_limit_bytes=VMEM_LIMIT_BYTES),
        name="in_stage",
    )(x, g1, w1, w3, w2, gmix, wq, wr, bq, br)


def _gm_norm(vg, ln_g, ln_b):
    a = _gelu(vg)
    mu = jnp.mean(a, axis=-1, keepdims=True)
    ac = a - mu
    return ac * lax.rsqrt(jnp.mean(ac * ac, axis=-1, keepdims=True) + LN_EPS) * ln_g + ln_b


def _mix_prompt_kernel(blocks, sinks_ref, q_ref, kv_ref, kvp_ref, uvg_ref, lng_ref, lnb_ref,
                       ws_ref, bst_ref, mix_ref, gmv_ref):
    j = pl.program_id(1)
    lane = lax.broadcasted_iota(jnp.int32, (2 * WINDOW, LANES), 1)
    lo = lane < HEAD_DIM
    qi = lax.broadcasted_iota(jnp.int32, (WINDOW, 2 * WINDOW), 0)
    kj = lax.broadcasted_iota(jnp.int32, (WINDOW, 2 * WINDOW), 1)
    band = (kj >= qi) & (kj <= qi + WINDOW)
    ti = lax.broadcasted_iota(jnp.int32, (CHUNK, CHUNK), 0)
    si = lax.broadcasted_iota(jnp.int32, (CHUNK, CHUNK), 1)
    tril = si <= ti

    for blk in range(blocks):
        rows = slice(blk * WINDOW, (blk + 1) * WINDOW)
        if blk == 0:
            kv2 = jnp.concatenate([kvp_ref[...], kv_ref[rows, :]], axis=0)
            mask = band & (kj >= jnp.where(j > 0, 0, WINDOW))
        else:
            kv2 = kv_ref[(blk - 1) * WINDOW:(blk + 1) * WINDOW, :]
            mask = band
        k2 = kv2[:, :KV_WIDTH].astype(BF16)
        v2 = kv2[:, KV_WIDTH:].astype(BF16)
        k2r = pltpu.roll(k2, HEAD_DIM, 1)
        v2r = pltpu.roll(v2, HEAD_DIM, 1)
        zero = jnp.zeros_like(k2)
        k_lo = (jnp.where(lo, k2, zero), jnp.where(lo, k2r, zero))
        k_hi = (jnp.where(lo, zero, k2r), jnp.where(lo, zero, k2))
        v_lo = (jnp.where(lo, v2, zero), jnp.where(lo, v2r, zero))
        v_hi = (jnp.where(lo, zero, v2r), jnp.where(lo, zero, v2))

        for pair in range(N_Q_HEADS // 2):
            h = pair // 2
            qp = q_ref[rows, pair * LANES:(pair + 1) * LANES]
            out = None
            for half, (kk, vv) in enumerate(((k_lo[h], v_lo[h]), (k_hi[h], v_hi[h]))):
                sink = sinks_ref[2 * pair + half]
                s = jnp.where(mask, _dot_nt(qp, kk), -jnp.inf)
                m = jnp.maximum(jnp.max(s, axis=-1, keepdims=True), sink)
                e = jnp.exp(s - m)
                denom = jnp.sum(e, axis=-1, keepdims=True) + jnp.exp(sink - m)
                p = (e * (1.0 / denom)).astype(BF16)
                o = _dot(p, vv)
                out = o if out is None else out + o
            mix_ref[rows, pair * LANES:(pair + 1) * LANES] = out.astype(BF16)

        u = uvg_ref[rows, :GM_WIDTH]
        vgn = _gm_norm(uvg_ref[rows, GM_WIDTH:], lng_ref[...], lnb_ref[...])
        if blk == blocks - 1:
            @pl.when(j == pl.num_programs(1) - 1)
            def _():
                gmv_ref[...] = vgn
        vgb = vgn.astype(BF16)
        gu = _gelu(u)
        for g in range(GM_GROUPS):
            cols = slice(g * GM_DIM, (g + 1) * GM_DIM)
            w = jnp.where(tril, ws_ref[g], 0.0).astype(BF16)
            sm = _dot(w, vgb[:, cols]) + bst_ref[:, g:g + 1]
            mix_ref[rows, ATTN_WIDTH + g * GM_DIM:ATTN_WIDTH + (g + 1) * GM_DIM] = (
                gu[:, cols] * sm).astype(BF16)


def _mix_prompt(sinks, q, kv, uvg, ln_g, ln_b, w_s, b_st, tm):
    b, s, _ = q.shape
    blocks = tm // WINDOW

    def tile(width):
        return pl.BlockSpec((None, tm, width), lambda i, j: (i, j, 0))

    return pl.pallas_call(
        functools.partial(_mix_prompt_kernel, blocks),
        out_shape=(jax.ShapeDtypeStruct((b, s, D_MODEL), BF16),
                   jax.ShapeDtypeStruct((b, CHUNK, GM_WIDTH), F32)),
        grid=(b, s // tm),
        in_specs=[pl.BlockSpec(memory_space=pltpu.SMEM),
                  tile(ATTN_WIDTH), tile(2 * KV_WIDTH),
                  pl.BlockSpec((None, WINDOW, 2 * KV_WIDTH),
                               lambda i, j: (i, jnp.maximum(j * blocks - 1, 0), 0)),
                  tile(2 * GM_WIDTH),
                  _resident(ln_g.shape), _resident(ln_b.shape), _resident(w_s.shape),
                  _resident(b_st.shape)],
        out_specs=(tile(D_MODEL), pl.BlockSpec((None, CHUNK, GM_WIDTH), lambda i, j: (i, 0, 0))),
        compiler_params=pltpu.CompilerParams(
            dimension_semantics=("arbitrary", "arbitrary"), vmem_limit_bytes=VMEM_LIMIT_BYTES),
        name="mix_prompt",
    )(sinks, q, kv, kv, uvg, ln_g, ln_b, w_s, b_st)


def _mix_sample_kernel(q8_ref, kn_ref, vn_ref, kc_ref, vc_ref, sinks_ref, uvg_ref, lng_ref,
                       lnb_ref, w00_ref, b0_ref, o8_ref, gm_ref, gmv_ref, kw_ref, vw_ref):
    q8 = q8_ref[...]
    kc = kc_ref[...]
    vc = vc_ref[...]
    kn = kn_ref[...]
    vn = vn_ref[...]
    sink = sinks_ref[...]
    s_c = jnp.einsum("bqd,bkd->bqk", q8, kc.astype(BF16), preferred_element_type=F32)
    s_n = jnp.sum(q8.astype(F32) * kn.astype(BF16).astype(F32), axis=-1, keepdims=True)
    m = jnp.maximum(jnp.maximum(jnp.max(s_c, axis=-1, keepdims=True), s_n), sink)
    e_c = jnp.exp(s_c - m)
    e_n = jnp.exp(s_n - m)
    inv = 1.0 / (jnp.sum(e_c, axis=-1, keepdims=True) + e_n + jnp.exp(sink - m))
    p_c = (e_c * inv).astype(BF16)
    p_n = (e_n * inv).astype(BF16).astype(F32)
    o = jnp.einsum("bqk,bkd->bqd", p_c, vc.astype(BF16), preferred_element_type=F32)
    o = o + p_n * vn.astype(BF16).astype(F32)
    row = lax.broadcasted_iota(jnp.int32, o.shape, 1)
    lane = lax.broadcasted_iota(jnp.int32, o.shape, 2)
    own = (lane < HEAD_DIM) == (row < N_Q_HEADS // N_KV_HEADS)
    o8_ref[...] = jnp.where(own, o, 0.0).astype(BF16)

    kw_ref[:, :WINDOW - 1, :] = kc[:, 1:, :]
    kw_ref[:, WINDOW - 1:, :] = kn
    vw_ref[:, :WINDOW - 1, :] = vc[:, 1:, :]
    vw_ref[:, WINDOW - 1:, :] = vn

    vgn = _gm_norm(uvg_ref[:, GM_WIDTH:], lng_ref[...], lnb_ref[...])
    gmv_ref[...] = vgn
    sm = w00_ref[...].astype(BF16).astype(F32) * vgn.astype(BF16).astype(F32) + b0_ref[...]
    gm_ref[...] = (_gelu(uvg_ref[:, :GM_WIDTH]) * sm).astype(BF16)


def _mix_sample(q8, kn, vn, kc, vc, sinks, uvg, ln_g, ln_b, w00, b0, bb):
    n = q8.shape[0]

    def b3(d1, d2):
        return pl.BlockSpec((bb, d1, d2), lambda i: (i, 0, 0))

    def b2(d):
        return pl.BlockSpec((bb, d), lambda i: (i, 0))

    return pl.pallas_call(
        _mix_sample_kernel,
        out_shape=(jax.ShapeDtypeStruct((n, N_Q_HEADS, LANES), BF16),
                   jax.ShapeDtypeStruct((n, GM_WIDTH), BF16),
                   jax.ShapeDtypeStruct((n, GM_WIDTH), F32),
                   jax.ShapeDtypeStruct((n, WINDOW, KV_WIDTH), F32),
                   jax.ShapeDtypeStruct((n, WINDOW, KV_WIDTH), F32)),
        grid=(n // bb,),
        in_specs=[b3(N_Q_HEADS, LANES), b3(1, KV_WIDTH), b3(1, KV_WIDTH),
                  b3(WINDOW, KV_WIDTH), b3(WINDOW, KV_WIDTH),
                  _resident(sinks.shape), b2(2 * GM_WIDTH), _resident(ln_g.shape),
                  _resident(ln_b.shape), _resident(w00.shape), _resident(b0.shape)],
        out_specs=(b3(N_Q_HEADS, LANES), b2(GM_WIDTH), b2(GM_WIDTH),
                   b3(WINDOW, KV_WIDTH), b3(WINDOW, KV_WIDTH)),
        compiler_params=pltpu.CompilerParams(
            dimension_semantics=("arbitrary",), vmem_limit_bytes=VMEM_LIMIT_BYTES),
        name="mix_sample",
    )(q8, kn, vn, kc, vc, sinks, uvg, ln_g, ln_b, w00, b0)


def _out_stage_kernel(x1_ref, mix_ref, p_ref, wo_ref, bo_ref, g2_ref, w1_ref, w3_ref, w2_ref,
                      gp_ref, wg_ref, wp_ref, gf_ref, y_ref):
    x2 = x1_ref[...] + _dot(mix_ref[...], wo_ref[...]) + bo_ref[...]
    h = _rms(x2, g2_ref[...]).astype(BF16)
    x3 = x2 + 0.5 * _swiglu(h, w1_ref, w3_ref, w2_ref)
    gate = _sigmoid(_dot(_rms(x3, gp_ref[...]).astype(BF16), wg_ref[...]))
    x4 = x3 + gate * _dot(p_ref[...].astype(BF16), wp_ref[...])
    y_ref[...] = _rms(x4, gf_ref[...])


def _out_stage(x1, mix, p, wo, bo, g2, w1, w3, w2, gp, wg, wp, gf, tm):
    b, s, _ = x1.shape

    def tile(width):
        return pl.BlockSpec((None, tm, width), lambda i, j: (i, j, 0))

    consts = (wo, bo, g2, w1, w3, w2, gp, wg, wp, gf)
    return pl.pallas_call(
        _out_stage_kernel,
        out_shape=jax.ShapeDtypeStruct((b, s, D_MODEL), F32),
        grid=(b, s // tm),
        in_specs=[tile(D_MODEL), tile(mix.shape[-1]), tile(PLE_DIM)]
        + [_resident(c.shape) for c in consts],
        out_specs=tile(D_MODEL),
        compiler_params=pltpu.CompilerParams(
            dimension_semantics=("arbitrary", "arbitrary"), vmem_limit_bytes=VMEM_LIMIT_BYTES),
        name="out_stage",
    )(x1, mix, p, *consts)


PROMPT_TILE = 512
MIX_TILE = 256
SAMPLE_BLOCK = 32


def _pad_heads(w):
    lead = w.shape[:-1]
    wh = w.reshape(lead + (N_Q_HEADS, HEAD_DIM))
    z = jnp.zeros_like(wh[..., :N_Q_HEADS // 2, :])
    first = jnp.concatenate([wh[..., :N_Q_HEADS // 2, :], z], axis=-1)
    second = jnp.concatenate([z, wh[..., N_Q_HEADS // 2:, :]], axis=-1)
    return jnp.concatenate([first, second], axis=-2).reshape(lead + (N_Q_HEADS * LANES,))


def kernel(x_prompt, x_sample, cache_k_win, cache_v_win, p_prompt, p_sample, g_ffn1, w1_ffn1, w3_ffn1, w2_ffn1, g_mix, w_in, b_in, attn_sinks, gm_ln_g, gm_ln_b, gm_w_s, gm_b_s, w_out, b_out, g_ffn2, w1_ffn2, w3_ffn2, w2_ffn2, g_ple, w_ple_gate, w_ple_proj, g_final):
    depth = g_ffn1.shape[0]
    assert depth == 1
    i = 0
    nb, seq, _ = x_prompt.shape
    nd = x_sample.shape[0]

    row = lambda a: a.reshape(1, -1)
    g1, gmx, g2, gp, gf = row(g_ffn1[i]), row(g_mix[i]), row(g_ffn2[i]), row(g_ple[i]), row(g_final)
    w1a, w3a, w2a = w1_ffn1[i].astype(BF16), w3_ffn1[i].astype(BF16), w2_ffn1[i].astype(BF16)
    w1b, w3b, w2b = w1_ffn2[i].astype(BF16), w3_ffn2[i].astype(BF16), w2_ffn2[i].astype(BF16)
    wq = w_in[i][:, :ATTN_WIDTH].astype(BF16)
    wr = w_in[i][:, ATTN_WIDTH:].astype(BF16)
    bq, br = row(b_in[i][:ATTN_WIDTH]), row(b_in[i][ATTN_WIDTH:])
    wo, bo = w_out[i].astype(BF16), row(b_out[i])
    wg, wp = w_ple_gate[i].astype(BF16), w_ple_proj[i].astype(BF16)
    ln_g, ln_b = row(gm_ln_g[i]), row(gm_ln_b[i])
    sinks = attn_sinks[i]

    x1, q, kv, uvg = _in_stage(x_prompt, g1, w1a, w3a, w2a, gmx, wq, wr, bq, br, PROMPT_TILE)
    mix, gmv_p = _mix_prompt(sinks, q, kv, uvg, ln_g, ln_b, gm_w_s[i], gm_b_s[i].T, MIX_TILE)
    y_prompt = _out_stage(x1, mix, p_prompt[i], wo, bo, g2, w1b, w3b, w2b, gp, wg, wp, gf,
                          PROMPT_TILE)
    kv_last = kv[:, seq - WINDOW:, :]
    k_win_p = kv_last[:, :, :KV_WIDTH].reshape(1, nb, WINDOW, N_KV_HEADS, HEAD_DIM)
    v_win_p = kv_last[:, :, KV_WIDTH:].reshape(1, nb, WINDOW, N_KV_HEADS, HEAD_DIM)

    wq8, bq8 = _pad_heads(w_in[i][:, :ATTN_WIDTH]).astype(BF16), row(_pad_heads(b_in[i][:ATTN_WIDTH]))
    wo8 = jnp.concatenate([_pad_heads(w_out[i][:ATTN_WIDTH].T).T, w_out[i][ATTN_WIDTH:]],
                          axis=0).astype(BF16)
    xs = x_sample.reshape(1, nd, D_MODEL)
    x1s, q8, kvs, uvgs = _in_stage(xs, g1, w1a, w3a, w2a, gmx, wq8, wr, bq8, br, nd)
    kvs = kvs.reshape(nd, 1, 2 * KV_WIDTH)
    o8, gm_s, gmv_s, kw_s, vw_s = _mix_sample(
        q8.reshape(nd, N_Q_HEADS, LANES), kvs[:, :, :KV_WIDTH], kvs[:, :, KV_WIDTH:],
        cache_k_win[i].reshape(nd, WINDOW, KV_WIDTH), cache_v_win[i].reshape(nd, WINDOW, KV_WIDTH),
        sinks.reshape(N_Q_HEADS, 1), uvgs.reshape(nd, 2 * GM_WIDTH), ln_g, ln_b,
        jnp.repeat(gm_w_s[i][:, 0, 0], GM_DIM).reshape(1, GM_WIDTH),
        jnp.repeat(gm_b_s[i][:, 0], GM_DIM).reshape(1, GM_WIDTH), SAMPLE_BLOCK)
    mix_s = jnp.concatenate([o8.reshape(nd, N_Q_HEADS * LANES), gm_s], axis=-1)
    y_sample = _out_stage(x1s, mix_s.reshape(1, nd, -1), p_sample[i].reshape(1, nd, PLE_DIM),
                          wo8, bo, g2, w1b, w3b, w2b, gp, wg, wp, gf, nd)

    return (y_prompt, y_sample.reshape(nd, 1, D_MODEL), k_win_p, v_win_p,
            kw_s.reshape(1, nd, WINDOW, N_KV_HEADS, HEAD_DIM),
            vw_s.reshape(1, nd, WINDOW, N_KV_HEADS, HEAD_DIM),
            gmv_p.reshape(1, nb, CHUNK, GM_WIDTH), gmv_s.reshape(1, nd, 1, GM_WIDTH))
```

```python
import functools
import math

import jax
import jax.numpy as jnp
from jax import lax
from jax.experimental import pallas as pl
from jax.experimental.pallas import tpu as pltpu

F32 = jnp.float32
BF16 = jnp.bfloat16

D_MODEL = 1024
HEAD_DIM = 64
N_Q_HEADS = 8
N_KV_HEADS = 2
ATTN_WIDTH = N_Q_HEADS * HEAD_DIM
KV_WIDTH = N_KV_HEADS * HEAD_DIM
WINDOW = 128
GM_WIDTH = 512
GM_GROUPS = 4
GM_DIM = 128
CHUNK = 128
D_FF = 2816
PLE_DIM = 256
RMS_EPS = 1e-6
LN_EPS = 1e-5
ATTN_SCALE = HEAD_DIM ** -0.5

LANES = 128
FF_CHUNK = 256
N_FF_CHUNKS = D_FF // FF_CHUNK
VMEM_LIMIT_BYTES = 56 * 1024 * 1024

_GELU_C = math.sqrt(2.0 / math.pi)


def _rms(x, g):
    return x * lax.rsqrt(jnp.mean(x * x, axis=-1, keepdims=True) + RMS_EPS) * g


def _sigmoid(x):
    return 1.0 / (1.0 + jnp.exp(-x))


def _gelu(x):
    return 0.5 * x * (1.0 + jnp.tanh(_GELU_C * (x + 0.044715 * (x * x * x))))


def _dot(a, b):
    return jnp.dot(a, b, preferred_element_type=F32)


def _dot_nt(a, b):
    return lax.dot_general(a, b, (((1,), (1,)), ((), ())), preferred_element_type=F32)


def _swiglu(h, w1_ref, w3_ref, w2_ref, before_chunk=None):
    acc = jnp.zeros((h.shape[0], D_MODEL), F32)
    for c in range(N_FF_CHUNKS):
        if before_chunk is not None:
            before_chunk(c)
        cols = slice(c * FF_CHUNK, (c + 1) * FF_CHUNK)
        a = _dot(h, w1_ref[:, cols])
        b = _dot(h, w3_ref[:, cols])
        act = (a * _sigmoid(a) * b).astype(BF16)
        acc = acc + _dot(act, w2_ref[cols, :])
    return acc


def _resident(shape):
    zeros = (0,) * len(shape)
    return pl.BlockSpec(shape, lambda *_: zeros, pipeline_mode=pl.Buffered(1))


def _in_stage_kernel(x_ref, g1_ref, w1_ref, w3_ref, w2_ref, gm_ref, wq_ref, wr_ref,
                     bq_ref, br_ref, x1_ref, q_ref, kv_ref, uvg_ref):
    x = x_ref[...]
    h = _rms(x, g1_ref[...]).astype(BF16)
    x1 = x + 0.5 * _swiglu(h, w1_ref, w3_ref, w2_ref)
    x1_ref[...] = x1
    h2 = _rms(x1, gm_ref[...]).astype(BF16)
    q_ref[...] = ((_dot(h2, wq_ref[...]) + bq_ref[...]) * ATTN_SCALE).astype(BF16)
    kv_ref[...] = _dot(h2, wr_ref[:, :2 * KV_WIDTH]) + br_ref[:, :2 * KV_WIDTH]
    uvg_ref[...] = _dot(h2, wr_ref[:, 2 * KV_WIDTH:]) + br_ref[:, 2 * KV_WIDTH:]


def _in_stage(x, g1, w1, w3, w2, gmix, wq, wr, bq, br, tm):
    b, s, _ = x.shape
    qw = wq.shape[1]
    rw = 2 * GM_WIDTH

    def tile(width):
        return pl.BlockSpec((None, tm, width), lambda i, j: (i, j, 0))

    consts = (g1, w1, w3, w2, gmix, wq, wr, bq, br)
    return pl.pallas_call(
        _in_stage_kernel,
        out_shape=(jax.ShapeDtypeStruct((b, s, D_MODEL), F32),
                   jax.ShapeDtypeStruct((b, s, qw), BF16),
                   jax.ShapeDtypeStruct((b, s, 2 * KV_WIDTH), F32),
                   jax.ShapeDtypeStruct((b, s, rw), F32)),
        grid=(b, s // tm),
        in_specs=[tile(D_MODEL)] + [_resident(c.shape) for c in consts],
        out_specs=(tile(D_MODEL), tile(qw), tile(2 * KV_WIDTH), tile(rw)),
        compiler_params=pltpu.CompilerParams(
            dimension_semantics=("arbitrary", "arbitrary"), vmem_limit_bytes=VMEM_LIMIT_BYTES),
        name="in_stage",
    )(x, *consts)


def _gm_norm(vg, ln_g, ln_b):
    a = _gelu(vg)
    mu = jnp.mean(a, axis=-1, keepdims=True)
    ac = a - mu
    return ac * lax.rsqrt(jnp.mean(ac * ac, axis=-1, keepdims=True) + LN_EPS) * ln_g + ln_b


def _mix_units(blocks, first_of_seq, sinks_ref, q_ref, kv_ref, kvp_ref, uvg_ref, lng_ref,
               lnb_ref, ws_ref, bst_ref, mix_ref, gmv_ref):
    lane = lax.broadcasted_iota(jnp.int32, (2 * WINDOW, LANES), 1)
    lo = lane < HEAD_DIM
    qi = lax.broadcasted_iota(jnp.int32, (WINDOW, 2 * WINDOW), 0)
    kj = lax.broadcasted_iota(jnp.int32, (WINDOW, 2 * WINDOW), 1)
    band = (kj >= qi) & (kj <= qi + WINDOW)
    ti = lax.broadcasted_iota(jnp.int32, (CHUNK, CHUNK), 0)
    si = lax.broadcasted_iota(jnp.int32, (CHUNK, CHUNK), 1)
    tril = si <= ti

    block_cache = {}

    def block_operands(blk):
        if blk not in block_cache:
            if blk == 0:
                kv2 = jnp.concatenate([kvp_ref[...], kv_ref[:WINDOW, :]], axis=0)
                mask = band & (kj >= jnp.where(first_of_seq, WINDOW, 0))
            else:
                kv2 = kv_ref[(blk - 1) * WINDOW:(blk + 1) * WINDOW, :]
                mask = band
            k2 = kv2[:, :KV_WIDTH].astype(BF16)
            v2 = kv2[:, KV_WIDTH:].astype(BF16)
            k2r = pltpu.roll(k2, HEAD_DIM, 1)
            v2r = pltpu.roll(v2, HEAD_DIM, 1)
            zero = jnp.zeros_like(k2)
            k_lo = (jnp.where(lo, k2, zero), jnp.where(lo, k2r, zero))
            k_hi = (jnp.where(lo, zero, k2r), jnp.where(lo, zero, k2))
            v_lo = (jnp.where(lo, v2, zero), jnp.where(lo, v2r, zero))
            v_hi = (jnp.where(lo, zero, v2r), jnp.where(lo, zero, v2))
            block_cache[blk] = (mask, k_lo, k_hi, v_lo, v_hi)
        return block_cache[blk]

    def attn_unit(blk, pair):
        rows = slice(blk * WINDOW, (blk + 1) * WINDOW)
        mask, k_lo, k_hi, v_lo, v_hi = block_operands(blk)
        h = pair // 2
        qp = q_ref[rows, pair * LANES:(pair + 1) * LANES]
        scores = [_dot_nt(qp, kk) for kk in (k_lo[h], k_hi[h])]
        yield
        out = None
        for half, vv in enumerate((v_lo[h], v_hi[h])):
            sink = sinks_ref[2 * pair + half]
            s = jnp.where(mask, scores[half], -jnp.inf)
            m = jnp.maximum(jnp.max(s, axis=-1, keepdims=True), sink)
            e = jnp.exp(s - m)
            denom = jnp.sum(e, axis=-1, keepdims=True) + jnp.exp(sink - m)
            p = (e * (1.0 / denom)).astype(BF16)
            o = _dot(p, vv)
            out = o if out is None else out + o
        mix_ref[rows, pair * LANES:(pair + 1) * LANES] = out.astype(BF16)

    def gm_unit(blk):
        rows = slice(blk * WINDOW, (blk + 1) * WINDOW)
        vgn = _gm_norm(uvg_ref[rows, GM_WIDTH:], lng_ref[...], lnb_ref[...])
        if blk == blocks - 1:
            gmv_ref[...] = vgn
        vgb = vgn.astype(BF16)
        gu = _gelu(uvg_ref[rows, :GM_WIDTH])
        yield
        for g in range(GM_GROUPS):
            cols = slice(g * GM_DIM, (g + 1) * GM_DIM)
            w = jnp.where(tril, ws_ref[g], 0.0).astype(BF16)
            sm = _dot(w, vgb[:, cols]) + bst_ref[:, g:g + 1]
            mix_ref[rows, ATTN_WIDTH + g * GM_DIM:ATTN_WIDTH + (g + 1) * GM_DIM] = (
                gu[:, cols] * sm).astype(BF16)

    units = []
    for blk in range(blocks):
        units += [attn_unit(blk, pair) for pair in range(N_Q_HEADS // 2)] + [gm_unit(blk)]
    return units


def _in_mix_kernel(tiles_per_seq, blocks, x_ref, g1_ref, w1_ref, w3_ref, w2_ref, gm_ref, wq_ref,
                   wr_ref, bq_ref, br_ref, sinks_ref, lng_ref, lnb_ref, ws_ref, bst_ref,
                   x1_ref, mix_ref, kvw_ref, gmv_ref, q_s, kv_s, uvg_s, kvp_s):
    t = pl.program_id(0)

    @pl.when(t == 0)
    def _():
        q_s[...] = jnp.zeros_like(q_s)
        kv_s[...] = jnp.zeros_like(kv_s)
        uvg_s[...] = jnp.zeros_like(uvg_s)
        kvp_s[...] = jnp.zeros_like(kvp_s)

    first_of_seq = (jnp.maximum(t - 1, 0) % tiles_per_seq) == 0
    units = _mix_units(blocks, first_of_seq, sinks_ref, q_s, kv_s, kvp_s, uvg_s, lng_ref, lnb_ref,
                       ws_ref, bst_ref, mix_ref, gmv_ref)
    in_flight = []

    def before_chunk(c):
        for unit in in_flight:
            next(unit, None)
        first, last = c * len(units) // N_FF_CHUNKS, (c + 1) * len(units) // N_FF_CHUNKS
        in_flight[:] = units[first:last]
        for unit in in_flight:
            next(unit)

    x = x_ref[...]
    h = _rms(x, g1_ref[...]).astype(BF16)
    x1 = x + 0.5 * _swiglu(h, w1_ref, w3_ref, w2_ref, before_chunk)
    for unit in in_flight:
        next(unit, None)
    kvp_s[...] = kv_s[(blocks - 1) * WINDOW:, :]
    x1_ref[...] = x1
    h2 = _rms(x1, gm_ref[...]).astype(BF16)
    q_s[...] = ((_dot(h2, wq_ref[...]) + bq_ref[...]) * ATTN_SCALE).astype(BF16)
    kv = _dot(h2, wr_ref[:, :2 * KV_WIDTH]) + br_ref[:, :2 * KV_WIDTH]
    kv_s[...] = kv
    kvw_ref[...] = kv[(blocks - 1) * WINDOW:, :]
    uvg_s[...] = _dot(h2, wr_ref[:, 2 * KV_WIDTH:]) + br_ref[:, 2 * KV_WIDTH:]


def _in_mix(x, g1, w1, w3, w2, gmix, wq, wr, bq, br, sinks, ln_g, ln_b, w_s, b_st, tm):
    b, s, _ = x.shape
    nj = s // tm
    nt = b * nj
    blocks = tm // WINDOW

    def cur(t):
        u = jnp.minimum(t, nt - 1)
        return u // nj, u % nj

    def prev(t):
        u = jnp.maximum(t - 1, 0)
        return u // nj, u % nj

    def tile(width, which):
        return pl.BlockSpec((None, tm, width), lambda t: which(t) + (0,))

    def per_seq(rows, width, which):
        return pl.BlockSpec((None, rows, width), lambda t: (which(t)[0], 0, 0))

    consts = (g1, w1, w3, w2, gmix, wq, wr, bq, br)
    gconsts = (ln_g, ln_b, w_s, b_st)
    return pl.pallas_call(
        functools.partial(_in_mix_kernel, nj, blocks),
        out_shape=(jax.ShapeDtypeStruct((b, s, D_MODEL), F32),
                   jax.ShapeDtypeStruct((b, s, D_MODEL), BF16),
                   jax.ShapeDtypeStruct((b, WINDOW, 2 * KV_WIDTH), F32),
                   jax.ShapeDtypeStruct((b, CHUNK, GM_WIDTH), F32)),
        grid=(nt + 1,),
        in_specs=[tile(D_MODEL, cur)] + [_resident(c.shape) for c in consts]
        + [pl.BlockSpec(memory_space=pltpu.SMEM)] + [_resident(c.shape) for c in gconsts],
        out_specs=(tile(D_MODEL, cur), tile(D_MODEL, prev),
                   per_seq(WINDOW, 2 * KV_WIDTH, cur), per_seq(CHUNK, GM_WIDTH, prev)),
        scratch_shapes=[pltpu.VMEM((tm, ATTN_WIDTH), BF16), pltpu.VMEM((tm, 2 * KV_WIDTH), F32),
                        pltpu.VMEM((tm, 2 * GM_WIDTH), F32),
                        pltpu.VMEM((WINDOW, 2 * KV_WIDTH), F32)],
        compiler_params=pltpu.CompilerParams(
            dimension_semantics=("arbitrary",), vmem_limit_bytes=VMEM_LIMIT_BYTES),
        name="in_mix",
    )(x, *consts, sinks, *gconsts)


def _mix_sample_kernel(q8_ref, kn_ref, vn_ref, kc_ref, vc_ref, sinks_ref, uvg_ref, lng_ref,
                       lnb_ref, w00_ref, b0_ref, o8_ref, gm_ref, gmv_ref, kw_ref, vw_ref):
    q8 = q8_ref[...]
    kc = kc_ref[...]
    vc = vc_ref[...]
    kn = kn_ref[...]
    vn = vn_ref[...]
    sink = sinks_ref[...]
    s_c = jnp.einsum("bqd,bkd->bqk", q8, kc.astype(BF16), preferred_element_type=F32)
    s_n = jnp.sum(q8.astype(F32) * kn.astype(BF16).astype(F32), axis=-1, keepdims=True)
    m = jnp.maximum(jnp.maximum(jnp.max(s_c, axis=-1, keepdims=True), s_n), sink)
    e_c = jnp.exp(s_c - m)
    e_n = jnp.exp(s_n - m)
    inv = 1.0 / (jnp.sum(e_c, axis=-1, keepdims=True) + e_n + jnp.exp(sink - m))
    p_c = (e_c * inv).astype(BF16)
    p_n = (e_n * inv).astype(BF16).astype(F32)
    o = jnp.einsum("bqk,bkd->bqd", p_c, vc.astype(BF16), preferred_element_type=F32)
    o = o + p_n * vn.astype(BF16).astype(F32)
    row = lax.broadcasted_iota(jnp.int32, o.shape, 1)
    lane = lax.broadcasted_iota(jnp.int32, o.shape, 2)
    own = (lane < HEAD_DIM) == (row < N_Q_HEADS // N_KV_HEADS)
    o8_ref[...] = jnp.where(own, o, 0.0).astype(BF16)

    kw_ref[:, :WINDOW - 1, :] = kc[:, 1:, :]
    kw_ref[:, WINDOW - 1:, :] = kn
    vw_ref[:, :WINDOW - 1, :] = vc[:, 1:, :]
    vw_ref[:, WINDOW - 1:, :] = vn

    vgn = _gm_norm(uvg_ref[:, GM_WIDTH:], lng_ref[...], lnb_ref[...])
    gmv_ref[...] = vgn
    sm = w00_ref[...].astype(BF16).astype(F32) * vgn.astype(BF16).astype(F32) + b0_ref[...]
    gm_ref[...] = (_gelu(uvg_ref[:, :GM_WIDTH]) * sm).astype(BF16)


def _mix_sample(q8, kn, vn, kc, vc, sinks, uvg, ln_g, ln_b, w00, b0, bb):
    n = q8.shape[0]

    def b3(d1, d2):
        return pl.BlockSpec((bb, d1, d2), lambda i: (i, 0, 0))

    def b2(d):
        return pl.BlockSpec((bb, d), lambda i: (i, 0))

    return pl.pallas_call(
        _mix_sample_kernel,
        out_shape=(jax.ShapeDtypeStruct((n, N_Q_HEADS, LANES), BF16),
                   jax.ShapeDtypeStruct((n, GM_WIDTH), BF16),
                   jax.ShapeDtypeStruct((n, GM_WIDTH), F32),
                   jax.ShapeDtypeStruct((n, WINDOW, KV_WIDTH), F32),
                   jax.ShapeDtypeStruct((n, WINDOW, KV_WIDTH), F32)),
        grid=(n // bb,),
        in_specs=[b3(N_Q_HEADS, LANES), b3(1, KV_WIDTH), b3(1, KV_WIDTH),
                  b3(WINDOW, KV_WIDTH), b3(WINDOW, KV_WIDTH),
                  _resident(sinks.shape), b2(2 * GM_WIDTH), _resident(ln_g.shape),
                  _resident(ln_b.shape), _resident(w00.shape), _resident(b0.shape)],
        out_specs=(b3(N_Q_HEADS, LANES), b2(GM_WIDTH), b2(GM_WIDTH),
                   b3(WINDOW, KV_WIDTH), b3(WINDOW, KV_WIDTH)),
        compiler_params=pltpu.CompilerParams(
            dimension_semantics=("arbitrary",), vmem_limit_bytes=VMEM_LIMIT_BYTES),
        name="mix_sample",
    )(q8, kn, vn, kc, vc, sinks, uvg, ln_g, ln_b, w00, b0)


def _out_stage_kernel(x1_ref, mix_ref, p_ref, wo_ref, bo_ref, g2_ref, w1_ref, w3_ref, w2_ref,
                      gp_ref, wg_ref, wp_ref, gf_ref, y_ref):
    x2 = x1_ref[...] + _dot(mix_ref[...], wo_ref[...]) + bo_ref[...]
    h = _rms(x2, g2_ref[...]).astype(BF16)
    x3 = x2 + 0.5 * _swiglu(h, w1_ref, w3_ref, w2_ref)
    gate = _sigmoid(_dot(_rms(x3, gp_ref[...]).astype(BF16), wg_ref[...]))
    x4 = x3 + gate * _dot(p_ref[...].astype(BF16), wp_ref[...])
    y_ref[...] = _rms(x4, gf_ref[...])


def _out_stage(x1, mix, p, wo, bo, g2, w1, w3, w2, gp, wg, wp, gf, tm):
    b, s, _ = x1.shape

    def tile(width):
        return pl.BlockSpec((None, tm, width), lambda i, j: (i, j, 0))

    consts = (wo, bo, g2, w1, w3, w2, gp, wg, wp, gf)
    return pl.pallas_call(
        _out_stage_kernel,
        out_shape=jax.ShapeDtypeStruct((b, s, D_MODEL), F32),
        grid=(b, s // tm),
        in_specs=[tile(D_MODEL), tile(mix.shape[-1]), tile(PLE_DIM)]
        + [_resident(c.shape) for c in consts],
        out_specs=tile(D_MODEL),
        compiler_params=pltpu.CompilerParams(
            dimension_semantics=("arbitrary", "arbitrary"), vmem_limit_bytes=VMEM_LIMIT_BYTES),
        name="out_stage",
    )(x1, mix, p, *consts)


PROMPT_TILE = 512
SAMPLE_BLOCK = 32


def _pad_heads(w):
    lead = w.shape[:-1]
    wh = w.reshape(lead + (N_Q_HEADS, HEAD_DIM))
    z = jnp.zeros_like(wh[..., :N_Q_HEADS // 2, :])
    first = jnp.concatenate([wh[..., :N_Q_HEADS // 2, :], z], axis=-1)
    second = jnp.concatenate([z, wh[..., N_Q_HEADS // 2:, :]], axis=-1)
    return jnp.concatenate([first, second], axis=-2).reshape(lead + (N_Q_HEADS * LANES,))


def kernel(x_prompt, x_sample, cache_k_win, cache_v_win, p_prompt, p_sample, g_ffn1, w1_ffn1, w3_ffn1, w2_ffn1, g_mix, w_in, b_in, attn_sinks, gm_ln_g, gm_ln_b, gm_w_s, gm_b_s, w_out, b_out, g_ffn2, w1_ffn2, w3_ffn2, w2_ffn2, g_ple, w_ple_gate, w_ple_proj, g_final):
    depth = g_ffn1.shape[0]
    assert depth == 1
    i = 0
    nb, seq, _ = x_prompt.shape
    nd = x_sample.shape[0]

    row = lambda a: a.reshape(1, -1)
    g1, gmx, g2, gp, gf = row(g_ffn1[i]), row(g_mix[i]), row(g_ffn2[i]), row(g_ple[i]), row(g_final)
    w1a, w3a, w2a = w1_ffn1[i].astype(BF16), w3_ffn1[i].astype(BF16), w2_ffn1[i].astype(BF16)
    w1b, w3b, w2b = w1_ffn2[i].astype(BF16), w3_ffn2[i].astype(BF16), w2_ffn2[i].astype(BF16)
    wq = w_in[i][:, :ATTN_WIDTH].astype(BF16)
    wr = w_in[i][:, ATTN_WIDTH:].astype(BF16)
    bq, br = row(b_in[i][:ATTN_WIDTH]), row(b_in[i][ATTN_WIDTH:])
    wo, bo = w_out[i].astype(BF16), row(b_out[i])
    wg, wp = w_ple_gate[i].astype(BF16), w_ple_proj[i].astype(BF16)
    ln_g, ln_b = row(gm_ln_g[i]), row(gm_ln_b[i])
    sinks = attn_sinks[i]

    x1, mix, kv_last, gmv_p = _in_mix(x_prompt, g1, w1a, w3a, w2a, gmx, wq, wr, bq, br, sinks,
                                      ln_g, ln_b, gm_w_s[i], gm_b_s[i].T, PROMPT_TILE)
    y_prompt = _out_stage(x1, mix, p_prompt[i], wo, bo, g2, w1b, w3b, w2b, gp, wg, wp, gf,
                          PROMPT_TILE)
    k_win_p = kv_last[:, :, :KV_WIDTH].reshape(1, nb, WINDOW, N_KV_HEADS, HEAD_DIM)
    v_win_p = kv_last[:, :, KV_WIDTH:].reshape(1, nb, WINDOW, N_KV_HEADS, HEAD_DIM)

    wq8, bq8 = _pad_heads(w_in[i][:, :ATTN_WIDTH]).astype(BF16), row(_pad_heads(b_in[i][:ATTN_WIDTH]))
    wo8 = jnp.concatenate([_pad_heads(w_out[i][:ATTN_WIDTH].T).T, w_out[i][ATTN_WIDTH:]],
                          axis=0).astype(BF16)
    xs = x_sample.reshape(1, nd, D_MODEL)
    x1s, q8, kvs, uvgs = _in_stage(xs, g1, w1a, w3a, w2a, gmx, wq8, wr, bq8, br, nd)
    kvs = kvs.reshape(nd, 1, 2 * KV_WIDTH)
    o8, gm_s, gmv_s, kw_s, vw_s = _mix_sample(
        q8.reshape(nd, N_Q_HEADS, LANES), kvs[:, :, :KV_WIDTH], kvs[:, :, KV_WIDTH:],
        cache_k_win[i].reshape(nd, WINDOW, KV_WIDTH), cache_v_win[i].reshape(nd, WINDOW, KV_WIDTH),
        sinks.reshape(N_Q_HEADS, 1), uvgs.reshape(nd, 2 * GM_WIDTH), ln_g, ln_b,
        jnp.repeat(gm_w_s[i][:, 0, 0], GM_DIM).reshape(1, GM_WIDTH),
        jnp.repeat(gm_b_s[i][:, 0], GM_DIM).reshape(1, GM_WIDTH), SAMPLE_BLOCK)
    mix_s = jnp.concatenate([o8.reshape(nd, N_Q_HEADS * LANES), gm_s], axis=-1)
    y_sample = _out_stage(x1s, mix_s.reshape(1, nd, -1), p_sample[i].reshape(1, nd, PLE_DIM),
                          wo8, bo, g2, w1b, w3b, w2b, gp, wg, wp, gf, nd)

    return (y_prompt, y_sample.reshape(nd, 1, D_MODEL), k_win_p, v_win_p,
            kw_s.reshape(1, nd, WINDOW, N_KV_HEADS, HEAD_DIM),
            vw_s.reshape(1, nd, WINDOW, N_KV_HEADS, HEAD_DIM),
            gmv_p.reshape(1, nb, CHUNK, GM_WIDTH), gmv_s.reshape(1, nd, 1, GM_WIDTH))
```

```python
import functools
import math

import jax
import jax.numpy as jnp
from jax import lax
from jax.experimental import pallas as pl
from jax.experimental.pallas import tpu as pltpu

F32 = jnp.float32
BF16 = jnp.bfloat16

D_MODEL = 1024
HEAD_DIM = 64
N_Q_HEADS = 8
N_KV_HEADS = 2
ATTN_WIDTH = N_Q_HEADS * HEAD_DIM
KV_WIDTH = N_KV_HEADS * HEAD_DIM
WINDOW = 128
GM_WIDTH = 512
GM_GROUPS = 4
GM_DIM = 128
CHUNK = 128
D_FF = 2816
PLE_DIM = 256
RMS_EPS = 1e-6
LN_EPS = 1e-5
ATTN_SCALE = HEAD_DIM ** -0.5

LANES = 128
FF_CHUNK = 256
N_FF_CHUNKS = D_FF // FF_CHUNK
VMEM_LIMIT_BYTES = 56 * 1024 * 1024

_GELU_C = math.sqrt(2.0 / math.pi)


def _rms(x, g):
    return x * lax.rsqrt(jnp.mean(x * x, axis=-1, keepdims=True) + RMS_EPS) * g


def _sigmoid(x):
    return 1.0 / (1.0 + jnp.exp(-x))


def _gelu(x):
    return 0.5 * x * (1.0 + jnp.tanh(_GELU_C * (x + 0.044715 * (x * x * x))))


def _dot(a, b):
    return jnp.dot(a, b, preferred_element_type=F32)


def _dot_nt(a, b):
    return lax.dot_general(a, b, (((1,), (1,)), ((), ())), preferred_element_type=F32)


def _swiglu(h, w1_ref, w3_ref, w2_ref, before_chunk=None):
    acc = jnp.zeros((h.shape[0], D_MODEL), F32)
    for c in range(N_FF_CHUNKS):
        if before_chunk is not None:
            before_chunk(c)
        cols = slice(c * FF_CHUNK, (c + 1) * FF_CHUNK)
        a = _dot(h, w1_ref[:, cols])
        b = _dot(h, w3_ref[:, cols])
        act = (a * _sigmoid(a) * b).astype(BF16)
        acc = acc + _dot(act, w2_ref[cols, :])
    return acc


def _resident(shape):
    zeros = (0,) * len(shape)
    return pl.BlockSpec(shape, lambda *_: zeros, pipeline_mode=pl.Buffered(1))


def _in_stage_kernel(x_ref, g1_ref, w1_ref, w3_ref, w2_ref, gm_ref, wq_ref, wr_ref,
                     bq_ref, br_ref, x1_ref, q_ref, kv_ref, uvg_ref):
    x = x_ref[...]
    h = _rms(x, g1_ref[...]).astype(BF16)
    x1 = x + 0.5 * _swiglu(h, w1_ref, w3_ref, w2_ref)
    x1_ref[...] = x1
    h2 = _rms(x1, gm_ref[...]).astype(BF16)
    q_ref[...] = ((_dot(h2, wq_ref[...]) + bq_ref[...]) * ATTN_SCALE).astype(BF16)
    kv_ref[...] = _dot(h2, wr_ref[:, :2 * KV_WIDTH]) + br_ref[:, :2 * KV_WIDTH]
    uvg_ref[...] = _dot(h2, wr_ref[:, 2 * KV_WIDTH:]) + br_ref[:, 2 * KV_WIDTH:]


def _in_stage(x, g1, w1, w3, w2, gmix, wq, wr, bq, br, tm):
    b, s, _ = x.shape
    qw = wq.shape[1]
    rw = 2 * GM_WIDTH

    def tile(width):
        return pl.BlockSpec((None, tm, width), lambda i, j: (i, j, 0))

    consts = (g1, w1, w3, w2, gmix, wq, wr, bq, br)
    return pl.pallas_call(
        _in_stage_kernel,
        out_shape=(jax.ShapeDtypeStruct((b, s, D_MODEL), F32),
                   jax.ShapeDtypeStruct((b, s, qw), BF16),
                   jax.ShapeDtypeStruct((b, s, 2 * KV_WIDTH), F32),
                   jax.ShapeDtypeStruct((b, s, rw), F32)),
        grid=(b, s // tm),
        in_specs=[tile(D_MODEL)] + [_resident(c.shape) for c in consts],
        out_specs=(tile(D_MODEL), tile(qw), tile(2 * KV_WIDTH), tile(rw)),
        compiler_params=pltpu.CompilerParams(
            dimension_semantics=("arbitrary", "arbitrary"), vmem_limit_bytes=VMEM_LIMIT_BYTES),
        name="in_stage",
    )(x, *consts)


def _gm_norm(vg, ln_g, ln_b):
    a = _gelu(vg)
    mu = jnp.mean(a, axis=-1, keepdims=True)
    ac = a - mu
    return ac * lax.rsqrt(jnp.mean(ac * ac, axis=-1, keepdims=True) + LN_EPS) * ln_g + ln_b


def _mix_units(blocks, first_of_seq, sinks_ref, q_ref, kv_ref, kvp_ref, uvg_ref, lng_ref,
               lnb_ref, ws_ref, bst_ref, mix_ref, gmv_ref):
    lane = lax.broadcasted_iota(jnp.int32, (2 * WINDOW, LANES), 1)
    lo = lane < HEAD_DIM
    qi = lax.broadcasted_iota(jnp.int32, (WINDOW, 2 * WINDOW), 0)
    kj = lax.broadcasted_iota(jnp.int32, (WINDOW, 2 * WINDOW), 1)
    band = (kj >= qi) & (kj <= qi + WINDOW)
    ti = lax.broadcasted_iota(jnp.int32, (CHUNK, CHUNK), 0)
    si = lax.broadcasted_iota(jnp.int32, (CHUNK, CHUNK), 1)
    tril = si <= ti

    block_cache = {}

    def block_operands(blk):
        if blk not in block_cache:
            if blk == 0:
                kv2 = jnp.concatenate([kvp_ref[...], kv_ref[:WINDOW, :]], axis=0)
                mask = band & (kj >= jnp.where(first_of_seq, WINDOW, 0))
            else:
                kv2 = kv_ref[(blk - 1) * WINDOW:(blk + 1) * WINDOW, :]
                mask = band
            k2 = kv2[:, :KV_WIDTH].astype(BF16)
            v2 = kv2[:, KV_WIDTH:].astype(BF16)
            k2r = pltpu.roll(k2, HEAD_DIM, 1)
            v2r = pltpu.roll(v2, HEAD_DIM, 1)
            zero = jnp.zeros_like(k2)
            k_lo = (jnp.where(lo, k2, zero), jnp.where(lo, k2r, zero))
            k_hi = (jnp.where(lo, zero, k2r), jnp.where(lo, zero, k2))
            v_lo = (jnp.where(lo, v2, zero), jnp.where(lo, v2r, zero))
            v_hi = (jnp.where(lo, zero, v2r), jnp.where(lo, zero, v2))
            block_cache[blk] = (mask, k_lo, k_hi, v_lo, v_hi)
        return block_cache[blk]

    def attn_unit(blk, pair):
        rows = slice(blk * WINDOW, (blk + 1) * WINDOW)
        mask, k_lo, k_hi, v_lo, v_hi = block_operands(blk)
        h = pair // 2
        qp = q_ref[rows, pair * LANES:(pair + 1) * LANES]
        scores = [_dot_nt(qp, kk) for kk in (k_lo[h], k_hi[h])]
        yield
        out = None
        for half, vv in enumerate((v_lo[h], v_hi[h])):
            sink = sinks_ref[2 * pair + half]
            s = jnp.where(mask, scores[half], -jnp.inf)
            m = jnp.maximum(jnp.max(s, axis=-1, keepdims=True), sink)
            e = jnp.exp(s - m)
            denom = jnp.sum(e, axis=-1, keepdims=True) + jnp.exp(sink - m)
            p = (e * (1.0 / denom)).astype(BF16)
            o = _dot(p, vv)
            out = o if out is None else out + o
        mix_ref[rows, pair * LANES:(pair + 1) * LANES] = out.astype(BF16)

    def gm_unit(blk):
        rows = slice(blk * WINDOW, (blk + 1) * WINDOW)
        vgn = _gm_norm(uvg_ref[rows, GM_WIDTH:], lng_ref[...], lnb_ref[...])
        if blk == blocks - 1:
            gmv_ref[...] = vgn
        vgb = vgn.astype(BF16)
        gu = _gelu(uvg_ref[rows, :GM_WIDTH])
        yield
        for g in range(GM_GROUPS):
            cols = slice(g * GM_DIM, (g + 1) * GM_DIM)
            w = jnp.where(tril, ws_ref[g], 0.0).astype(BF16)
            sm = _dot(w, vgb[:, cols]) + bst_ref[:, g:g + 1]
            mix_ref[rows, ATTN_WIDTH + g * GM_DIM:ATTN_WIDTH + (g + 1) * GM_DIM] = (
                gu[:, cols] * sm).astype(BF16)

    units = []
    for blk in range(blocks):
        units += [attn_unit(blk, pair) for pair in range(N_Q_HEADS // 2)] + [gm_unit(blk)]
    return units


def _in_mix_kernel(tiles_per_seq, blocks, x_ref, g1_ref, w1_ref, w3_ref, w2_ref, gm_ref, wq_ref,
                   wr_ref, bq_ref, br_ref, sinks_ref, lng_ref, lnb_ref, ws_ref, bst_ref,
                   x1_ref, mix_ref, kvw_ref, gmv_ref, h2_s, q_s, kv_s, uvg_s, kvp_s):
    t = pl.program_id(0)
    last_step = pl.num_programs(0) - 1
    tail = slice((blocks - 1) * WINDOW, blocks * WINDOW)

    @pl.when(t == 0)
    def _():
        h2_s[...] = jnp.zeros_like(h2_s)
        kv_s[...] = jnp.zeros_like(kv_s)

    def project_previous():
        kvp_s[...] = kv_s[tail, :]
        h2 = h2_s[...]
        q_s[...] = ((_dot(h2, wq_ref[...]) + bq_ref[...]) * ATTN_SCALE).astype(BF16)
        kv = _dot(h2, wr_ref[:, :2 * KV_WIDTH]) + br_ref[:, :2 * KV_WIDTH]
        kv_s[...] = kv
        kvw_ref[...] = kv[tail, :]
        uvg_s[...] = _dot(h2, wr_ref[:, 2 * KV_WIDTH:]) + br_ref[:, 2 * KV_WIDTH:]
        first_of_seq = (jnp.maximum(t - 1, 0) % tiles_per_seq) == 0
        return _mix_units(blocks, first_of_seq, sinks_ref, q_s, kv_s, kvp_s, uvg_s, lng_ref,
                          lnb_ref, ws_ref, bst_ref, mix_ref, gmv_ref)

    @pl.when(t < last_step)
    def _():
        units = project_previous()
        in_flight = []

        def before_chunk(c):
            for unit in in_flight:
                next(unit, None)
            first, last = c * len(units) // N_FF_CHUNKS, (c + 1) * len(units) // N_FF_CHUNKS
            in_flight[:] = units[first:last]
            for unit in in_flight:
                next(unit)

        x = x_ref[...]
        h = _rms(x, g1_ref[...]).astype(BF16)
        x1 = x + 0.5 * _swiglu(h, w1_ref, w3_ref, w2_ref, before_chunk)
        for unit in in_flight:
            next(unit, None)
        x1_ref[...] = x1
        h2_s[...] = _rms(x1, gm_ref[...]).astype(BF16)

    @pl.when(t == last_step)
    def _():
        units = project_previous()
        for unit in units:
            next(unit)
        for unit in units:
            next(unit, None)


def _in_mix(x, g1, w1, w3, w2, gmix, wq, wr, bq, br, sinks, ln_g, ln_b, w_s, b_st, tm):
    b, s, _ = x.shape
    nj = s // tm
    nt = b * nj
    blocks = tm // WINDOW

    def cur(t):
        u = jnp.minimum(t, nt - 1)
        return u // nj, u % nj

    def prev(t):
        u = jnp.maximum(t - 1, 0)
        return u // nj, u % nj

    def tile(width, which):
        return pl.BlockSpec((None, tm, width), lambda t: which(t) + (0,))

    def per_seq(rows, width, which):
        return pl.BlockSpec((None, rows, width), lambda t: (which(t)[0], 0, 0))

    consts = (g1, w1, w3, w2, gmix, wq, wr, bq, br)
    gconsts = (ln_g, ln_b, w_s, b_st)
    return pl.pallas_call(
        functools.partial(_in_mix_kernel, nj, blocks),
        out_shape=(jax.ShapeDtypeStruct((b, s, D_MODEL), F32),
                   jax.ShapeDtypeStruct((b, s, D_MODEL), BF16),
                   jax.ShapeDtypeStruct((b, WINDOW, 2 * KV_WIDTH), F32),
                   jax.ShapeDtypeStruct((b, CHUNK, GM_WIDTH), F32)),
        grid=(nt + 1,),
        in_specs=[tile(D_MODEL, cur)] + [_resident(c.shape) for c in consts]
        + [pl.BlockSpec(memory_space=pltpu.SMEM)] + [_resident(c.shape) for c in gconsts],
        out_specs=(tile(D_MODEL, cur), tile(D_MODEL, prev),
                   per_seq(WINDOW, 2 * KV_WIDTH, prev), per_seq(CHUNK, GM_WIDTH, prev)),
        scratch_shapes=[pltpu.VMEM((tm, D_MODEL), BF16),
                        pltpu.VMEM((tm, ATTN_WIDTH), BF16), pltpu.VMEM((tm, 2 * KV_WIDTH), F32),
                        pltpu.VMEM((tm, 2 * GM_WIDTH), F32),
                        pltpu.VMEM((WINDOW, 2 * KV_WIDTH), F32)],
        compiler_params=pltpu.CompilerParams(
            dimension_semantics=("arbitrary",), vmem_limit_bytes=VMEM_LIMIT_BYTES),
        name="in_mix",
    )(x, *consts, sinks, *gconsts)


def _mix_sample_kernel(q8_ref, kn_ref, vn_ref, kc_ref, vc_ref, sinks_ref, uvg_ref, lng_ref,
                       lnb_ref, w00_ref, b0_ref, o8_ref, gm_ref, gmv_ref, kw_ref, vw_ref):
    q8 = q8_ref[...]
    kc = kc_ref[...]
    vc = vc_ref[...]
    kn = kn_ref[...]
    vn = vn_ref[...]
    sink = sinks_ref[...]
    s_c = jnp.einsum("bqd,bkd->bqk", q8, kc.astype(BF16), preferred_element_type=F32)
    s_n = jnp.sum(q8.astype(F32) * kn.astype(BF16).astype(F32), axis=-1, keepdims=True)
    m = jnp.maximum(jnp.maximum(jnp.max(s_c, axis=-1, keepdims=True), s_n), sink)
    e_c = jnp.exp(s_c - m)
    e_n = jnp.exp(s_n - m)
    inv = 1.0 / (jnp.sum(e_c, axis=-1, keepdims=True) + e_n + jnp.exp(sink - m))
    p_c = (e_c * inv).astype(BF16)
    p_n = (e_n * inv).astype(BF16).astype(F32)
    o = jnp.einsum("bqk,bkd->bqd", p_c, vc.astype(BF16), preferred_element_type=F32)
    o = o + p_n * vn.astype(BF16).astype(F32)
    row = lax.broadcasted_iota(jnp.int32, o.shape, 1)
    lane = lax.broadcasted_iota(jnp.int32, o.shape, 2)
    own = (lane < HEAD_DIM) == (row < N_Q_HEADS // N_KV_HEADS)
    o8_ref[...] = jnp.where(own, o, 0.0).astype(BF16)

    kw_ref[:, :WINDOW - 1, :] = kc[:, 1:, :]
    kw_ref[:, WINDOW - 1:, :] = kn
    vw_ref[:, :WINDOW - 1, :] = vc[:, 1:, :]
    vw_ref[:, WINDOW - 1:, :] = vn

    vgn = _gm_norm(uvg_ref[:, GM_WIDTH:], lng_ref[...], lnb_ref[...])
    gmv_ref[...] = vgn
    sm = w00_ref[...].astype(BF16).astype(F32) * vgn.astype(BF16).astype(F32) + b0_ref[...]
    gm_ref[...] = (_gelu(uvg_ref[:, :GM_WIDTH]) * sm).astype(BF16)


def _mix_sample(q8, kn, vn, kc, vc, sinks, uvg, ln_g, ln_b, w00, b0, bb):
    n = q8.shape[0]

    def b3(d1, d2):
        return pl.BlockSpec((bb, d1, d2), lambda i: (i, 0, 0))

    def b2(d):
        return pl.BlockSpec((bb, d), lambda i: (i, 0))

    return pl.pallas_call(
        _mix_sample_kernel,
        out_shape=(jax.ShapeDtypeStruct((n, N_Q_HEADS, LANES), BF16),
                   jax.ShapeDtypeStruct((n, GM_WIDTH), BF16),
                   jax.ShapeDtypeStruct((n, GM_WIDTH), F32),
                   jax.ShapeDtypeStruct((n, WINDOW, KV_WIDTH), F32),
                   jax.ShapeDtypeStruct((n, WINDOW, KV_WIDTH), F32)),
        grid=(n // bb,),
        in_specs=[b3(N_Q_HEADS, LANES), b3(1, KV_WIDTH), b3(1, KV_WIDTH),
                  b3(WINDOW, KV_WIDTH), b3(WINDOW, KV_WIDTH),
                  _resident(sinks.shape), b2(2 * GM_WIDTH), _resident(ln_g.shape),
                  _resident(ln_b.shape), _resident(w00.shape), _resident(b0.shape)],
        out_specs=(b3(N_Q_HEADS, LANES), b2(GM_WIDTH), b2(GM_WIDTH),
                   b3(WINDOW, KV_WIDTH), b3(WINDOW, KV_WIDTH)),
        compiler_params=pltpu.CompilerParams(
            dimension_semantics=("arbitrary",), vmem_limit_bytes=VMEM_LIMIT_BYTES),
        name="mix_sample",
    )(q8, kn, vn, kc, vc, sinks, uvg, ln_g, ln_b, w00, b0)


def _out_stage_kernel(x1_ref, mix_ref, p_ref, wo_ref, bo_ref, g2_ref, w1_ref, w3_ref, w2_ref,
                      gp_ref, wg_ref, wp_ref, gf_ref, y_ref):
    x2 = x1_ref[...] + _dot(mix_ref[...], wo_ref[...]) + bo_ref[...]
    h = _rms(x2, g2_ref[...]).astype(BF16)
    x3 = x2 + 0.5 * _swiglu(h, w1_ref, w3_ref, w2_ref)
    gate = _sigmoid(_dot(_rms(x3, gp_ref[...]).astype(BF16), wg_ref[...]))
    x4 = x3 + gate * _dot(p_ref[...].astype(BF16), wp_ref[...])
    y_ref[...] = _rms(x4, gf_ref[...])


def _out_stage(x1, mix, p, wo, bo, g2, w1, w3, w2, gp, wg, wp, gf, tm):
    b, s, _ = x1.shape

    def tile(width):
        return pl.BlockSpec((None, tm, width), lambda i, j: (i, j, 0))

    consts = (wo, bo, g2, w1, w3, w2, gp, wg, wp, gf)
    return pl.pallas_call(
        _out_stage_kernel,
        out_shape=jax.ShapeDtypeStruct((b, s, D_MODEL), F32),
        grid=(b, s // tm),
        in_specs=[tile(D_MODEL), tile(mix.shape[-1]), tile(PLE_DIM)]
        + [_resident(c.shape) for c in consts],
        out_specs=tile(D_MODEL),
        compiler_params=pltpu.CompilerParams(
            dimension_semantics=("arbitrary", "arbitrary"), vmem_limit_bytes=VMEM_LIMIT_BYTES),
        name="out_stage",
    )(x1, mix, p, *consts)


def _out_carry_kernel(x1_ref, mix_ref, p_ref, wo_ref, bo_ref, g2_ref, w1_ref, w3_ref, w2_ref,
                      gp_ref, wg_ref, wp_ref, gf_ref, y_ref, x3_s, h3_s):
    t = pl.program_id(0)
    last_step = pl.num_programs(0) - 1

    @pl.when(t == 0)
    def _():
        x3_s[...] = jnp.zeros_like(x3_s)
        h3_s[...] = jnp.zeros_like(h3_s)

    def finish_previous():
        gate = _sigmoid(_dot(h3_s[...], wg_ref[...]))
        x4 = x3_s[...] + gate * _dot(p_ref[...].astype(BF16), wp_ref[...])
        y_ref[...] = _rms(x4, gf_ref[...])

    @pl.when(t < last_step)
    def _():
        x2 = x1_ref[...] + _dot(mix_ref[...], wo_ref[...]) + bo_ref[...]
        finish_previous()
        h = _rms(x2, g2_ref[...]).astype(BF16)
        x3 = x2 + 0.5 * _swiglu(h, w1_ref, w3_ref, w2_ref)
        x3_s[...] = x3
        h3_s[...] = _rms(x3, gp_ref[...]).astype(BF16)

    @pl.when(t == last_step)
    def _():
        finish_previous()


def _out_carry(x1, mix, p, wo, bo, g2, w1, w3, w2, gp, wg, wp, gf, tm):
    b, s, _ = x1.shape
    nj = s // tm
    nt = b * nj

    def cur(t):
        u = jnp.minimum(t, nt - 1)
        return u // nj, u % nj, 0

    def prev(t):
        u = jnp.maximum(t - 1, 0)
        return u // nj, u % nj, 0

    def tile(width, which):
        return pl.BlockSpec((None, tm, width), which)

    consts = (wo, bo, g2, w1, w3, w2, gp, wg, wp, gf)
    return pl.pallas_call(
        _out_carry_kernel,
        out_shape=jax.ShapeDtypeStruct((b, s, D_MODEL), F32),
        grid=(nt + 1,),
        in_specs=[tile(D_MODEL, cur), tile(D_MODEL, cur), tile(PLE_DIM, prev)]
        + [_resident(c.shape) for c in consts],
        out_specs=tile(D_MODEL, prev),
        scratch_shapes=[pltpu.VMEM((tm, D_MODEL), F32), pltpu.VMEM((tm, D_MODEL), BF16)],
        compiler_params=pltpu.CompilerParams(
            dimension_semantics=("arbitrary",), vmem_limit_bytes=VMEM_LIMIT_BYTES),
        name="out_carry",
    )(x1, mix, p, *consts)


PROMPT_TILE = 512
SAMPLE_BLOCK = 32


def _pad_heads(w):
    lead = w.shape[:-1]
    wh = w.reshape(lead + (N_Q_HEADS, HEAD_DIM))
    z = jnp.zeros_like(wh[..., :N_Q_HEADS // 2, :])
    first = jnp.concatenate([wh[..., :N_Q_HEADS // 2, :], z], axis=-1)
    second = jnp.concatenate([z, wh[..., N_Q_HEADS // 2:, :]], axis=-1)
    return jnp.concatenate([first, second], axis=-2).reshape(lead + (N_Q_HEADS * LANES,))


def kernel(x_prompt, x_sample, cache_k_win, cache_v_win, p_prompt, p_sample, g_ffn1, w1_ffn1, w3_ffn1, w2_ffn1, g_mix, w_in, b_in, attn_sinks, gm_ln_g, gm_ln_b, gm_w_s, gm_b_s, w_out, b_out, g_ffn2, w1_ffn2, w3_ffn2, w2_ffn2, g_ple, w_ple_gate, w_ple_proj, g_final):
    depth = g_ffn1.shape[0]
    assert depth == 1
    i = 0
    nb, seq, _ = x_prompt.shape
    nd = x_sample.shape[0]

    row = lambda a: a.reshape(1, -1)
    g1, gmx, g2, gp, gf = row(g_ffn1[i]), row(g_mix[i]), row(g_ffn2[i]), row(g_ple[i]), row(g_final)
    w1a, w3a, w2a = w1_ffn1[i].astype(BF16), w3_ffn1[i].astype(BF16), w2_ffn1[i].astype(BF16)
    w1b, w3b, w2b = w1_ffn2[i].astype(BF16), w3_ffn2[i].astype(BF16), w2_ffn2[i].astype(BF16)
    wq = w_in[i][:, :ATTN_WIDTH].astype(BF16)
    wr = w_in[i][:, ATTN_WIDTH:].astype(BF16)
    bq, br = row(b_in[i][:ATTN_WIDTH]), row(b_in[i][ATTN_WIDTH:])
    wo, bo = w_out[i].astype(BF16), row(b_out[i])
    wg, wp = w_ple_gate[i].astype(BF16), w_ple_proj[i].astype(BF16)
    ln_g, ln_b = row(gm_ln_g[i]), row(gm_ln_b[i])
    sinks = attn_sinks[i]

    x1, mix, kv_last, gmv_p = _in_mix(x_prompt, g1, w1a, w3a, w2a, gmx, wq, wr, bq, br, sinks,
                                      ln_g, ln_b, gm_w_s[i], gm_b_s[i].T, PROMPT_TILE)
    y_prompt = _out_carry(x1, mix, p_prompt[i], wo, bo, g2, w1b, w3b, w2b, gp, wg, wp, gf,
                          PROMPT_TILE)
    k_win_p = kv_last[:, :, :KV_WIDTH].reshape(1, nb, WINDOW, N_KV_HEADS, HEAD_DIM)
    v_win_p = kv_last[:, :, KV_WIDTH:].reshape(1, nb, WINDOW, N_KV_HEADS, HEAD_DIM)

    wq8, bq8 = _pad_heads(w_in[i][:, :ATTN_WIDTH]).astype(BF16), row(_pad_heads(b_in[i][:ATTN_WIDTH]))
    wo8 = jnp.concatenate([_pad_heads(w_out[i][:ATTN_WIDTH].T).T, w_out[i][ATTN_WIDTH:]],
                          axis=0).astype(BF16)
    xs = x_sample.reshape(1, nd, D_MODEL)
    x1s, q8, kvs, uvgs = _in_stage(xs, g1, w1a, w3a, w2a, gmx, wq8, wr, bq8, br, nd)
    kvs = kvs.reshape(nd, 1, 2 * KV_WIDTH)
    o8, gm_s, gmv_s, kw_s, vw_s = _mix_sample(
        q8.reshape(nd, N_Q_HEADS, LANES), kvs[:, :, :KV_WIDTH], kvs[:, :, KV_WIDTH:],
        cache_k_win[i].reshape(nd, WINDOW, KV_WIDTH), cache_v_win[i].reshape(nd, WINDOW, KV_WIDTH),
        sinks.reshape(N_Q_HEADS, 1), uvgs.reshape(nd, 2 * GM_WIDTH), ln_g, ln_b,
        jnp.repeat(gm_w_s[i][:, 0, 0], GM_DIM).reshape(1, GM_WIDTH),
        jnp.repeat(gm_b_s[i][:, 0], GM_DIM).reshape(1, GM_WIDTH), SAMPLE_BLOCK)
    mix_s = jnp.concatenate([o8.reshape(nd, N_Q_HEADS * LANES), gm_s], axis=-1)
    y_sample = _out_stage(x1s, mix_s.reshape(1, nd, -1), p_sample[i].reshape(1, nd, PLE_DIM),
                          wo8, bo, g2, w1b, w3b, w2b, gp, wg, wp, gf, nd)

    return (y_prompt, y_sample.reshape(nd, 1, D_MODEL), k_win_p, v_win_p,
            kw_s.reshape(1, nd, WINDOW, N_KV_HEADS, HEAD_DIM),
            vw_s.reshape(1, nd, WINDOW, N_KV_HEADS, HEAD_DIM),
            gmv_p.reshape(1, nb, CHUNK, GM_WIDTH), gmv_s.reshape(1, nd, 1, GM_WIDTH))
```

```python
import functools
import math

import jax
import jax.numpy as jnp
from jax import lax
from jax.experimental import pallas as pl
from jax.experimental.pallas import tpu as pltpu

F32 = jnp.float32
BF16 = jnp.bfloat16

D_MODEL = 1024
HEAD_DIM = 64
N_Q_HEADS = 8
N_KV_HEADS = 2
ATTN_WIDTH = N_Q_HEADS * HEAD_DIM
KV_WIDTH = N_KV_HEADS * HEAD_DIM
WINDOW = 128
GM_WIDTH = 512
GM_GROUPS = 4
GM_DIM = 128
CHUNK = 128
D_FF = 2816
PLE_DIM = 256
RMS_EPS = 1e-6
LN_EPS = 1e-5
ATTN_SCALE = HEAD_DIM ** -0.5

LANES = 128
FF_CHUNK = 512
FF_BOUNDS = tuple((lo, min(lo + FF_CHUNK, D_FF)) for lo in range(0, D_FF, FF_CHUNK))
N_FF_CHUNKS = len(FF_BOUNDS)
VMEM_LIMIT_BYTES = 60 * 1024 * 1024

_GELU_C = math.sqrt(2.0 / math.pi)


def _rms(x, g):
    return x * lax.rsqrt(jnp.mean(x * x, axis=-1, keepdims=True) + RMS_EPS) * g


def _sigmoid(x):
    return 1.0 / (1.0 + jnp.exp(-x))


def _gelu(x):
    return 0.5 * x * (1.0 + jnp.tanh(_GELU_C * (x + 0.044715 * (x * x * x))))


def _dot(a, b):
    return jnp.dot(a, b, preferred_element_type=F32)


def _dot_nt(a, b):
    return lax.dot_general(a, b, (((1,), (1,)), ((), ())), preferred_element_type=F32)


def _swiglu(h, w1_ref, w3_ref, w2_ref, before_chunk=None):
    acc = jnp.zeros((h.shape[0], D_MODEL), F32)
    for c, (lo, hi) in enumerate(FF_BOUNDS):
        if before_chunk is not None:
            before_chunk(c)
        cols = slice(lo, hi)
        a = _dot(h, w1_ref[:, cols])
        b = _dot(h, w3_ref[:, cols])
        act = (a * _sigmoid(a) * b).astype(BF16)
        acc = acc + _dot(act, w2_ref[cols, :])
    return acc


def _resident(shape):
    zeros = (0,) * len(shape)
    return pl.BlockSpec(shape, lambda *_: zeros, pipeline_mode=pl.Buffered(1))


def _in_stage_kernel(x_ref, g1_ref, w1_ref, w3_ref, w2_ref, gm_ref, wq_ref, wr_ref,
                     bq_ref, br_ref, x1_ref, q_ref, kv_ref, uvg_ref):
    x = x_ref[...]
    h = _rms(x, g1_ref[...]).astype(BF16)
    x1 = x + 0.5 * _swiglu(h, w1_ref, w3_ref, w2_ref)
    x1_ref[...] = x1
    h2 = _rms(x1, gm_ref[...]).astype(BF16)
    q_ref[...] = ((_dot(h2, wq_ref[...]) + bq_ref[...]) * ATTN_SCALE).astype(BF16)
    kv_ref[...] = _dot(h2, wr_ref[:, :2 * KV_WIDTH]) + br_ref[:, :2 * KV_WIDTH]
    uvg_ref[...] = _dot(h2, wr_ref[:, 2 * KV_WIDTH:]) + br_ref[:, 2 * KV_WIDTH:]


def _in_stage(x, g1, w1, w3, w2, gmix, wq, wr, bq, br, tm):
    b, s, _ = x.shape
    qw = wq.shape[1]
    rw = 2 * GM_WIDTH

    def tile(width):
        return pl.BlockSpec((None, tm, width), lambda i, j: (i, j, 0))

    consts = (g1, w1, w3, w2, gmix, wq, wr, bq, br)
    return pl.pallas_call(
        _in_stage_kernel,
        out_shape=(jax.ShapeDtypeStruct((b, s, D_MODEL), F32),
                   jax.ShapeDtypeStruct((b, s, qw), BF16),
                   jax.ShapeDtypeStruct((b, s, 2 * KV_WIDTH), F32),
                   jax.ShapeDtypeStruct((b, s, rw), F32)),
        grid=(b, s // tm),
        in_specs=[tile(D_MODEL)] + [_resident(c.shape) for c in consts],
        out_specs=(tile(D_MODEL), tile(qw), tile(2 * KV_WIDTH), tile(rw)),
        compiler_params=pltpu.CompilerParams(
            dimension_semantics=("arbitrary", "arbitrary"), vmem_limit_bytes=VMEM_LIMIT_BYTES),
        name="in_stage",
    )(x, *consts)


def _gm_norm(vg, ln_g, ln_b):
    a = _gelu(vg)
    mu = jnp.mean(a, axis=-1, keepdims=True)
    ac = a - mu
    return ac * lax.rsqrt(jnp.mean(ac * ac, axis=-1, keepdims=True) + LN_EPS) * ln_g + ln_b


def _mix_units(blocks, first_of_seq, sinks_ref, q_ref, kv_ref, kvp_ref, uvg_ref, lng_ref,
               lnb_ref, ws_ref, bst_ref, mix_ref, gmv_ref):
    lane = lax.broadcasted_iota(jnp.int32, (2 * WINDOW, LANES), 1)
    lo = lane < HEAD_DIM
    qi = lax.broadcasted_iota(jnp.int32, (WINDOW, 2 * WINDOW), 0)
    kj = lax.broadcasted_iota(jnp.int32, (WINDOW, 2 * WINDOW), 1)
    band = (kj >= qi) & (kj <= qi + WINDOW)
    ti = lax.broadcasted_iota(jnp.int32, (CHUNK, CHUNK), 0)
    si = lax.broadcasted_iota(jnp.int32, (CHUNK, CHUNK), 1)
    tril = si <= ti

    block_cache = {}

    def block_operands(blk):
        if blk not in block_cache:
            if blk == 0:
                kv2 = jnp.concatenate([kvp_ref[...], kv_ref[:WINDOW, :]], axis=0)
                mask = band & (kj >= jnp.where(first_of_seq, WINDOW, 0))
            else:
                kv2 = kv_ref[(blk - 1) * WINDOW:(blk + 1) * WINDOW, :]
                mask = band
            k2 = kv2[:, :KV_WIDTH].astype(BF16)
            v2 = kv2[:, KV_WIDTH:].astype(BF16)
            k2r = pltpu.roll(k2, HEAD_DIM, 1)
            v2r = pltpu.roll(v2, HEAD_DIM, 1)
            zero = jnp.zeros_like(k2)
            k_lo = (jnp.where(lo, k2, zero), jnp.where(lo, k2r, zero))
            k_hi = (jnp.where(lo, zero, k2r), jnp.where(lo, zero, k2))
            v_lo = (jnp.where(lo, v2, zero), jnp.where(lo, v2r, zero))
            v_hi = (jnp.where(lo, zero, v2r), jnp.where(lo, zero, v2))
            block_cache[blk] = (mask, k_lo, k_hi, v_lo, v_hi)
        return block_cache[blk]

    def attn_unit(blk, pair):
        rows = slice(blk * WINDOW, (blk + 1) * WINDOW)
        mask, k_lo, k_hi, v_lo, v_hi = block_operands(blk)
        h = pair // 2
        qp = q_ref[rows, pair * LANES:(pair + 1) * LANES]
        scores = [_dot_nt(qp, kk) for kk in (k_lo[h], k_hi[h])]
        yield
        out = None
        for half, vv in enumerate((v_lo[h], v_hi[h])):
            sink = sinks_ref[2 * pair + half]
            s = jnp.where(mask, scores[half], -jnp.inf)
            m = jnp.maximum(jnp.max(s, axis=-1, keepdims=True), sink)
            e = jnp.exp(s - m)
            denom = jnp.sum(e, axis=-1, keepdims=True) + jnp.exp(sink - m)
            p = (e * (1.0 / denom)).astype(BF16)
            o = _dot(p, vv)
            out = o if out is None else out + o
        mix_ref[rows, pair * LANES:(pair + 1) * LANES] = out.astype(BF16)

    def gm_unit(blk):
        rows = slice(blk * WINDOW, (blk + 1) * WINDOW)
        vgn = _gm_norm(uvg_ref[rows, GM_WIDTH:], lng_ref[...], lnb_ref[...])
        if blk == blocks - 1:
            gmv_ref[...] = vgn
        vgb = vgn.astype(BF16)
        gu = _gelu(uvg_ref[rows, :GM_WIDTH])
        yield
        for g in range(GM_GROUPS):
            cols = slice(g * GM_DIM, (g + 1) * GM_DIM)
            w = jnp.where(tril, ws_ref[g], 0.0).astype(BF16)
            sm = _dot(w, vgb[:, cols]) + bst_ref[:, g:g + 1]
            mix_ref[rows, ATTN_WIDTH + g * GM_DIM:ATTN_WIDTH + (g + 1) * GM_DIM] = (
                gu[:, cols] * sm).astype(BF16)

    units = []
    for blk in range(blocks):
        units += [attn_unit(blk, pair) for pair in range(N_Q_HEADS // 2)] + [gm_unit(blk)]
    return units


def _in_mix_kernel(tiles_per_seq, blocks, x_ref, g1_ref, w1_ref, w3_ref, w2_ref, gm_ref, wq_ref,
                   wr_ref, bq_ref, br_ref, sinks_ref, lng_ref, lnb_ref, ws_ref, bst_ref,
                   x1_ref, mix_ref, kvw_ref, gmv_ref, q_s, kv_s, uvg_s, kvp_s):
    t = pl.program_id(0)

    @pl.when(t == 0)
    def _():
        q_s[...] = jnp.zeros_like(q_s)
        kv_s[...] = jnp.zeros_like(kv_s)
        uvg_s[...] = jnp.zeros_like(uvg_s)
        kvp_s[...] = jnp.zeros_like(kvp_s)

    first_of_seq = (jnp.maximum(t - 1, 0) % tiles_per_seq) == 0
    units = _mix_units(blocks, first_of_seq, sinks_ref, q_s, kv_s, kvp_s, uvg_s, lng_ref, lnb_ref,
                       ws_ref, bst_ref, mix_ref, gmv_ref)
    in_flight = []

    def before_chunk(c):
        for unit in in_flight:
            next(unit, None)
        first, last = c * len(units) // N_FF_CHUNKS, (c + 1) * len(units) // N_FF_CHUNKS
        in_flight[:] = units[first:last]
        for unit in in_flight:
            next(unit)

    x = x_ref[...]
    h = _rms(x, g1_ref[...]).astype(BF16)
    x1 = x + 0.5 * _swiglu(h, w1_ref, w3_ref, w2_ref, before_chunk)
    for unit in in_flight:
        next(unit, None)
    kvp_s[...] = kv_s[(blocks - 1) * WINDOW:, :]
    x1_ref[...] = x1
    h2 = _rms(x1, gm_ref[...]).astype(BF16)
    q_s[...] = ((_dot(h2, wq_ref[...]) + bq_ref[...]) * ATTN_SCALE).astype(BF16)
    kv = _dot(h2, wr_ref[:, :2 * KV_WIDTH]) + br_ref[:, :2 * KV_WIDTH]
    kv_s[...] = kv
    kvw_ref[...] = kv[(blocks - 1) * WINDOW:, :]
    uvg_s[...] = _dot(h2, wr_ref[:, 2 * KV_WIDTH:]) + br_ref[:, 2 * KV_WIDTH:]


def _in_mix(x, g1, w1, w3, w2, gmix, wq, wr, bq, br, sinks, ln_g, ln_b, w_s, b_st, tm):
    b, s, _ = x.shape
    nj = s // tm
    nt = b * nj
    blocks = tm // WINDOW

    def cur(t):
        u = jnp.minimum(t, nt - 1)
        return u // nj, u % nj

    def prev(t):
        u = jnp.maximum(t - 1, 0)
        return u // nj, u % nj

    def tile(width, which):
        return pl.BlockSpec((None, tm, width), lambda t: which(t) + (0,))

    def per_seq(rows, width, which):
        return pl.BlockSpec((None, rows, width), lambda t: (which(t)[0], 0, 0))

    consts = (g1, w1, w3, w2, gmix, wq, wr, bq, br)
    gconsts = (ln_g, ln_b, w_s, b_st)
    return pl.pallas_call(
        functools.partial(_in_mix_kernel, nj, blocks),
        out_shape=(jax.ShapeDtypeStruct((b, s, D_MODEL), F32),
                   jax.ShapeDtypeStruct((b, s, D_MODEL), BF16),
                   jax.ShapeDtypeStruct((b, WINDOW, 2 * KV_WIDTH), F32),
                   jax.ShapeDtypeStruct((b, CHUNK, GM_WIDTH), F32)),
        grid=(nt + 1,),
        in_specs=[tile(D_MODEL, cur)] + [_resident(c.shape) for c in consts]
        + [pl.BlockSpec(memory_space=pltpu.SMEM)] + [_resident(c.shape) for c in gconsts],
        out_specs=(tile(D_MODEL, cur), tile(D_MODEL, prev),
                   per_seq(WINDOW, 2 * KV_WIDTH, cur), per_seq(CHUNK, GM_WIDTH, prev)),
        scratch_shapes=[pltpu.VMEM((tm, ATTN_WIDTH), BF16), pltpu.VMEM((tm, 2 * KV_WIDTH), F32),
                        pltpu.VMEM((tm, 2 * GM_WIDTH), F32),
                        pltpu.VMEM((WINDOW, 2 * KV_WIDTH), F32)],
        compiler_params=pltpu.CompilerParams(
            dimension_semantics=("arbitrary",), vmem_limit_bytes=VMEM_LIMIT_BYTES),
        name="in_mix",
    )(x, *consts, sinks, *gconsts)


def _mix_sample_kernel(q8_ref, kn_ref, vn_ref, kc_ref, vc_ref, sinks_ref, uvg_ref, lng_ref,
                       lnb_ref, w00_ref, b0_ref, o8_ref, gm_ref, gmv_ref, kw_ref, vw_ref):
    q8 = q8_ref[...]
    kc = kc_ref[...]
    vc = vc_ref[...]
    kn = kn_ref[...]
    vn = vn_ref[...]
    sink = sinks_ref[...]
    s_c = jnp.einsum("bqd,bkd->bqk", q8, kc.astype(BF16), preferred_element_type=F32)
    s_n = jnp.sum(q8.astype(F32) * kn.astype(BF16).astype(F32), axis=-1, keepdims=True)
    m = jnp.maximum(jnp.maximum(jnp.max(s_c, axis=-1, keepdims=True), s_n), sink)
    e_c = jnp.exp(s_c - m)
    e_n = jnp.exp(s_n - m)
    inv = 1.0 / (jnp.sum(e_c, axis=-1, keepdims=True) + e_n + jnp.exp(sink - m))
    p_c = (e_c * inv).astype(BF16)
    p_n = (e_n * inv).astype(BF16).astype(F32)
    o = jnp.einsum("bqk,bkd->bqd", p_c, vc.astype(BF16), preferred_element_type=F32)
    o = o + p_n * vn.astype(BF16).astype(F32)
    row = lax.broadcasted_iota(jnp.int32, o.shape, 1)
    lane = lax.broadcasted_iota(jnp.int32, o.shape, 2)
    own = (lane < HEAD_DIM) == (row < N_Q_HEADS // N_KV_HEADS)
    o8_ref[...] = jnp.where(own, o, 0.0).astype(BF16)

    kw_ref[:, :WINDOW - 1, :] = kc[:, 1:, :]
    kw_ref[:, WINDOW - 1:, :] = kn
    vw_ref[:, :WINDOW - 1, :] = vc[:, 1:, :]
    vw_ref[:, WINDOW - 1:, :] = vn

    vgn = _gm_norm(uvg_ref[:, GM_WIDTH:], lng_ref[...], lnb_ref[...])
    gmv_ref[...] = vgn
    sm = w00_ref[...].astype(BF16).astype(F32) * vgn.astype(BF16).astype(F32) + b0_ref[...]
    gm_ref[...] = (_gelu(uvg_ref[:, :GM_WIDTH]) * sm).astype(BF16)


def _mix_sample(q8, kn, vn, kc, vc, sinks, uvg, ln_g, ln_b, w00, b0, bb):
    n = q8.shape[0]

    def b3(d1, d2):
        return pl.BlockSpec((bb, d1, d2), lambda i: (i, 0, 0))

    def b2(d):
        return pl.BlockSpec((bb, d), lambda i: (i, 0))

    return pl.pallas_call(
        _mix_sample_kernel,
        out_shape=(jax.ShapeDtypeStruct((n, N_Q_HEADS, LANES), BF16),
                   jax.ShapeDtypeStruct((n, GM_WIDTH), BF16),
                   jax.ShapeDtypeStruct((n, GM_WIDTH), F32),
                   jax.ShapeDtypeStruct((n, WINDOW, KV_WIDTH), F32),
                   jax.ShapeDtypeStruct((n, WINDOW, KV_WIDTH), F32)),
        grid=(n // bb,),
        in_specs=[b3(N_Q_HEADS, LANES), b3(1, KV_WIDTH), b3(1, KV_WIDTH),
                  b3(WINDOW, KV_WIDTH), b3(WINDOW, KV_WIDTH),
                  _resident(sinks.shape), b2(2 * GM_WIDTH), _resident(ln_g.shape),
                  _resident(ln_b.shape), _resident(w00.shape), _resident(b0.shape)],
        out_specs=(b3(N_Q_HEADS, LANES), b2(GM_WIDTH), b2(GM_WIDTH),
                   b3(WINDOW, KV_WIDTH), b3(WINDOW, KV_WIDTH)),
        compiler_params=pltpu.CompilerParams(
            dimension_semantics=("arbitrary",), vmem_limit_bytes=VMEM_LIMIT_BYTES),
        name="mix_sample",
    )(q8, kn, vn, kc, vc, sinks, uvg, ln_g, ln_b, w00, b0)


def _out_stage_kernel(x1_ref, mix_ref, p_ref, wo_ref, bo_ref, g2_ref, w1_ref, w3_ref, w2_ref,
                      gp_ref, wg_ref, wp_ref, gf_ref, y_ref):
    x2 = x1_ref[...] + _dot(mix_ref[...], wo_ref[...]) + bo_ref[...]
    h = _rms(x2, g2_ref[...]).astype(BF16)
    x3 = x2 + 0.5 * _swiglu(h, w1_ref, w3_ref, w2_ref)
    gate = _sigmoid(_dot(_rms(x3, gp_ref[...]).astype(BF16), wg_ref[...]))
    x4 = x3 + gate * _dot(p_ref[...].astype(BF16), wp_ref[...])
    y_ref[...] = _rms(x4, gf_ref[...])


def _out_stage(x1, mix, p, wo, bo, g2, w1, w3, w2, gp, wg, wp, gf, tm):
    b, s, _ = x1.shape

    def tile(width):
        return pl.BlockSpec((None, tm, width), lambda i, j: (i, j, 0))

    consts = (wo, bo, g2, w1, w3, w2, gp, wg, wp, gf)
    return pl.pallas_call(
        _out_stage_kernel,
        out_shape=jax.ShapeDtypeStruct((b, s, D_MODEL), F32),
        grid=(b, s // tm),
        in_specs=[tile(D_MODEL), tile(mix.shape[-1]), tile(PLE_DIM)]
        + [_resident(c.shape) for c in consts],
        out_specs=tile(D_MODEL),
        compiler_params=pltpu.CompilerParams(
            dimension_semantics=("arbitrary", "arbitrary"), vmem_limit_bytes=VMEM_LIMIT_BYTES),
        name="out_stage",
    )(x1, mix, p, *consts)


PROMPT_TILE = 512
SAMPLE_BLOCK = 32


def _pad_heads(w):
    lead = w.shape[:-1]
    wh = w.reshape(lead + (N_Q_HEADS, HEAD_DIM))
    z = jnp.zeros_like(wh[..., :N_Q_HEADS // 2, :])
    first = jnp.concatenate([wh[..., :N_Q_HEADS // 2, :], z], axis=-1)
    second = jnp.concatenate([z, wh[..., N_Q_HEADS // 2:, :]], axis=-1)
    return jnp.concatenate([first, second], axis=-2).reshape(lead + (N_Q_HEADS * LANES,))


def kernel(x_prompt, x_sample, cache_k_win, cache_v_win, p_prompt, p_sample, g_ffn1, w1_ffn1, w3_ffn1, w2_ffn1, g_mix, w_in, b_in, attn_sinks, gm_ln_g, gm_ln_b, gm_w_s, gm_b_s, w_out, b_out, g_ffn2, w1_ffn2, w3_ffn2, w2_ffn2, g_ple, w_ple_gate, w_ple_proj, g_final):
    depth = g_ffn1.shape[0]
    assert depth == 1
    i = 0
    nb, seq, _ = x_prompt.shape
    nd = x_sample.shape[0]

    row = lambda a: a.reshape(1, -1)
    g1, gmx, g2, gp, gf = row(g_ffn1[i]), row(g_mix[i]), row(g_ffn2[i]), row(g_ple[i]), row(g_final)
    w1a, w3a, w2a = w1_ffn1[i].astype(BF16), w3_ffn1[i].astype(BF16), w2_ffn1[i].astype(BF16)
    w1b, w3b, w2b = w1_ffn2[i].astype(BF16), w3_ffn2[i].astype(BF16), w2_ffn2[i].astype(BF16)
    wq = w_in[i][:, :ATTN_WIDTH].astype(BF16)
    wr = w_in[i][:, ATTN_WIDTH:].astype(BF16)
    bq, br = row(b_in[i][:ATTN_WIDTH]), row(b_in[i][ATTN_WIDTH:])
    wo, bo = w_out[i].astype(BF16), row(b_out[i])
    wg, wp = w_ple_gate[i].astype(BF16), w_ple_proj[i].astype(BF16)
    ln_g, ln_b = row(gm_ln_g[i]), row(gm_ln_b[i])
    sinks = attn_sinks[i]

    x1, mix, kv_last, gmv_p = _in_mix(x_prompt, g1, w1a, w3a, w2a, gmx, wq, wr, bq, br, sinks,
                                      ln_g, ln_b, gm_w_s[i], gm_b_s[i].T, PROMPT_TILE)
    y_prompt = _out_stage(x1, mix, p_prompt[i], wo, bo, g2, w1b, w3b, w2b, gp, wg, wp, gf,
                          2 * PROMPT_TILE)
    k_win_p = kv_last[:, :, :KV_WIDTH].reshape(1, nb, WINDOW, N_KV_HEADS, HEAD_DIM)
    v_win_p = kv_last[:, :, KV_WIDTH:].reshape(1, nb, WINDOW, N_KV_HEADS, HEAD_DIM)

    wq8, bq8 = _pad_heads(w_in[i][:, :ATTN_WIDTH]).astype(BF16), row(_pad_heads(b_in[i][:ATTN_WIDTH]))
    wo8 = jnp.concatenate([_pad_heads(w_out[i][:ATTN_WIDTH].T).T, w_out[i][ATTN_WIDTH:]],
                          axis=0).astype(BF16)
    xs = x_sample.reshape(1, nd, D_MODEL)
    x1s, q8, kvs, uvgs = _in_stage(xs, g1, w1a, w3a, w2a, gmx, wq8, wr, bq8, br, nd)
    kvs = kvs.reshape(nd, 1, 2 * KV_WIDTH)
    o8, gm_s, gmv_s, kw_s, vw_s = _mix_sample(
        q8.reshape(nd, N_Q_HEADS, LANES), kvs[:, :, :KV_WIDTH], kvs[:, :, KV_WIDTH:],
        cache_k_win[i].reshape(nd, WINDOW, KV_WIDTH), cache_v_win[i].reshape(nd, WINDOW, KV_WIDTH),
        sinks.reshape(N_Q_HEADS, 1), uvgs.reshape(nd, 2 * GM_WIDTH), ln_g, ln_b,
        jnp.repeat(gm_w_s[i][:, 0, 0], GM_DIM).reshape(1, GM_WIDTH),
        jnp.repeat(gm_b_s[i][:, 0], GM_DIM).reshape(1, GM_WIDTH), SAMPLE_BLOCK)
    mix_s = jnp.concatenate([o8.reshape(nd, N_Q_HEADS * LANES), gm_s], axis=-1)
    y_sample = _out_stage(x1s, mix_s.reshape(1, nd, -1), p_sample[i].reshape(1, nd, PLE_DIM),
                          wo8, bo, g2, w1b, w3b, w2b, gp, wg, wp, gf, nd)

    return (y_prompt, y_sample.reshape(nd, 1, D_MODEL), k_win_p, v_win_p,
            kw_s.reshape(1, nd, WINDOW, N_KV_HEADS, HEAD_DIM),
            vw_s.reshape(1, nd, WINDOW, N_KV_HEADS, HEAD_DIM),
            gmv_p.reshape(1, nb, CHUNK, GM_WIDTH), gmv_s.reshape(1, nd, 1, GM_WIDTH))
```

```python
import functools
import math

import jax
import jax.numpy as jnp
from jax import lax
from jax.experimental import pallas as pl
from jax.experimental.pallas import tpu as pltpu

F32 = jnp.float32
BF16 = jnp.bfloat16

D_MODEL = 1024
HEAD_DIM = 64
N_Q_HEADS = 8
N_KV_HEADS = 2
ATTN_WIDTH = N_Q_HEADS * HEAD_DIM
KV_WIDTH = N_KV_HEADS * HEAD_DIM
WINDOW = 128
GM_WIDTH = 512
GM_GROUPS = 4
GM_DIM = 128
CHUNK = 128
D_FF = 2816
PLE_DIM = 256
RMS_EPS = 1e-6
LN_EPS = 1e-5
ATTN_SCALE = HEAD_DIM ** -0.5

LANES = 128
FF_CHUNK = 512
FF_BOUNDS = tuple((lo, min(lo + FF_CHUNK, D_FF)) for lo in range(0, D_FF, FF_CHUNK))
N_FF_CHUNKS = len(FF_BOUNDS)
VMEM_LIMIT_BYTES = 60 * 1024 * 1024

_GELU_C = math.sqrt(2.0 / math.pi)


def _rms(x, g):
    return x * lax.rsqrt(jnp.mean(x * x, axis=-1, keepdims=True) + RMS_EPS) * g


def _sigmoid(x):
    return 1.0 / (1.0 + jnp.exp(-x))


def _gelu(x):
    return 0.5 * x * (1.0 + jnp.tanh(_GELU_C * (x + 0.044715 * (x * x * x))))


def _dot(a, b):
    return jnp.dot(a, b, preferred_element_type=F32)


def _dot_nt(a, b):
    return lax.dot_general(a, b, (((1,), (1,)), ((), ())), preferred_element_type=F32)


def _swiglu(h, w1_ref, w3_ref, w2_ref, before_chunk=None):
    acc = jnp.zeros((h.shape[0], D_MODEL), F32)
    for c, (lo, hi) in enumerate(FF_BOUNDS):
        if before_chunk is not None:
            before_chunk(c)
        cols = slice(lo, hi)
        a = _dot(h, w1_ref[:, cols])
        b = _dot(h, w3_ref[:, cols])
        act = (a * _sigmoid(a) * b).astype(BF16)
        acc = acc + _dot(act, w2_ref[cols, :])
    return acc


def _resident(shape):
    zeros = (0,) * len(shape)
    return pl.BlockSpec(shape, lambda *_: zeros, pipeline_mode=pl.Buffered(1))


def _in_stage_kernel(x_ref, g1_ref, w1_ref, w3_ref, w2_ref, gm_ref, wq_ref, wr_ref,
                     bq_ref, br_ref, x1_ref, q_ref, kv_ref, uvg_ref):
    x = x_ref[...]
    h = _rms(x, g1_ref[...]).astype(BF16)
    x1 = x + 0.5 * _swiglu(h, w1_ref, w3_ref, w2_ref)
    x1_ref[...] = x1
    h2 = _rms(x1, gm_ref[...]).astype(BF16)
    q_ref[...] = ((_dot(h2, wq_ref[...]) + bq_ref[...]) * ATTN_SCALE).astype(BF16)
    kv_ref[...] = _dot(h2, wr_ref[:, :2 * KV_WIDTH]) + br_ref[:, :2 * KV_WIDTH]
    uvg_ref[...] = _dot(h2, wr_ref[:, 2 * KV_WIDTH:]) + br_ref[:, 2 * KV_WIDTH:]


def _in_stage(x, g1, w1, w3, w2, gmix, wq, wr, bq, br, tm):
    b, s, _ = x.shape
    qw = wq.shape[1]
    rw = 2 * GM_WIDTH

    def tile(width):
        return pl.BlockSpec((None, tm, width), lambda i, j: (i, j, 0))

    consts = (g1, w1, w3, w2, gmix, wq, wr, bq, br)
    return pl.pallas_call(
        _in_stage_kernel,
        out_shape=(jax.ShapeDtypeStruct((b, s, D_MODEL), F32),
                   jax.ShapeDtypeStruct((b, s, qw), BF16),
                   jax.ShapeDtypeStruct((b, s, 2 * KV_WIDTH), F32),
                   jax.ShapeDtypeStruct((b, s, rw), F32)),
        grid=(b, s // tm),
        in_specs=[tile(D_MODEL)] + [_resident(c.shape) for c in consts],
        out_specs=(tile(D_MODEL), tile(qw), tile(2 * KV_WIDTH), tile(rw)),
        compiler_params=pltpu.CompilerParams(
            dimension_semantics=("arbitrary", "arbitrary"), vmem_limit_bytes=VMEM_LIMIT_BYTES),
        name="in_stage",
    )(x, *consts)


def _gm_norm(vg, ln_g, ln_b):
    a = _gelu(vg)
    mu = jnp.mean(a, axis=-1, keepdims=True)
    ac = a - mu
    return ac * lax.rsqrt(jnp.mean(ac * ac, axis=-1, keepdims=True) + LN_EPS) * ln_g + ln_b


def _mix_units(blocks, first_of_seq, sinks_ref, q_ref, kv_ref, kvp_ref, uvg_ref, lng_ref,
               lnb_ref, ws_ref, bst_ref, mix_ref, gmv_ref):
    lane = lax.broadcasted_iota(jnp.int32, (2 * WINDOW, LANES), 1)
    lo = lane < HEAD_DIM
    qi = lax.broadcasted_iota(jnp.int32, (WINDOW, 2 * WINDOW), 0)
    kj = lax.broadcasted_iota(jnp.int32, (WINDOW, 2 * WINDOW), 1)
    band = (kj >= qi) & (kj <= qi + WINDOW)
    ti = lax.broadcasted_iota(jnp.int32, (CHUNK, CHUNK), 0)
    si = lax.broadcasted_iota(jnp.int32, (CHUNK, CHUNK), 1)
    tril = si <= ti

    block_cache = {}

    def block_operands(blk):
        if blk not in block_cache:
            if blk == 0:
                kv2 = jnp.concatenate([kvp_ref[...], kv_ref[:WINDOW, :]], axis=0)
                mask = band & (kj >= jnp.where(first_of_seq, WINDOW, 0))
            else:
                kv2 = kv_ref[(blk - 1) * WINDOW:(blk + 1) * WINDOW, :]
                mask = band
            kt = kv2[:, :KV_WIDTH].T.astype(BF16)
            v2 = kv2[:, KV_WIDTH:].astype(BF16)
            v2r = pltpu.roll(v2, HEAD_DIM, 1)
            zero_v = jnp.zeros_like(v2)
            zero_k = jnp.zeros((HEAD_DIM, 2 * WINDOW), BF16)
            heads = [kt[h * HEAD_DIM:(h + 1) * HEAD_DIM, :] for h in range(N_KV_HEADS)]
            k_lo = [jnp.concatenate([kh, zero_k], axis=0) for kh in heads]
            k_hi = [jnp.concatenate([zero_k, kh], axis=0) for kh in heads]
            v_lo = (jnp.where(lo, v2, zero_v), jnp.where(lo, v2r, zero_v))
            v_hi = (jnp.where(lo, zero_v, v2r), jnp.where(lo, zero_v, v2))
            mask2 = jnp.concatenate([mask, mask], axis=0)
            block_cache[blk] = (mask2, k_lo, k_hi, v_lo, v_hi)
        return block_cache[blk]

    upper = lax.broadcasted_iota(jnp.int32, (2 * WINDOW, 1), 0) < WINDOW

    def attn_unit(blk, h):
        rows = slice(blk * WINDOW, (blk + 1) * WINDOW)
        mask2, k_lo, k_hi, v_lo, v_hi = block_operands(blk)
        slots = (2 * h, 2 * h + 1)
        q2 = jnp.concatenate([q_ref[rows, s * LANES:(s + 1) * LANES] for s in slots], axis=0)
        scores = [_dot(q2, kk) for kk in (k_lo[h], k_hi[h])]
        yield
        out = None
        for half, vv in enumerate((v_lo[h], v_hi[h])):
            sink = jnp.where(upper, sinks_ref[2 * slots[0] + half], sinks_ref[2 * slots[1] + half])
            s = jnp.where(mask2, scores[half], -jnp.inf)
            m = jnp.maximum(jnp.max(s, axis=-1, keepdims=True), sink)
            e = jnp.exp(s - m)
            denom = jnp.sum(e, axis=-1, keepdims=True) + jnp.exp(sink - m)
            p = (e * (1.0 / denom)).astype(BF16)
            o = _dot(p, vv)
            out = o if out is None else out + o
        out = out.astype(BF16)
        mix_ref[rows, slots[0] * LANES:(slots[0] + 1) * LANES] = out[:WINDOW]
        mix_ref[rows, slots[1] * LANES:(slots[1] + 1) * LANES] = out[WINDOW:]

    def gm_unit(blk0):
        pair = (blk0, blk0 + 1)
        rows = [slice(blk * WINDOW, (blk + 1) * WINDOW) for blk in pair]
        vgn = [_gm_norm(uvg_ref[r, GM_WIDTH:], lng_ref[...], lnb_ref[...]) for r in rows]
        if pair[1] == blocks - 1:
            gmv_ref[...] = vgn[1]
        vgb = [v.astype(BF16) for v in vgn]
        gu = [_gelu(uvg_ref[r, :GM_WIDTH]) for r in rows]
        yield
        for g in range(GM_GROUPS):
            cols = slice(g * GM_DIM, (g + 1) * GM_DIM)
            w = jnp.where(tril, ws_ref[g], 0.0).astype(BF16)
            sm = _dot(w, jnp.concatenate([v[:, cols] for v in vgb], axis=1)) + bst_ref[:, g:g + 1]
            for i, r in enumerate(rows):
                mix_ref[r, ATTN_WIDTH + g * GM_DIM:ATTN_WIDTH + (g + 1) * GM_DIM] = (
                    gu[i][:, cols] * sm[:, i * GM_DIM:(i + 1) * GM_DIM]).astype(BF16)

    assert blocks % 2 == 0
    units = []
    for blk0 in range(0, blocks, 2):
        units += [attn_unit(blk, h) for blk in (blk0, blk0 + 1) for h in range(N_KV_HEADS)]
        units.append(gm_unit(blk0))
    return units


def _in_mix_kernel(tiles_per_seq, blocks, x_ref, g1_ref, w1_ref, w3_ref, w2_ref, gm_ref, wq_ref,
                   wr_ref, bq_ref, br_ref, sinks_ref, lng_ref, lnb_ref, ws_ref, bst_ref,
                   x1_ref, mix_ref, kvw_ref, gmv_ref, q_s, kv_s, uvg_s, kvp_s):
    t = pl.program_id(0)

    @pl.when(t == 0)
    def _():
        q_s[...] = jnp.zeros_like(q_s)
        kv_s[...] = jnp.zeros_like(kv_s)
        uvg_s[...] = jnp.zeros_like(uvg_s)
        kvp_s[...] = jnp.zeros_like(kvp_s)

    first_of_seq = (jnp.maximum(t - 1, 0) % tiles_per_seq) == 0
    units = _mix_units(blocks, first_of_seq, sinks_ref, q_s, kv_s, kvp_s, uvg_s, lng_ref, lnb_ref,
                       ws_ref, bst_ref, mix_ref, gmv_ref)
    in_flight = []

    def before_chunk(c):
        for unit in in_flight:
            next(unit, None)
        first, last = c * len(units) // N_FF_CHUNKS, (c + 1) * len(units) // N_FF_CHUNKS
        in_flight[:] = units[first:last]
        for unit in in_flight:
            next(unit)

    x = x_ref[...]
    h = _rms(x, g1_ref[...]).astype(BF16)
    x1 = x + 0.5 * _swiglu(h, w1_ref, w3_ref, w2_ref, before_chunk)
    for unit in in_flight:
        next(unit, None)
    kvp_s[...] = kv_s[(blocks - 1) * WINDOW:, :]
    x1_ref[...] = x1
    h2 = _rms(x1, gm_ref[...]).astype(BF16)
    q_s[...] = ((_dot(h2, wq_ref[...]) + bq_ref[...]) * ATTN_SCALE).astype(BF16)
    kv = _dot(h2, wr_ref[:, :2 * KV_WIDTH]) + br_ref[:, :2 * KV_WIDTH]
    kv_s[...] = kv
    kvw_ref[...] = kv[(blocks - 1) * WINDOW:, :]
    uvg_s[...] = _dot(h2, wr_ref[:, 2 * KV_WIDTH:]) + br_ref[:, 2 * KV_WIDTH:]


def _in_mix(x, g1, w1, w3, w2, gmix, wq, wr, bq, br, sinks, ln_g, ln_b, w_s, b_st, tm):
    b, s, _ = x.shape
    nj = s // tm
    nt = b * nj
    blocks = tm // WINDOW

    def cur(t):
        u = jnp.minimum(t, nt - 1)
        return u // nj, u % nj

    def prev(t):
        u = jnp.maximum(t - 1, 0)
        return u // nj, u % nj

    def tile(width, which):
        return pl.BlockSpec((None, tm, width), lambda t: which(t) + (0,))

    def per_seq(rows, width, which):
        return pl.BlockSpec((None, rows, width), lambda t: (which(t)[0], 0, 0))

    consts = (g1, w1, w3, w2, gmix, wq, wr, bq, br)
    gconsts = (ln_g, ln_b, w_s, b_st)
    return pl.pallas_call(
        functools.partial(_in_mix_kernel, nj, blocks),
        out_shape=(jax.ShapeDtypeStruct((b, s, D_MODEL), F32),
                   jax.ShapeDtypeStruct((b, s, D_MODEL), BF16),
                   jax.ShapeDtypeStruct((b, WINDOW, 2 * KV_WIDTH), F32),
                   jax.ShapeDtypeStruct((b, CHUNK, GM_WIDTH), F32)),
        grid=(nt + 1,),
        in_specs=[tile(D_MODEL, cur)] + [_resident(c.shape) for c in consts]
        + [pl.BlockSpec(memory_space=pltpu.SMEM)] + [_resident(c.shape) for c in gconsts],
        out_specs=(tile(D_MODEL, cur), tile(D_MODEL, prev),
                   per_seq(WINDOW, 2 * KV_WIDTH, cur), per_seq(CHUNK, GM_WIDTH, prev)),
        scratch_shapes=[pltpu.VMEM((tm, ATTN_WIDTH), BF16), pltpu.VMEM((tm, 2 * KV_WIDTH), F32),
                        pltpu.VMEM((tm, 2 * GM_WIDTH), F32),
                        pltpu.VMEM((WINDOW, 2 * KV_WIDTH), F32)],
        compiler_params=pltpu.CompilerParams(
            dimension_semantics=("arbitrary",), vmem_limit_bytes=VMEM_LIMIT_BYTES),
        name="in_mix",
    )(x, *consts, sinks, *gconsts)


def _mix_sample_kernel(q8_ref, kn_ref, vn_ref, kc_ref, vc_ref, sinks_ref, uvg_ref, lng_ref,
                       lnb_ref, w00_ref, b0_ref, o8_ref, gm_ref, gmv_ref, kw_ref, vw_ref):
    q8 = q8_ref[...]
    kc = kc_ref[...]
    vc = vc_ref[...]
    kn = kn_ref[...]
    vn = vn_ref[...]
    sink = sinks_ref[...]
    s_c = jnp.einsum("bqd,bkd->bqk", q8, kc.astype(BF16), preferred_element_type=F32)
    s_n = jnp.sum(q8.astype(F32) * kn.astype(BF16).astype(F32), axis=-1, keepdims=True)
    m = jnp.maximum(jnp.maximum(jnp.max(s_c, axis=-1, keepdims=True), s_n), sink)
    e_c = jnp.exp(s_c - m)
    e_n = jnp.exp(s_n - m)
    inv = 1.0 / (jnp.sum(e_c, axis=-1, keepdims=True) + e_n + jnp.exp(sink - m))
    p_c = (e_c * inv).astype(BF16)
    p_n = (e_n * inv).astype(BF16).astype(F32)
    o = jnp.einsum("bqk,bkd->bqd", p_c, vc.astype(BF16), preferred_element_type=F32)
    o = o + p_n * vn.astype(BF16).astype(F32)
    row = lax.broadcasted_iota(jnp.int32, o.shape, 1)
    lane = lax.broadcasted_iota(jnp.int32, o.shape, 2)
    own = (lane < HEAD_DIM) == (row < N_Q_HEADS // N_KV_HEADS)
    o8_ref[...] = jnp.where(own, o, 0.0).astype(BF16)

    kw_ref[:, :WINDOW - 1, :] = kc[:, 1:, :]
    kw_ref[:, WINDOW - 1:, :] = kn
    vw_ref[:, :WINDOW - 1, :] = vc[:, 1:, :]
    vw_ref[:, WINDOW - 1:, :] = vn

    vgn = _gm_norm(uvg_ref[:, GM_WIDTH:], lng_ref[...], lnb_ref[...])
    gmv_ref[...] = vgn
    sm = w00_ref[...].astype(BF16).astype(F32) * vgn.astype(BF16).astype(F32) + b0_ref[...]
    gm_ref[...] = (_gelu(uvg_ref[:, :GM_WIDTH]) * sm).astype(BF16)


def _mix_sample(q8, kn, vn, kc, vc, sinks, uvg, ln_g, ln_b, w00, b0, bb):
    n = q8.shape[0]

    def b3(d1, d2):
        return pl.BlockSpec((bb, d1, d2), lambda i: (i, 0, 0))

    def b2(d):
        return pl.BlockSpec((bb, d), lambda i: (i, 0))

    return pl.pallas_call(
        _mix_sample_kernel,
        out_shape=(jax.ShapeDtypeStruct((n, N_Q_HEADS, LANES), BF16),
                   jax.ShapeDtypeStruct((n, GM_WIDTH), BF16),
                   jax.ShapeDtypeStruct((n, GM_WIDTH), F32),
                   jax.ShapeDtypeStruct((n, WINDOW, KV_WIDTH), F32),
                   jax.ShapeDtypeStruct((n, WINDOW, KV_WIDTH), F32)),
        grid=(n // bb,),
        in_specs=[b3(N_Q_HEADS, LANES), b3(1, KV_WIDTH), b3(1, KV_WIDTH),
                  b3(WINDOW, KV_WIDTH), b3(WINDOW, KV_WIDTH),
                  _resident(sinks.shape), b2(2 * GM_WIDTH), _resident(ln_g.shape),
                  _resident(ln_b.shape), _resident(w00.shape), _resident(b0.shape)],
        out_specs=(b3(N_Q_HEADS, LANES), b2(GM_WIDTH), b2(GM_WIDTH),
                   b3(WINDOW, KV_WIDTH), b3(WINDOW, KV_WIDTH)),
        compiler_params=pltpu.CompilerParams(
            dimension_semantics=("arbitrary",), vmem_limit_bytes=VMEM_LIMIT_BYTES),
        name="mix_sample",
    )(q8, kn, vn, kc, vc, sinks, uvg, ln_g, ln_b, w00, b0)


def _out_stage_kernel(x1_ref, mix_ref, p_ref, wo_ref, bo_ref, g2_ref, w1_ref, w3_ref, w2_ref,
                      gp_ref, wg_ref, wp_ref, gf_ref, y_ref):
    x2 = x1_ref[...] + _dot(mix_ref[...], wo_ref[...]) + bo_ref[...]
    h = _rms(x2, g2_ref[...]).astype(BF16)
    x3 = x2 + 0.5 * _swiglu(h, w1_ref, w3_ref, w2_ref)
    gate = _sigmoid(_dot(_rms(x3, gp_ref[...]).astype(BF16), wg_ref[...]))
    x4 = x3 + gate * _dot(p_ref[...].astype(BF16), wp_ref[...])
    y_ref[...] = _rms(x4, gf_ref[...])


def _out_stage(x1, mix, p, wo, bo, g2, w1, w3, w2, gp, wg, wp, gf, tm):
    b, s, _ = x1.shape

    def tile(width):
        return pl.BlockSpec((None, tm, width), lambda i, j: (i, j, 0))

    consts = (wo, bo, g2, w1, w3, w2, gp, wg, wp, gf)
    return pl.pallas_call(
        _out_stage_kernel,
        out_shape=jax.ShapeDtypeStruct((b, s, D_MODEL), F32),
        grid=(b, s // tm),
        in_specs=[tile(D_MODEL), tile(mix.shape[-1]), tile(PLE_DIM)]
        + [_resident(c.shape) for c in consts],
        out_specs=tile(D_MODEL),
        compiler_params=pltpu.CompilerParams(
            dimension_semantics=("arbitrary", "arbitrary"), vmem_limit_bytes=VMEM_LIMIT_BYTES),
        name="out_stage",
    )(x1, mix, p, *consts)


PROMPT_TILE = 512
SAMPLE_BLOCK = 32


def _pad_heads(w):
    lead = w.shape[:-1]
    wh = w.reshape(lead + (N_Q_HEADS, HEAD_DIM))
    z = jnp.zeros_like(wh[..., :N_Q_HEADS // 2, :])
    first = jnp.concatenate([wh[..., :N_Q_HEADS // 2, :], z], axis=-1)
    second = jnp.concatenate([z, wh[..., N_Q_HEADS // 2:, :]], axis=-1)
    return jnp.concatenate([first, second], axis=-2).reshape(lead + (N_Q_HEADS * LANES,))


def kernel(x_prompt, x_sample, cache_k_win, cache_v_win, p_prompt, p_sample, g_ffn1, w1_ffn1, w3_ffn1, w2_ffn1, g_mix, w_in, b_in, attn_sinks, gm_ln_g, gm_ln_b, gm_w_s, gm_b_s, w_out, b_out, g_ffn2, w1_ffn2, w3_ffn2, w2_ffn2, g_ple, w_ple_gate, w_ple_proj, g_final):
    depth = g_ffn1.shape[0]
    assert depth == 1
    i = 0
    nb, seq, _ = x_prompt.shape
    nd = x_sample.shape[0]

    row = lambda a: a.reshape(1, -1)
    g1, gmx, g2, gp, gf = row(g_ffn1[i]), row(g_mix[i]), row(g_ffn2[i]), row(g_ple[i]), row(g_final)
    w1a, w3a, w2a = w1_ffn1[i].astype(BF16), w3_ffn1[i].astype(BF16), w2_ffn1[i].astype(BF16)
    w1b, w3b, w2b = w1_ffn2[i].astype(BF16), w3_ffn2[i].astype(BF16), w2_ffn2[i].astype(BF16)
    wq = w_in[i][:, :ATTN_WIDTH].astype(BF16)
    wr = w_in[i][:, ATTN_WIDTH:].astype(BF16)
    bq, br = row(b_in[i][:ATTN_WIDTH]), row(b_in[i][ATTN_WIDTH:])
    wo, bo = w_out[i].astype(BF16), row(b_out[i])
    wg, wp = w_ple_gate[i].astype(BF16), w_ple_proj[i].astype(BF16)
    ln_g, ln_b = row(gm_ln_g[i]), row(gm_ln_b[i])
    sinks = attn_sinks[i]

    x1, mix, kv_last, gmv_p = _in_mix(x_prompt, g1, w1a, w3a, w2a, gmx, wq, wr, bq, br, sinks,
                                      ln_g, ln_b, gm_w_s[i], gm_b_s[i].T, PROMPT_TILE)
    y_prompt = _out_stage(x1, mix, p_prompt[i], wo, bo, g2, w1b, w3b, w2b, gp, wg, wp, gf,
                          2 * PROMPT_TILE)
    k_win_p = kv_last[:, :, :KV_WIDTH].reshape(1, nb, WINDOW, N_KV_HEADS, HEAD_DIM)
    v_win_p = kv_last[:, :, KV_WIDTH:].reshape(1, nb, WINDOW, N_KV_HEADS, HEAD_DIM)

    wq8, bq8 = _pad_heads(w_in[i][:, :ATTN_WIDTH]).astype(BF16), row(_pad_heads(b_in[i][:ATTN_WIDTH]))
    wo8 = jnp.concatenate([_pad_heads(w_out[i][:ATTN_WIDTH].T).T, w_out[i][ATTN_WIDTH:]],
                          axis=0).astype(BF16)
    xs = x_sample.reshape(1, nd, D_MODEL)
    x1s, q8, kvs, uvgs = _in_stage(xs, g1, w1a, w3a, w2a, gmx, wq8, wr, bq8, br, nd)
    kvs = kvs.reshape(nd, 1, 2 * KV_WIDTH)
    o8, gm_s, gmv_s, kw_s, vw_s = _mix_sample(
        q8.reshape(nd, N_Q_HEADS, LANES), kvs[:, :, :KV_WIDTH], kvs[:, :, KV_WIDTH:],
        cache_k_win[i].reshape(nd, WINDOW, KV_WIDTH), cache_v_win[i].reshape(nd, WINDOW, KV_WIDTH),
        sinks.reshape(N_Q_HEADS, 1), uvgs.reshape(nd, 2 * GM_WIDTH), ln_g, ln_b,
        jnp.repeat(gm_w_s[i][:, 0, 0], GM_DIM).reshape(1, GM_WIDTH),
        jnp.repeat(gm_b_s[i][:, 0], GM_DIM).reshape(1, GM_WIDTH), SAMPLE_BLOCK)
    mix_s = jnp.concatenate([o8.reshape(nd, N_Q_HEADS * LANES), gm_s], axis=-1)
    y_sample = _out_stage(x1s, mix_s.reshape(1, nd, -1), p_sample[i].reshape(1, nd, PLE_DIM),
                          wo8, bo, g2, w1b, w3b, w2b, gp, wg, wp, gf, nd)

    return (y_prompt, y_sample.reshape(nd, 1, D_MODEL), k_win_p, v_win_p,
            kw_s.reshape(1, nd, WINDOW, N_KV_HEADS, HEAD_DIM),
            vw_s.reshape(1, nd, WINDOW, N_KV_HEADS, HEAD_DIM),
            gmv_p.reshape(1, nb, CHUNK, GM_WIDTH), gmv_s.reshape(1, nd, 1, GM_WIDTH))
```

```python
import functools
import math

import jax
import jax.numpy as jnp
from jax import lax
from jax.experimental import pallas as pl
from jax.experimental.pallas import tpu as pltpu

F32 = jnp.float32
BF16 = jnp.bfloat16

D_MODEL = 1024
HEAD_DIM = 64
N_Q_HEADS = 8
N_KV_HEADS = 2
ATTN_WIDTH = N_Q_HEADS * HEAD_DIM
KV_WIDTH = N_KV_HEADS * HEAD_DIM
WINDOW = 128
GM_WIDTH = 512
GM_GROUPS = 4
GM_DIM = 128
CHUNK = 128
D_FF = 2816
PLE_DIM = 256
RMS_EPS = 1e-6
LN_EPS = 1e-5
ATTN_SCALE = HEAD_DIM ** -0.5

LANES = 128
FF_CHUNK = 512
FF_BOUNDS = tuple((lo, min(lo + FF_CHUNK, D_FF)) for lo in range(0, D_FF, FF_CHUNK))
N_FF_CHUNKS = len(FF_BOUNDS)
VMEM_LIMIT_BYTES = 60 * 1024 * 1024

_GELU_C = math.sqrt(2.0 / math.pi)


def _rms(x, g):
    return x * lax.rsqrt(jnp.mean(x * x, axis=-1, keepdims=True) + RMS_EPS) * g


def _sigmoid(x):
    return 1.0 / (1.0 + jnp.exp(-x))


def _gelu(x):
    return 0.5 * x * (1.0 + jnp.tanh(_GELU_C * (x + 0.044715 * (x * x * x))))


def _dot(a, b):
    return jnp.dot(a, b, preferred_element_type=F32)


def _dot_nt(a, b):
    return lax.dot_general(a, b, (((1,), (1,)), ((), ())), preferred_element_type=F32)


def _swiglu(h, w1_ref, w3_ref, w2_ref, before_chunk=None):
    acc = jnp.zeros((h.shape[0], D_MODEL), F32)
    for c, (lo, hi) in enumerate(FF_BOUNDS):
        if before_chunk is not None:
            before_chunk(c)
        cols = slice(lo, hi)
        a = _dot(h, w1_ref[:, cols])
        b = _dot(h, w3_ref[:, cols])
        act = (a * _sigmoid(a) * b).astype(BF16)
        acc = acc + _dot(act, w2_ref[cols, :])
    return acc


def _resident(shape):
    zeros = (0,) * len(shape)
    return pl.BlockSpec(shape, lambda *_: zeros, pipeline_mode=pl.Buffered(1))


CAST_STEPS = 8
BF16_SUBLANES = 16


def _cast_spec(w):
    rows = w.shape[0] // CAST_STEPS
    assert rows * CAST_STEPS == w.shape[0] and rows % BF16_SUBLANES == 0
    return pl.BlockSpec((rows, w.shape[1]), lambda s: (jnp.minimum(s, CAST_STEPS - 1), 0))


def _cast_chunk(s, src_ref, dst_ref):
    rows = src_ref.shape[0]
    start = pl.multiple_of(s * rows, BF16_SUBLANES)
    dst_ref[pl.ds(start, rows), :] = src_ref[...].astype(BF16)


def _bf16_scratch(w):
    return pltpu.VMEM(w.shape, BF16)


def _gm_norm(vg, ln_g, ln_b):
    a = _gelu(vg)
    mu = jnp.mean(a, axis=-1, keepdims=True)
    ac = a - mu
    return ac * lax.rsqrt(jnp.mean(ac * ac, axis=-1, keepdims=True) + LN_EPS) * ln_g + ln_b


def _mix_units(blocks, first_of_seq, sinks_ref, q_ref, kv_ref, kvp_ref, uvg_ref, lng_ref,
               lnb_ref, ws_ref, bst_ref, mix_ref, gmv_ref):
    lane = lax.broadcasted_iota(jnp.int32, (2 * WINDOW, LANES), 1)
    lo = lane < HEAD_DIM
    qi = lax.broadcasted_iota(jnp.int32, (WINDOW, 2 * WINDOW), 0)
    kj = lax.broadcasted_iota(jnp.int32, (WINDOW, 2 * WINDOW), 1)
    band = (kj >= qi) & (kj <= qi + WINDOW)
    ti = lax.broadcasted_iota(jnp.int32, (CHUNK, CHUNK), 0)
    si = lax.broadcasted_iota(jnp.int32, (CHUNK, CHUNK), 1)
    tril = si <= ti

    block_cache = {}

    def block_operands(blk):
        if blk not in block_cache:
            if blk == 0:
                kv2 = jnp.concatenate([kvp_ref[...], kv_ref[:WINDOW, :]], axis=0)
                mask = band & (kj >= jnp.where(first_of_seq, WINDOW, 0))
            else:
                kv2 = kv_ref[(blk - 1) * WINDOW:(blk + 1) * WINDOW, :]
                mask = band
            kt = kv2[:, :KV_WIDTH].T.astype(BF16)
            v2 = kv2[:, KV_WIDTH:].astype(BF16)
            v2r = pltpu.roll(v2, HEAD_DIM, 1)
            zero_v = jnp.zeros_like(v2)
            zero_k = jnp.zeros((HEAD_DIM, 2 * WINDOW), BF16)
            heads = [kt[h * HEAD_DIM:(h + 1) * HEAD_DIM, :] for h in range(N_KV_HEADS)]
            k_lo = [jnp.concatenate([kh, zero_k], axis=0) for kh in heads]
            k_hi = [jnp.concatenate([zero_k, kh], axis=0) for kh in heads]
            v_lo = (jnp.where(lo, v2, zero_v), jnp.where(lo, v2r, zero_v))
            v_hi = (jnp.where(lo, zero_v, v2r), jnp.where(lo, zero_v, v2))
            mask2 = jnp.concatenate([mask, mask], axis=0)
            block_cache[blk] = (mask2, k_lo, k_hi, v_lo, v_hi)
        return block_cache[blk]

    upper = lax.broadcasted_iota(jnp.int32, (2 * WINDOW, 1), 0) < WINDOW

    def attn_unit(blk, h):
        rows = slice(blk * WINDOW, (blk + 1) * WINDOW)
        mask2, k_lo, k_hi, v_lo, v_hi = block_operands(blk)
        slots = (2 * h, 2 * h + 1)
        q2 = jnp.concatenate([q_ref[rows, s * LANES:(s + 1) * LANES] for s in slots], axis=0)
        scores = [_dot(q2, kk) for kk in (k_lo[h], k_hi[h])]
        yield
        out = None
        for half, vv in enumerate((v_lo[h], v_hi[h])):
            sink = jnp.where(upper, sinks_ref[2 * slots[0] + half], sinks_ref[2 * slots[1] + half])
            s = jnp.where(mask2, scores[half], -jnp.inf)
            m = jnp.maximum(jnp.max(s, axis=-1, keepdims=True), sink)
            e = jnp.exp(s - m)
            denom = jnp.sum(e, axis=-1, keepdims=True) + jnp.exp(sink - m)
            p = (e * (1.0 / denom)).astype(BF16)
            o = _dot(p, vv)
            out = o if out is None else out + o
        out = out.astype(BF16)
        mix_ref[rows, slots[0] * LANES:(slots[0] + 1) * LANES] = out[:WINDOW]
        mix_ref[rows, slots[1] * LANES:(slots[1] + 1) * LANES] = out[WINDOW:]

    def gm_unit(blk0):
        pair = (blk0, blk0 + 1)
        rows = [slice(blk * WINDOW, (blk + 1) * WINDOW) for blk in pair]
        vgn = [_gm_norm(uvg_ref[r, GM_WIDTH:], lng_ref[...], lnb_ref[...]) for r in rows]
        if pair[1] == blocks - 1:
            gmv_ref[...] = vgn[1]
        vgb = [v.astype(BF16) for v in vgn]
        gu = [_gelu(uvg_ref[r, :GM_WIDTH]) for r in rows]
        yield
        for g in range(GM_GROUPS):
            cols = slice(g * GM_DIM, (g + 1) * GM_DIM)
            w = jnp.where(tril, ws_ref[g], 0.0).astype(BF16)
            sm = _dot(w, jnp.concatenate([v[:, cols] for v in vgb], axis=1)) + bst_ref[:, g:g + 1]
            for i, r in enumerate(rows):
                mix_ref[r, ATTN_WIDTH + g * GM_DIM:ATTN_WIDTH + (g + 1) * GM_DIM] = (
                    gu[i][:, cols] * sm[:, i * GM_DIM:(i + 1) * GM_DIM]).astype(BF16)

    assert blocks % 2 == 0
    units = []
    for blk0 in range(0, blocks, 2):
        units += [attn_unit(blk, h) for blk in (blk0, blk0 + 1) for h in range(N_KV_HEADS)]
        units.append(gm_unit(blk0))
    return units


KV_COL = ATTN_WIDTH
UVG_COL = ATTN_WIDTH + 2 * KV_WIDTH


def _in_mix_kernel(tiles_per_seq, blocks, n_tiles,
                   x_ref, g1_ref, w1c_ref, w3c_ref, w2c_ref, gm_ref, winc_ref, bin_ref,
                   sinks_ref, lng_ref, lnb_ref, ws_ref, bst_ref, xs_ref, wq8_ref, bq8_ref,
                   x1_ref, mix_ref, kvw_ref, gmv_ref, x1s_ref, q8s_ref, kvs_ref, uvgs_ref,
                   w1_s, w3_s, w2_s, win_s, q_s, kv_s, uvg_s, kvp_s):
    s = pl.program_id(0)
    t = s - CAST_STEPS

    @pl.when(s < CAST_STEPS)
    def _():
        for src, dst in ((w1c_ref, w1_s), (w3c_ref, w3_s), (w2c_ref, w2_s), (winc_ref, win_s)):
            _cast_chunk(s, src, dst)

    @pl.when(t == 0)
    def _():
        q_s[...] = jnp.zeros_like(q_s)
        kv_s[...] = jnp.zeros_like(kv_s)
        uvg_s[...] = jnp.zeros_like(uvg_s)
        kvp_s[...] = jnp.zeros_like(kvp_s)

    def ffn(x, before_chunk=None):
        h = _rms(x, g1_ref[...]).astype(BF16)
        return x + 0.5 * _swiglu(h, w1_s, w3_s, w2_s, before_chunk)

    def project(h2, cols):
        return _dot(h2, win_s[:, cols]) + bin_ref[:, cols]

    @pl.when((t >= 0) & (t <= n_tiles))
    def _():
        first_of_seq = (jnp.maximum(t - 1, 0) % tiles_per_seq) == 0
        units = _mix_units(blocks, first_of_seq, sinks_ref, q_s, kv_s, kvp_s, uvg_s, lng_ref,
                           lnb_ref, ws_ref, bst_ref, mix_ref, gmv_ref)
        in_flight = []

        def before_chunk(c):
            for unit in in_flight:
                next(unit, None)
            first, last = c * len(units) // N_FF_CHUNKS, (c + 1) * len(units) // N_FF_CHUNKS
            in_flight[:] = units[first:last]
            for unit in in_flight:
                next(unit)

        x1 = ffn(x_ref[...], before_chunk)
        for unit in in_flight:
            next(unit, None)
        kvp_s[...] = kv_s[(blocks - 1) * WINDOW:, :]
        x1_ref[...] = x1
        h2 = _rms(x1, gm_ref[...]).astype(BF16)
        q_s[...] = (project(h2, slice(0, KV_COL)) * ATTN_SCALE).astype(BF16)
        kv = project(h2, slice(KV_COL, UVG_COL))
        kv_s[...] = kv
        kvw_ref[...] = kv[(blocks - 1) * WINDOW:, :]
        uvg_s[...] = project(h2, slice(UVG_COL, None))

    @pl.when(t == n_tiles + 1)
    def _():
        x1 = ffn(xs_ref[...])
        x1s_ref[...] = x1
        h2 = _rms(x1, gm_ref[...]).astype(BF16)
        q8s_ref[...] = ((_dot(h2, wq8_ref[...]) + bq8_ref[...]) * ATTN_SCALE).astype(BF16)
        kvs_ref[...] = project(h2, slice(KV_COL, UVG_COL))
        uvgs_ref[...] = project(h2, slice(UVG_COL, None))


def _in_mix(x, xs, g1, w1, w3, w2, gmix, w_in, b_in, sinks, ln_g, ln_b, w_s, b_st, wq8, bq8, tm):
    b, s, _ = x.shape
    n = xs.shape[0]
    nj = s // tm
    nt = b * nj
    blocks = tm // WINDOW

    def cur(step):
        u = jnp.clip(step - CAST_STEPS, 0, nt - 1)
        return u // nj, u % nj

    def prev(step):
        u = jnp.clip(step - CAST_STEPS - 1, 0, nt - 1)
        return u // nj, u % nj

    def tile(width, which):
        return pl.BlockSpec((None, tm, width), lambda step: which(step) + (0,))

    def per_seq(rows, width, which):
        return pl.BlockSpec((None, rows, width), lambda step: (which(step)[0], 0, 0))

    def whole(shape):
        return pl.BlockSpec(shape, lambda step: (0,) * len(shape))

    gconsts = (ln_g, ln_b, w_s, b_st)
    sample_out = ((n, D_MODEL), F32), ((n, N_Q_HEADS * LANES), BF16), ((n, 2 * KV_WIDTH), F32), (
        (n, 2 * GM_WIDTH), F32)
    return pl.pallas_call(
        functools.partial(_in_mix_kernel, nj, blocks, nt),
        out_shape=(jax.ShapeDtypeStruct((b, s, D_MODEL), F32),
                   jax.ShapeDtypeStruct((b, s, D_MODEL), BF16),
                   jax.ShapeDtypeStruct((b, WINDOW, 2 * KV_WIDTH), F32),
                   jax.ShapeDtypeStruct((b, CHUNK, GM_WIDTH), F32))
        + tuple(jax.ShapeDtypeStruct(shape, dtype) for shape, dtype in sample_out),
        grid=(CAST_STEPS + nt + 2,),
        in_specs=[tile(D_MODEL, cur), _resident(g1.shape), _cast_spec(w1), _cast_spec(w3),
                  _cast_spec(w2), _resident(gmix.shape), _cast_spec(w_in), _resident(b_in.shape),
                  pl.BlockSpec(memory_space=pltpu.SMEM)]
        + [_resident(c.shape) for c in gconsts + (xs, wq8, bq8)],
        out_specs=(tile(D_MODEL, cur), tile(D_MODEL, prev),
                   per_seq(WINDOW, 2 * KV_WIDTH, cur), per_seq(CHUNK, GM_WIDTH, prev))
        + tuple(whole(shape) for shape, _ in sample_out),
        scratch_shapes=[_bf16_scratch(w1), _bf16_scratch(w3), _bf16_scratch(w2),
                        _bf16_scratch(w_in),
                        pltpu.VMEM((tm, ATTN_WIDTH), BF16), pltpu.VMEM((tm, 2 * KV_WIDTH), F32),
                        pltpu.VMEM((tm, 2 * GM_WIDTH), F32),
                        pltpu.VMEM((WINDOW, 2 * KV_WIDTH), F32)],
        compiler_params=pltpu.CompilerParams(
            dimension_semantics=("arbitrary",), vmem_limit_bytes=VMEM_LIMIT_BYTES),
        name="in_mix",
    )(x, g1, w1, w3, w2, gmix, w_in, b_in, sinks, *gconsts, xs, wq8, bq8)


def _mix_sample_kernel(q8_ref, kn_ref, vn_ref, kc_ref, vc_ref, sinks_ref, uvg_ref, lng_ref,
                       lnb_ref, w00_ref, b0_ref, o8_ref, gm_ref, gmv_ref, kw_ref, vw_ref):
    q8 = q8_ref[...]
    kc = kc_ref[...]
    vc = vc_ref[...]
    kn = kn_ref[...]
    vn = vn_ref[...]
    sink = sinks_ref[...]
    s_c = jnp.einsum("bqd,bkd->bqk", q8, kc.astype(BF16), preferred_element_type=F32)
    s_n = jnp.sum(q8.astype(F32) * kn.astype(BF16).astype(F32), axis=-1, keepdims=True)
    m = jnp.maximum(jnp.maximum(jnp.max(s_c, axis=-1, keepdims=True), s_n), sink)
    e_c = jnp.exp(s_c - m)
    e_n = jnp.exp(s_n - m)
    inv = 1.0 / (jnp.sum(e_c, axis=-1, keepdims=True) + e_n + jnp.exp(sink - m))
    p_c = (e_c * inv).astype(BF16)
    p_n = (e_n * inv).astype(BF16).astype(F32)
    o = jnp.einsum("bqk,bkd->bqd", p_c, vc.astype(BF16), preferred_element_type=F32)
    o = o + p_n * vn.astype(BF16).astype(F32)
    row = lax.broadcasted_iota(jnp.int32, o.shape, 1)
    lane = lax.broadcasted_iota(jnp.int32, o.shape, 2)
    own = (lane < HEAD_DIM) == (row < N_Q_HEADS // N_KV_HEADS)
    o8_ref[...] = jnp.where(own, o, 0.0).astype(BF16)

    kw_ref[:, :WINDOW - 1, :] = kc[:, 1:, :]
    kw_ref[:, WINDOW - 1:, :] = kn
    vw_ref[:, :WINDOW - 1, :] = vc[:, 1:, :]
    vw_ref[:, WINDOW - 1:, :] = vn

    vgn = _gm_norm(uvg_ref[:, GM_WIDTH:], lng_ref[...], lnb_ref[...])
    gmv_ref[...] = vgn
    sm = w00_ref[...].astype(BF16).astype(F32) * vgn.astype(BF16).astype(F32) + b0_ref[...]
    gm_ref[...] = (_gelu(uvg_ref[:, :GM_WIDTH]) * sm).astype(BF16)


def _mix_sample(q8, kn, vn, kc, vc, sinks, uvg, ln_g, ln_b, w00, b0, bb):
    n = q8.shape[0]

    def b3(d1, d2):
        return pl.BlockSpec((bb, d1, d2), lambda i: (i, 0, 0))

    def b2(d):
        return pl.BlockSpec((bb, d), lambda i: (i, 0))

    return pl.pallas_call(
        _mix_sample_kernel,
        out_shape=(jax.ShapeDtypeStruct((n, N_Q_HEADS, LANES), BF16),
                   jax.ShapeDtypeStruct((n, GM_WIDTH), BF16),
                   jax.ShapeDtypeStruct((n, GM_WIDTH), F32),
                   jax.ShapeDtypeStruct((n, WINDOW, KV_WIDTH), F32),
                   jax.ShapeDtypeStruct((n, WINDOW, KV_WIDTH), F32)),
        grid=(n // bb,),
        in_specs=[b3(N_Q_HEADS, LANES), b3(1, KV_WIDTH), b3(1, KV_WIDTH),
                  b3(WINDOW, KV_WIDTH), b3(WINDOW, KV_WIDTH),
                  _resident(sinks.shape), b2(2 * GM_WIDTH), _resident(ln_g.shape),
                  _resident(ln_b.shape), _resident(w00.shape), _resident(b0.shape)],
        out_specs=(b3(N_Q_HEADS, LANES), b2(GM_WIDTH), b2(GM_WIDTH),
                   b3(WINDOW, KV_WIDTH), b3(WINDOW, KV_WIDTH)),
        compiler_params=pltpu.CompilerParams(
            dimension_semantics=("arbitrary",), vmem_limit_bytes=VMEM_LIMIT_BYTES),
        name="mix_sample",
    )(q8, kn, vn, kc, vc, sinks, uvg, ln_g, ln_b, w00, b0)


def _out_stage_kernel(n_tiles, x1_ref, mix_ref, p_ref, woc_ref, bo_ref, g2_ref, w1c_ref, w3c_ref,
                      w2c_ref, gp_ref, wgc_ref, wpc_ref, gf_ref, x1s_ref, mixs_ref, ps_ref, wo8_ref,
                      y_ref, ys_ref, wo_s, w1_s, w3_s, w2_s, wg_s, wp_s):
    s = pl.program_id(0)
    t = s - CAST_STEPS

    @pl.when(s < CAST_STEPS)
    def _():
        for src, dst in ((woc_ref, wo_s), (w1c_ref, w1_s), (w3c_ref, w3_s), (w2c_ref, w2_s),
                         (wgc_ref, wg_s), (wpc_ref, wp_s)):
            _cast_chunk(s, src, dst)

    def out_stage(x1, mix, p, wo_ref):
        x2 = x1 + _dot(mix, wo_ref[...]) + bo_ref[...]
        h = _rms(x2, g2_ref[...]).astype(BF16)
        x3 = x2 + 0.5 * _swiglu(h, w1_s, w3_s, w2_s)
        gate = _sigmoid(_dot(_rms(x3, gp_ref[...]).astype(BF16), wg_s[...]))
        x4 = x3 + gate * _dot(p.astype(BF16), wp_s[...])
        return _rms(x4, gf_ref[...])

    @pl.when((t >= 0) & (t < n_tiles))
    def _():
        y_ref[...] = out_stage(x1_ref[...], mix_ref[...], p_ref[...], wo_s)

    @pl.when(t == n_tiles)
    def _():
        ys_ref[...] = out_stage(x1s_ref[...], mixs_ref[...], ps_ref[...], wo8_ref)


def _out_stage(x1, mix, p, x1s, mixs, ps, wo, bo, g2, w1, w3, w2, gp, wg, wp, gf, wo8, tm):
    b, s, _ = x1.shape
    n = x1s.shape[0]
    nj = s // tm
    nt = b * nj

    def tile(width):
        def index(step):
            u = jnp.clip(step - CAST_STEPS, 0, nt - 1)
            return u // nj, u % nj, 0
        return pl.BlockSpec((None, tm, width), index)

    return pl.pallas_call(
        functools.partial(_out_stage_kernel, nt),
        out_shape=(jax.ShapeDtypeStruct((b, s, D_MODEL), F32),
                   jax.ShapeDtypeStruct((n, D_MODEL), F32)),
        grid=(CAST_STEPS + nt + 1,),
        in_specs=[tile(D_MODEL), tile(D_MODEL), tile(PLE_DIM), _cast_spec(wo),
                  _resident(bo.shape), _resident(g2.shape), _cast_spec(w1), _cast_spec(w3),
                  _cast_spec(w2), _resident(gp.shape), _cast_spec(wg), _cast_spec(wp),
                  _resident(gf.shape)] + [_resident(a.shape) for a in (x1s, mixs, ps, wo8)],
        out_specs=(tile(D_MODEL), pl.BlockSpec((n, D_MODEL), lambda step: (0, 0))),
        scratch_shapes=[_bf16_scratch(w) for w in (wo, w1, w3, w2, wg, wp)],
        compiler_params=pltpu.CompilerParams(
            dimension_semantics=("arbitrary",), vmem_limit_bytes=VMEM_LIMIT_BYTES),
        name="out_stage",
    )(x1, mix, p, wo, bo, g2, w1, w3, w2, gp, wg, wp, gf, x1s, mixs, ps, wo8)


PROMPT_TILE = 512
SAMPLE_BLOCK = 32


def _pad_heads(w):
    lead = w.shape[:-1]
    wh = w.reshape(lead + (N_Q_HEADS, HEAD_DIM))
    z = jnp.zeros_like(wh[..., :N_Q_HEADS // 2, :])
    first = jnp.concatenate([wh[..., :N_Q_HEADS // 2, :], z], axis=-1)
    second = jnp.concatenate([z, wh[..., N_Q_HEADS // 2:, :]], axis=-1)
    return jnp.concatenate([first, second], axis=-2).reshape(lead + (N_Q_HEADS * LANES,))


def kernel(x_prompt, x_sample, cache_k_win, cache_v_win, p_prompt, p_sample, g_ffn1, w1_ffn1, w3_ffn1, w2_ffn1, g_mix, w_in, b_in, attn_sinks, gm_ln_g, gm_ln_b, gm_w_s, gm_b_s, w_out, b_out, g_ffn2, w1_ffn2, w3_ffn2, w2_ffn2, g_ple, w_ple_gate, w_ple_proj, g_final):
    depth = g_ffn1.shape[0]
    assert depth == 1
    i = 0
    nb, seq, _ = x_prompt.shape
    nd = x_sample.shape[0]

    row = lambda a: a.reshape(1, -1)
    g1, gmx, g2, gp, gf = row(g_ffn1[i]), row(g_mix[i]), row(g_ffn2[i]), row(g_ple[i]), row(g_final)
    ln_g, ln_b = row(gm_ln_g[i]), row(gm_ln_b[i])
    sinks = attn_sinks[i]
    wq8, bq8 = _pad_heads(w_in[i][:, :ATTN_WIDTH]).astype(BF16), row(_pad_heads(b_in[i][:ATTN_WIDTH]))
    wo8 = jnp.concatenate([_pad_heads(w_out[i][:ATTN_WIDTH].T).T, w_out[i][ATTN_WIDTH:]],
                          axis=0).astype(BF16)

    x1, mix, kv_last, gmv_p, x1s, q8, kvs, uvgs = _in_mix(
        x_prompt, x_sample.reshape(nd, D_MODEL), g1, w1_ffn1[i], w3_ffn1[i], w2_ffn1[i], gmx,
        w_in[i], row(b_in[i]), sinks, ln_g, ln_b, gm_w_s[i], gm_b_s[i].T, wq8, bq8, PROMPT_TILE)
    k_win_p = kv_last[:, :, :KV_WIDTH].reshape(1, nb, WINDOW, N_KV_HEADS, HEAD_DIM)
    v_win_p = kv_last[:, :, KV_WIDTH:].reshape(1, nb, WINDOW, N_KV_HEADS, HEAD_DIM)

    kvs = kvs.reshape(nd, 1, 2 * KV_WIDTH)
    o8, gm_s, gmv_s, kw_s, vw_s = _mix_sample(
        q8.reshape(nd, N_Q_HEADS, LANES), kvs[:, :, :KV_WIDTH], kvs[:, :, KV_WIDTH:],
        cache_k_win[i].reshape(nd, WINDOW, KV_WIDTH), cache_v_win[i].reshape(nd, WINDOW, KV_WIDTH),
        sinks.reshape(N_Q_HEADS, 1), uvgs, ln_g, ln_b,
        jnp.repeat(gm_w_s[i][:, 0, 0], GM_DIM).reshape(1, GM_WIDTH),
        jnp.repeat(gm_b_s[i][:, 0], GM_DIM).reshape(1, GM_WIDTH), SAMPLE_BLOCK)
    mix_s = jnp.concatenate([o8.reshape(nd, N_Q_HEADS * LANES), gm_s], axis=-1)

    y_prompt, y_sample = _out_stage(
        x1, mix, p_prompt[i], x1s, mix_s, p_sample[i].reshape(nd, PLE_DIM), w_out[i], row(b_out[i]),
        g2, w1_ffn2[i], w3_ffn2[i], w2_ffn2[i], gp, w_ple_gate[i], w_ple_proj[i], gf, wo8,
        PROMPT_TILE)

    return (y_prompt, y_sample.reshape(nd, 1, D_MODEL), k_win_p, v_win_p,
            kw_s.reshape(1, nd, WINDOW, N_KV_HEADS, HEAD_DIM),
            vw_s.reshape(1, nd, WINDOW, N_KV_HEADS, HEAD_DIM),
            gmv_p.reshape(1, nb, CHUNK, GM_WIDTH), gmv_s.reshape(1, nd, 1, GM_WIDTH))
```

```python
import functools
import math

import jax
import jax.numpy as jnp
from jax import lax
from jax.experimental import pallas as pl
from jax.experimental.pallas import tpu as pltpu

F32 = jnp.float32
BF16 = jnp.bfloat16

D_MODEL = 1024
HEAD_DIM = 64
N_Q_HEADS = 8
N_KV_HEADS = 2
ATTN_WIDTH = N_Q_HEADS * HEAD_DIM
KV_WIDTH = N_KV_HEADS * HEAD_DIM
WINDOW = 128
GM_WIDTH = 512
GM_GROUPS = 4
GM_DIM = 128
CHUNK = 128
D_FF = 2816
PLE_DIM = 256
RMS_EPS = 1e-6
LN_EPS = 1e-5
ATTN_SCALE = HEAD_DIM ** -0.5

LANES = 128
FF_CHUNK = 512
FF_BOUNDS = tuple((lo, min(lo + FF_CHUNK, D_FF)) for lo in range(0, D_FF, FF_CHUNK))
N_FF_CHUNKS = len(FF_BOUNDS)
VMEM_LIMIT_BYTES = 60 * 1024 * 1024

_GELU_C = math.sqrt(2.0 / math.pi)


def _rms(x, g):
    return x * lax.rsqrt(jnp.mean(x * x, axis=-1, keepdims=True) + RMS_EPS) * g


def _sigmoid(x):
    return 1.0 / (1.0 + jnp.exp(-x))


def _gelu(x):
    return 0.5 * x * (1.0 + jnp.tanh(_GELU_C * (x + 0.044715 * (x * x * x))))


def _dot(a, b):
    return jnp.dot(a, b, preferred_element_type=F32)


def _dot_nt(a, b):
    return lax.dot_general(a, b, (((1,), (1,)), ((), ())), preferred_element_type=F32)


def _swiglu(h, w1_ref, w3_ref, w2_ref, before_chunk=None):
    acc = jnp.zeros((h.shape[0], D_MODEL), F32)
    for c, (lo, hi) in enumerate(FF_BOUNDS):
        if before_chunk is not None:
            before_chunk(c)
        cols = slice(lo, hi)
        a = _dot(h, w1_ref[:, cols])
        b = _dot(h, w3_ref[:, cols])
        act = (a * _sigmoid(a) * b).astype(BF16)
        acc = acc + _dot(act, w2_ref[cols, :])
    return acc


def _resident(shape):
    zeros = (0,) * len(shape)
    return pl.BlockSpec(shape, lambda *_: zeros, pipeline_mode=pl.Buffered(1))


CAST_STEPS = 4
BF16_SUBLANES = 16


CAST_BUFFERS = 1


def _cast_spec(w):
    rows = w.shape[0] // CAST_STEPS
    assert rows * CAST_STEPS == w.shape[0] and rows % BF16_SUBLANES == 0
    return pl.BlockSpec((rows, w.shape[1]), lambda s: (jnp.minimum(s, CAST_STEPS - 1), 0),
                        pipeline_mode=pl.Buffered(CAST_BUFFERS))


def _trickle_rows(w, max_chunks):
    rows = BF16_SUBLANES
    while w.shape[0] % rows or w.shape[0] // rows > max_chunks:
        rows += BF16_SUBLANES
    return rows


def _trickle_spec(w, max_chunks):
    rows = _trickle_rows(w, max_chunks)
    last = w.shape[0] // rows - 1
    return pl.BlockSpec((rows, w.shape[1]), lambda s: (jnp.clip(s - CAST_STEPS, 0, last), 0))


def _cast_chunk(s, src_ref, dst_ref):
    rows = src_ref.shape[0]
    start = pl.multiple_of(s * rows, BF16_SUBLANES)
    dst_ref[pl.ds(start, rows), :] = src_ref[...].astype(BF16)


def _bf16_scratch(w):
    return pltpu.VMEM(w.shape, BF16)


def _gm_norm(vg, ln_g, ln_b):
    a = _gelu(vg)
    mu = jnp.mean(a, axis=-1, keepdims=True)
    ac = a - mu
    return ac * lax.rsqrt(jnp.mean(ac * ac, axis=-1, keepdims=True) + LN_EPS) * ln_g + ln_b


def _mix_units(blocks, first_of_seq, sinks_ref, q_ref, kv_ref, kvp_ref, uvg_ref, lng_ref,
               lnb_ref, ws_ref, bst_ref, mix_ref, gmv_ref):
    lane = lax.broadcasted_iota(jnp.int32, (2 * WINDOW, LANES), 1)
    lo = lane < HEAD_DIM
    qi = lax.broadcasted_iota(jnp.int32, (WINDOW, 2 * WINDOW), 0)
    kj = lax.broadcasted_iota(jnp.int32, (WINDOW, 2 * WINDOW), 1)
    band = (kj >= qi) & (kj <= qi + WINDOW)
    ti = lax.broadcasted_iota(jnp.int32, (CHUNK, CHUNK), 0)
    si = lax.broadcasted_iota(jnp.int32, (CHUNK, CHUNK), 1)
    tril = si <= ti

    block_cache = {}

    def block_operands(blk):
        if blk not in block_cache:
            if blk == 0:
                kv2 = jnp.concatenate([kvp_ref[...], kv_ref[:WINDOW, :]], axis=0)
                mask = band & (kj >= jnp.where(first_of_seq, WINDOW, 0))
            else:
                kv2 = kv_ref[(blk - 1) * WINDOW:(blk + 1) * WINDOW, :]
                mask = band
            kt = kv2[:, :KV_WIDTH].T.astype(BF16)
            v2 = kv2[:, KV_WIDTH:].astype(BF16)
            v2r = pltpu.roll(v2, HEAD_DIM, 1)
            zero_v = jnp.zeros_like(v2)
            zero_k = jnp.zeros((HEAD_DIM, 2 * WINDOW), BF16)
            heads = [kt[h * HEAD_DIM:(h + 1) * HEAD_DIM, :] for h in range(N_KV_HEADS)]
            k_lo = [jnp.concatenate([kh, zero_k], axis=0) for kh in heads]
            k_hi = [jnp.concatenate([zero_k, kh], axis=0) for kh in heads]
            v_lo = (jnp.where(lo, v2, zero_v), jnp.where(lo, v2r, zero_v))
            v_hi = (jnp.where(lo, zero_v, v2r), jnp.where(lo, zero_v, v2))
            mask2 = jnp.concatenate([mask, mask], axis=0)
            block_cache[blk] = (mask2, k_lo, k_hi, v_lo, v_hi)
        return block_cache[blk]

    upper = lax.broadcasted_iota(jnp.int32, (2 * WINDOW, 1), 0) < WINDOW

    def attn_unit(blk, h):
        rows = slice(blk * WINDOW, (blk + 1) * WINDOW)
        mask2, k_lo, k_hi, v_lo, v_hi = block_operands(blk)
        slots = (2 * h, 2 * h + 1)
        q2 = jnp.concatenate([q_ref[rows, s * LANES:(s + 1) * LANES] for s in slots], axis=0)
        scores = [_dot(q2, kk) for kk in (k_lo[h], k_hi[h])]
        yield
        out = None
        for half, vv in enumerate((v_lo[h], v_hi[h])):
            sink = jnp.where(upper, sinks_ref[2 * slots[0] + half], sinks_ref[2 * slots[1] + half])
            s = jnp.where(mask2, scores[half], -jnp.inf)
            m = jnp.maximum(jnp.max(s, axis=-1, keepdims=True), sink)
            e = jnp.exp(s - m)
            denom = jnp.sum(e, axis=-1, keepdims=True) + jnp.exp(sink - m)
            p = (e * (1.0 / denom)).astype(BF16)
            o = _dot(p, vv)
            out = o if out is None else out + o
        out = out.astype(BF16)
        mix_ref[rows, slots[0] * LANES:(slots[0] + 1) * LANES] = out[:WINDOW]
        mix_ref[rows, slots[1] * LANES:(slots[1] + 1) * LANES] = out[WINDOW:]

    def gm_unit(blk0):
        pair = (blk0, blk0 + 1)
        rows = [slice(blk * WINDOW, (blk + 1) * WINDOW) for blk in pair]
        vgn = [_gm_norm(uvg_ref[r, GM_WIDTH:], lng_ref[...], lnb_ref[...]) for r in rows]
        if pair[1] == blocks - 1:
            gmv_ref[...] = vgn[1]
        vgb = [v.astype(BF16) for v in vgn]
        gu = [_gelu(uvg_ref[r, :GM_WIDTH]) for r in rows]
        yield
        for g in range(GM_GROUPS):
            cols = slice(g * GM_DIM, (g + 1) * GM_DIM)
            w = jnp.where(tril, ws_ref[g], 0.0).astype(BF16)
            sm = _dot(w, jnp.concatenate([v[:, cols] for v in vgb], axis=1)) + bst_ref[:, g:g + 1]
            for i, r in enumerate(rows):
                mix_ref[r, ATTN_WIDTH + g * GM_DIM:ATTN_WIDTH + (g + 1) * GM_DIM] = (
                    gu[i][:, cols] * sm[:, i * GM_DIM:(i + 1) * GM_DIM]).astype(BF16)

    assert blocks % 2 == 0
    units = []
    for blk0 in range(0, blocks, 2):
        units += [attn_unit(blk, h) for blk in (blk0, blk0 + 1) for h in range(N_KV_HEADS)]
        units.append(gm_unit(blk0))
    return units


KV_COL = ATTN_WIDTH
UVG_COL = ATTN_WIDTH + 2 * KV_WIDTH


N_NEXT = 6


def _in_mix_kernel(tiles_per_seq, blocks, n_tiles,
                   x_ref, g1_ref, w1c_ref, w3c_ref, w2c_ref, gm_ref, winc_ref, bin_ref,
                   sinks_ref, lng_ref, lnb_ref, ws_ref, bst_ref, xs_ref, wq8_ref, bq8_ref,
                   *rest):
    next_f32, rest = rest[:N_NEXT], rest[N_NEXT:]
    (x1_ref, mix_ref, kvw_ref, gmv_ref, x1s_ref, q8s_ref, kvs_ref, uvgs_ref), rest = (
        rest[:8], rest[8:])
    next_bf16, rest = rest[:N_NEXT], rest[N_NEXT:]
    w1_s, w3_s, w2_s, win_s, q_s, kv_s, uvg_s, kvp_s = rest
    s = pl.program_id(0)
    t = s - CAST_STEPS

    @pl.when(s < CAST_STEPS)
    def _():
        for src, dst in ((w1c_ref, w1_s), (w3c_ref, w3_s), (w2c_ref, w2_s), (winc_ref, win_s)):
            _cast_chunk(s, src, dst)

    @pl.when(t == 0)
    def _():
        q_s[...] = jnp.zeros_like(q_s)
        kv_s[...] = jnp.zeros_like(kv_s)
        uvg_s[...] = jnp.zeros_like(uvg_s)
        kvp_s[...] = jnp.zeros_like(kvp_s)

    def ffn(x, before_chunk=None):
        h = _rms(x, g1_ref[...]).astype(BF16)
        return x + 0.5 * _swiglu(h, w1_s, w3_s, w2_s, before_chunk)

    def project(h2, cols):
        return _dot(h2, win_s[:, cols]) + bin_ref[:, cols]

    @pl.when((t >= 0) & (t <= n_tiles))
    def _():
        first_of_seq = (jnp.maximum(t - 1, 0) % tiles_per_seq) == 0
        units = _mix_units(blocks, first_of_seq, sinks_ref, q_s, kv_s, kvp_s, uvg_s, lng_ref,
                           lnb_ref, ws_ref, bst_ref, mix_ref, gmv_ref)
        in_flight = []

        def before_chunk(c):
            for unit in in_flight:
                next(unit, None)
            first, last = c * len(units) // N_FF_CHUNKS, (c + 1) * len(units) // N_FF_CHUNKS
            in_flight[:] = units[first:last]
            for unit in in_flight:
                next(unit)

        for src, dst in zip(next_f32, next_bf16):
            dst[...] = src[...].astype(BF16)
        x1 = ffn(x_ref[...], before_chunk)
        for unit in in_flight:
            next(unit, None)
        kvp_s[...] = kv_s[(blocks - 1) * WINDOW:, :]
        x1_ref[...] = x1
        h2 = _rms(x1, gm_ref[...]).astype(BF16)
        q_s[...] = (project(h2, slice(0, KV_COL)) * ATTN_SCALE).astype(BF16)
        kv = project(h2, slice(KV_COL, UVG_COL))
        kv_s[...] = kv
        kvw_ref[...] = kv[(blocks - 1) * WINDOW:, :]
        uvg_s[...] = project(h2, slice(UVG_COL, None))

    @pl.when(t == n_tiles + 1)
    def _():
        x1 = ffn(xs_ref[...])
        x1s_ref[...] = x1
        h2 = _rms(x1, gm_ref[...]).astype(BF16)
        q8s_ref[...] = ((_dot(h2, wq8_ref[...]) + bq8_ref[...]) * ATTN_SCALE).astype(BF16)
        kvs_ref[...] = project(h2, slice(KV_COL, UVG_COL))
        uvgs_ref[...] = project(h2, slice(UVG_COL, None))


def _in_mix(x, xs, g1, w1, w3, w2, gmix, w_in, b_in, sinks, ln_g, ln_b, w_s, b_st, wq8, bq8,
            next_weights, tm):
    assert len(next_weights) == N_NEXT
    b, s, _ = x.shape
    n = xs.shape[0]
    nj = s // tm
    nt = b * nj
    blocks = tm // WINDOW

    def cur(step):
        u = jnp.clip(step - CAST_STEPS, 0, nt - 1)
        return u // nj, u % nj

    def prev(step):
        u = jnp.clip(step - CAST_STEPS - 1, 0, nt - 1)
        return u // nj, u % nj

    def tile(width, which):
        return pl.BlockSpec((None, tm, width), lambda step: which(step) + (0,))

    def per_seq(rows, width, which):
        return pl.BlockSpec((None, rows, width), lambda step: (which(step)[0], 0, 0))

    def whole(shape):
        return pl.BlockSpec(shape, lambda step: (0,) * len(shape))

    gconsts = (ln_g, ln_b, w_s, b_st)
    sample_out = ((n, D_MODEL), F32), ((n, N_Q_HEADS * LANES), BF16), ((n, 2 * KV_WIDTH), F32), (
        (n, 2 * GM_WIDTH), F32)
    return pl.pallas_call(
        functools.partial(_in_mix_kernel, nj, blocks, nt),
        out_shape=(jax.ShapeDtypeStruct((b, s, D_MODEL), F32),
                   jax.ShapeDtypeStruct((b, s, D_MODEL), BF16),
                   jax.ShapeDtypeStruct((b, WINDOW, 2 * KV_WIDTH), F32),
                   jax.ShapeDtypeStruct((b, CHUNK, GM_WIDTH), F32))
        + tuple(jax.ShapeDtypeStruct(shape, dtype) for shape, dtype in sample_out)
        + tuple(jax.ShapeDtypeStruct(w.shape, BF16) for w in next_weights),
        grid=(CAST_STEPS + nt + 2,),
        in_specs=[tile(D_MODEL, cur), _resident(g1.shape), _cast_spec(w1), _cast_spec(w3),
                  _cast_spec(w2), _resident(gmix.shape), _cast_spec(w_in), _resident(b_in.shape),
                  pl.BlockSpec(memory_space=pltpu.SMEM)]
        + [_resident(c.shape) for c in gconsts + (xs, wq8, bq8)]
        + [_trickle_spec(w, nt) for w in next_weights],
        out_specs=(tile(D_MODEL, cur), tile(D_MODEL, prev),
                   per_seq(WINDOW, 2 * KV_WIDTH, cur), per_seq(CHUNK, GM_WIDTH, prev))
        + tuple(whole(shape) for shape, _ in sample_out)
        + tuple(_trickle_spec(w, nt) for w in next_weights),
        scratch_shapes=[_bf16_scratch(w1), _bf16_scratch(w3), _bf16_scratch(w2),
                        _bf16_scratch(w_in),
                        pltpu.VMEM((tm, ATTN_WIDTH), BF16), pltpu.VMEM((tm, 2 * KV_WIDTH), F32),
                        pltpu.VMEM((tm, 2 * GM_WIDTH), F32),
                        pltpu.VMEM((WINDOW, 2 * KV_WIDTH), F32)],
        compiler_params=pltpu.CompilerParams(
            dimension_semantics=("arbitrary",), vmem_limit_bytes=VMEM_LIMIT_BYTES),
        name="in_mix",
    )(x, g1, w1, w3, w2, gmix, w_in, b_in, sinks, *gconsts, xs, wq8, bq8, *next_weights)


def _mix_sample_kernel(q8_ref, kn_ref, vn_ref, kc_ref, vc_ref, sinks_ref, uvg_ref, lng_ref,
                       lnb_ref, w00_ref, b0_ref, o8_ref, gm_ref, gmv_ref, kw_ref, vw_ref):
    q8 = q8_ref[...]
    kc = kc_ref[...]
    vc = vc_ref[...]
    kn = kn_ref[...]
    vn = vn_ref[...]
    sink = sinks_ref[...]
    s_c = jnp.einsum("bqd,bkd->bqk", q8, kc.astype(BF16), preferred_element_type=F32)
    s_n = jnp.sum(q8.astype(F32) * kn.astype(BF16).astype(F32), axis=-1, keepdims=True)
    m = jnp.maximum(jnp.maximum(jnp.max(s_c, axis=-1, keepdims=True), s_n), sink)
    e_c = jnp.exp(s_c - m)
    e_n = jnp.exp(s_n - m)
    inv = 1.0 / (jnp.sum(e_c, axis=-1, keepdims=True) + e_n + jnp.exp(sink - m))
    p_c = (e_c * inv).astype(BF16)
    p_n = (e_n * inv).astype(BF16).astype(F32)
    o = jnp.einsum("bqk,bkd->bqd", p_c, vc.astype(BF16), preferred_element_type=F32)
    o = o + p_n * vn.astype(BF16).astype(F32)
    row = lax.broadcasted_iota(jnp.int32, o.shape, 1)
    lane = lax.broadcasted_iota(jnp.int32, o.shape, 2)
    own = (lane < HEAD_DIM) == (row < N_Q_HEADS // N_KV_HEADS)
    o8_ref[...] = jnp.where(own, o, 0.0).astype(BF16)

    kw_ref[:, :WINDOW - 1, :] = kc[:, 1:, :]
    kw_ref[:, WINDOW - 1:, :] = kn
    vw_ref[:, :WINDOW - 1, :] = vc[:, 1:, :]
    vw_ref[:, WINDOW - 1:, :] = vn

    vgn = _gm_norm(uvg_ref[:, GM_WIDTH:], lng_ref[...], lnb_ref[...])
    gmv_ref[...] = vgn
    sm = w00_ref[...].astype(BF16).astype(F32) * vgn.astype(BF16).astype(F32) + b0_ref[...]
    gm_ref[...] = (_gelu(uvg_ref[:, :GM_WIDTH]) * sm).astype(BF16)


def _mix_sample(q8, kn, vn, kc, vc, sinks, uvg, ln_g, ln_b, w00, b0, bb):
    n = q8.shape[0]

    def b3(d1, d2):
        return pl.BlockSpec((bb, d1, d2), lambda i: (i, 0, 0))

    def b2(d):
        return pl.BlockSpec((bb, d), lambda i: (i, 0))

    return pl.pallas_call(
        _mix_sample_kernel,
        out_shape=(jax.ShapeDtypeStruct((n, N_Q_HEADS, LANES), BF16),
                   jax.ShapeDtypeStruct((n, GM_WIDTH), BF16),
                   jax.ShapeDtypeStruct((n, GM_WIDTH), F32),
                   jax.ShapeDtypeStruct((n, WINDOW, KV_WIDTH), F32),
                   jax.ShapeDtypeStruct((n, WINDOW, KV_WIDTH), F32)),
        grid=(n // bb,),
        in_specs=[b3(N_Q_HEADS, LANES), b3(1, KV_WIDTH), b3(1, KV_WIDTH),
                  b3(WINDOW, KV_WIDTH), b3(WINDOW, KV_WIDTH),
                  _resident(sinks.shape), b2(2 * GM_WIDTH), _resident(ln_g.shape),
                  _resident(ln_b.shape), _resident(w00.shape), _resident(b0.shape)],
        out_specs=(b3(N_Q_HEADS, LANES), b2(GM_WIDTH), b2(GM_WIDTH),
                   b3(WINDOW, KV_WIDTH), b3(WINDOW, KV_WIDTH)),
        compiler_params=pltpu.CompilerParams(
            dimension_semantics=("arbitrary",), vmem_limit_bytes=VMEM_LIMIT_BYTES),
        name="mix_sample",
    )(q8, kn, vn, kc, vc, sinks, uvg, ln_g, ln_b, w00, b0)


def _out_stage_kernel(n_tiles, x1_ref, mix_ref, p_ref, wo_s, bo_ref, g2_ref, w1_s, w3_s, w2_s,
                      gp_ref, wg_s, wp_s, gf_ref, x1s_ref, mixs_ref, ps_ref, wo8_ref,
                      y_ref, ys_ref):
    t = pl.program_id(0)

    def out_stage(x1, mix, p, wo_ref):
        x2 = x1 + _dot(mix, wo_ref[...]) + bo_ref[...]
        h = _rms(x2, g2_ref[...]).astype(BF16)
        x3 = x2 + 0.5 * _swiglu(h, w1_s, w3_s, w2_s)
        gate = _sigmoid(_dot(_rms(x3, gp_ref[...]).astype(BF16), wg_s[...]))
        x4 = x3 + gate * _dot(p.astype(BF16), wp_s[...])
        return _rms(x4, gf_ref[...])

    @pl.when(t < n_tiles)
    def _():
        y_ref[...] = out_stage(x1_ref[...], mix_ref[...], p_ref[...], wo_s)

    @pl.when(t == n_tiles)
    def _():
        ys_ref[...] = out_stage(x1s_ref[...], mixs_ref[...], ps_ref[...], wo8_ref)


def _out_stage(x1, mix, p, x1s, mixs, ps, wo, bo, g2, w1, w3, w2, gp, wg, wp, gf, wo8, tm):
    b, s, _ = x1.shape
    n = x1s.shape[0]
    nj = s // tm
    nt = b * nj

    def tile(width):
        def index(step):
            u = jnp.minimum(step, nt - 1)
            return u // nj, u % nj, 0
        return pl.BlockSpec((None, tm, width), index)

    consts = (wo, bo, g2, w1, w3, w2, gp, wg, wp, gf, x1s, mixs, ps, wo8)
    return pl.pallas_call(
        functools.partial(_out_stage_kernel, nt),
        out_shape=(jax.ShapeDtypeStruct((b, s, D_MODEL), F32),
                   jax.ShapeDtypeStruct((n, D_MODEL), F32)),
        grid=(nt + 1,),
        in_specs=[tile(D_MODEL), tile(D_MODEL), tile(PLE_DIM)]
        + [_resident(c.shape) for c in consts],
        out_specs=(tile(D_MODEL), pl.BlockSpec((n, D_MODEL), lambda step: (0, 0))),
        compiler_params=pltpu.CompilerParams(
            dimension_semantics=("arbitrary",), vmem_limit_bytes=VMEM_LIMIT_BYTES),
        name="out_stage",
    )(x1, mix, p, wo, bo, g2, w1, w3, w2, gp, wg, wp, gf, x1s, mixs, ps, wo8)


PROMPT_TILE = 512
SAMPLE_BLOCK = 32


def _pad_heads(w):
    lead = w.shape[:-1]
    wh = w.reshape(lead + (N_Q_HEADS, HEAD_DIM))
    z = jnp.zeros_like(wh[..., :N_Q_HEADS // 2, :])
    first = jnp.concatenate([wh[..., :N_Q_HEADS // 2, :], z], axis=-1)
    second = jnp.concatenate([z, wh[..., N_Q_HEADS // 2:, :]], axis=-1)
    return jnp.concatenate([first, second], axis=-2).reshape(lead + (N_Q_HEADS * LANES,))


def kernel(x_prompt, x_sample, cache_k_win, cache_v_win, p_prompt, p_sample, g_ffn1, w1_ffn1, w3_ffn1, w2_ffn1, g_mix, w_in, b_in, attn_sinks, gm_ln_g, gm_ln_b, gm_w_s, gm_b_s, w_out, b_out, g_ffn2, w1_ffn2, w3_ffn2, w2_ffn2, g_ple, w_ple_gate, w_ple_proj, g_final):
    depth = g_ffn1.shape[0]
    assert depth == 1
    i = 0
    nb, seq, _ = x_prompt.shape
    nd = x_sample.shape[0]

    row = lambda a: a.reshape(1, -1)
    g1, gmx, g2, gp, gf = row(g_ffn1[i]), row(g_mix[i]), row(g_ffn2[i]), row(g_ple[i]), row(g_final)
    ln_g, ln_b = row(gm_ln_g[i]), row(gm_ln_b[i])
    sinks = attn_sinks[i]
    wq8, bq8 = _pad_heads(w_in[i][:, :ATTN_WIDTH]).astype(BF16), row(_pad_heads(b_in[i][:ATTN_WIDTH]))
    wo8 = jnp.concatenate([_pad_heads(w_out[i][:ATTN_WIDTH].T).T, w_out[i][ATTN_WIDTH:]],
                          axis=0).astype(BF16)

    out_weights = (w_out[i], w1_ffn2[i], w3_ffn2[i], w2_ffn2[i], w_ple_gate[i], w_ple_proj[i])
    x1, mix, kv_last, gmv_p, x1s, q8, kvs, uvgs, wo, w1b, w3b, w2b, wg, wp = _in_mix(
        x_prompt, x_sample.reshape(nd, D_MODEL), g1, w1_ffn1[i], w3_ffn1[i], w2_ffn1[i], gmx,
        w_in[i], row(b_in[i]), sinks, ln_g, ln_b, gm_w_s[i], gm_b_s[i].T, wq8, bq8,
        out_weights, PROMPT_TILE)
    k_win_p = kv_last[:, :, :KV_WIDTH].reshape(1, nb, WINDOW, N_KV_HEADS, HEAD_DIM)
    v_win_p = kv_last[:, :, KV_WIDTH:].reshape(1, nb, WINDOW, N_KV_HEADS, HEAD_DIM)

    kvs = kvs.reshape(nd, 1, 2 * KV_WIDTH)
    o8, gm_s, gmv_s, kw_s, vw_s = _mix_sample(
        q8.reshape(nd, N_Q_HEADS, LANES), kvs[:, :, :KV_WIDTH], kvs[:, :, KV_WIDTH:],
        cache_k_win[i].reshape(nd, WINDOW, KV_WIDTH), cache_v_win[i].reshape(nd, WINDOW, KV_WIDTH),
        sinks.reshape(N_Q_HEADS, 1), uvgs, ln_g, ln_b,
        jnp.repeat(gm_w_s[i][:, 0, 0], GM_DIM).reshape(1, GM_WIDTH),
        jnp.repeat(gm_b_s[i][:, 0], GM_DIM).reshape(1, GM_WIDTH), SAMPLE_BLOCK)
    mix_s = jnp.concatenate([o8.reshape(nd, N_Q_HEADS * LANES), gm_s], axis=-1)

    y_prompt, y_sample = _out_stage(
        x1, mix, p_prompt[i], x1s, mix_s, p_sample[i].reshape(nd, PLE_DIM), wo, row(b_out[i]),
        g2, w1b, w3b, w2b, gp, wg, wp, gf, wo8, PROMPT_TILE)

    return (y_prompt, y_sample.reshape(nd, 1, D_MODEL), k_win_p, v_win_p,
            kw_s.reshape(1, nd, WINDOW, N_KV_HEADS, HEAD_DIM),
            vw_s.reshape(1, nd, WINDOW, N_KV_HEADS, HEAD_DIM),
            gmv_p.reshape(1, nb, CHUNK, GM_WIDTH), gmv_s.reshape(1, nd, 1, GM_WIDTH))
```

```python
import functools
import math

import jax
import jax.numpy as jnp
from jax import lax
from jax.experimental import pallas as pl
from jax.experimental.pallas import tpu as pltpu

F32 = jnp.float32
BF16 = jnp.bfloat16

D_MODEL = 1024
HEAD_DIM = 64
N_Q_HEADS = 8
N_KV_HEADS = 2
ATTN_WIDTH = N_Q_HEADS * HEAD_DIM
KV_WIDTH = N_KV_HEADS * HEAD_DIM
WINDOW = 128
GM_WIDTH = 512
GM_GROUPS = 4
GM_DIM = 128
CHUNK = 128
D_FF = 2816
PLE_DIM = 256
RMS_EPS = 1e-6
LN_EPS = 1e-5
ATTN_SCALE = HEAD_DIM ** -0.5

LANES = 128
FF_CHUNK = 512
FF_BOUNDS = tuple((lo, min(lo + FF_CHUNK, D_FF)) for lo in range(0, D_FF, FF_CHUNK))
N_FF_CHUNKS = len(FF_BOUNDS)
VMEM_LIMIT_BYTES = 62 * 1024 * 1024

_GELU_C = math.sqrt(2.0 / math.pi)


def _rms(x, g):
    return x * lax.rsqrt(jnp.mean(x * x, axis=-1, keepdims=True) + RMS_EPS) * g


def _sigmoid(x):
    return 1.0 / (1.0 + jnp.exp(-x))


def _gelu(x):
    return 0.5 * x * (1.0 + jnp.tanh(_GELU_C * (x + 0.044715 * (x * x * x))))


def _dot(a, b):
    return jnp.dot(a, b, preferred_element_type=F32)


def _dot_nt(a, b):
    return lax.dot_general(a, b, (((1,), (1,)), ((), ())), preferred_element_type=F32)


def _swiglu(h, w1_ref, w3_ref, w2_ref, before_chunk=None):
    acc = jnp.zeros((h.shape[0], D_MODEL), F32)
    for c, (lo, hi) in enumerate(FF_BOUNDS):
        if before_chunk is not None:
            before_chunk(c)
        cols = slice(lo, hi)
        a = _dot(h, w1_ref[:, cols])
        b = _dot(h, w3_ref[:, cols])
        act = (a * _sigmoid(a) * b).astype(BF16)
        acc = acc + _dot(act, w2_ref[cols, :])
    return acc


def _resident(shape):
    zeros = (0,) * len(shape)
    return pl.BlockSpec(shape, lambda *_: zeros, pipeline_mode=pl.Buffered(1))


CAST_STEPS = 4
BF16_SUBLANES = 16


CAST_BUFFERS = 1


def _cast_spec(w):
    rows = w.shape[0] // CAST_STEPS
    assert rows * CAST_STEPS == w.shape[0] and rows % BF16_SUBLANES == 0
    return pl.BlockSpec((rows, w.shape[1]), lambda s: (jnp.minimum(s, CAST_STEPS - 1), 0),
                        pipeline_mode=pl.Buffered(CAST_BUFFERS))


def _trickle_rows(w, max_chunks):
    rows = BF16_SUBLANES
    while w.shape[0] % rows or w.shape[0] // rows > max_chunks:
        rows += BF16_SUBLANES
    return rows


def _trickle_spec(w, max_chunks):
    rows = _trickle_rows(w, max_chunks)
    last = w.shape[0] // rows - 1
    return pl.BlockSpec((rows, w.shape[1]), lambda s: (jnp.clip(s - CAST_STEPS, 0, last), 0))


def _cast_chunk(s, src_ref, dst_ref):
    rows = src_ref.shape[0]
    start = pl.multiple_of(s * rows, BF16_SUBLANES)
    dst_ref[pl.ds(start, rows), :] = src_ref[...].astype(BF16)


def _bf16_scratch(w):
    return pltpu.VMEM(w.shape, BF16)


def _gm_norm(vg, ln_g, ln_b):
    a = _gelu(vg)
    mu = jnp.mean(a, axis=-1, keepdims=True)
    ac = a - mu
    return ac * lax.rsqrt(jnp.mean(ac * ac, axis=-1, keepdims=True) + LN_EPS) * ln_g + ln_b


def _mix_units(blocks, first_of_seq, sinks_ref, q_ref, kv_ref, kvp_ref, uvg_ref, lng_ref,
               lnb_ref, ws_ref, bst_ref, mix_ref, gmv_ref):
    lane = lax.broadcasted_iota(jnp.int32, (2 * WINDOW, LANES), 1)
    lo = lane < HEAD_DIM
    qi = lax.broadcasted_iota(jnp.int32, (WINDOW, 2 * WINDOW), 0)
    kj = lax.broadcasted_iota(jnp.int32, (WINDOW, 2 * WINDOW), 1)
    band = (kj >= qi) & (kj <= qi + WINDOW)
    ti = lax.broadcasted_iota(jnp.int32, (CHUNK, CHUNK), 0)
    si = lax.broadcasted_iota(jnp.int32, (CHUNK, CHUNK), 1)
    tril = si <= ti

    block_cache = {}

    def block_operands(blk):
        if blk not in block_cache:
            if blk == 0:
                kv2 = jnp.concatenate([kvp_ref[...], kv_ref[:WINDOW, :]], axis=0)
                mask = band & (kj >= jnp.where(first_of_seq, WINDOW, 0))
            else:
                kv2 = kv_ref[(blk - 1) * WINDOW:(blk + 1) * WINDOW, :]
                mask = band
            kt = kv2[:, :KV_WIDTH].T.astype(BF16)
            v2 = kv2[:, KV_WIDTH:].astype(BF16)
            v2r = pltpu.roll(v2, HEAD_DIM, 1)
            zero_v = jnp.zeros_like(v2)
            zero_k = jnp.zeros((HEAD_DIM, 2 * WINDOW), BF16)
            heads = [kt[h * HEAD_DIM:(h + 1) * HEAD_DIM, :] for h in range(N_KV_HEADS)]
            k_lo = [jnp.concatenate([kh, zero_k], axis=0) for kh in heads]
            k_hi = [jnp.concatenate([zero_k, kh], axis=0) for kh in heads]
            v_lo = (jnp.where(lo, v2, zero_v), jnp.where(lo, v2r, zero_v))
            v_hi = (jnp.where(lo, zero_v, v2r), jnp.where(lo, zero_v, v2))
            mask2 = jnp.concatenate([mask, mask], axis=0)
            block_cache[blk] = (mask2, k_lo, k_hi, v_lo, v_hi)
        return block_cache[blk]

    upper = lax.broadcasted_iota(jnp.int32, (2 * WINDOW, 1), 0) < WINDOW

    def attn_unit(blk, h):
        rows = slice(blk * WINDOW, (blk + 1) * WINDOW)
        mask2, k_lo, k_hi, v_lo, v_hi = block_operands(blk)
        slots = (2 * h, 2 * h + 1)
        q2 = jnp.concatenate([q_ref[rows, s * LANES:(s + 1) * LANES] for s in slots], axis=0)
        scores = [_dot(q2, kk) for kk in (k_lo[h], k_hi[h])]
        yield
        out = None
        for half, vv in enumerate((v_lo[h], v_hi[h])):
            sink = jnp.where(upper, sinks_ref[2 * slots[0] + half], sinks_ref[2 * slots[1] + half])
            s = jnp.where(mask2, scores[half], -jnp.inf)
            m = jnp.maximum(jnp.max(s, axis=-1, keepdims=True), sink)
            e = jnp.exp(s - m)
            denom = jnp.sum(e, axis=-1, keepdims=True) + jnp.exp(sink - m)
            p = (e * (1.0 / denom)).astype(BF16)
            o = _dot(p, vv)
            out = o if out is None else out + o
        out = out.astype(BF16)
        mix_ref[rows, slots[0] * LANES:(slots[0] + 1) * LANES] = out[:WINDOW]
        mix_ref[rows, slots[1] * LANES:(slots[1] + 1) * LANES] = out[WINDOW:]

    def gm_unit(blk0):
        pair = (blk0, blk0 + 1)
        rows = [slice(blk * WINDOW, (blk + 1) * WINDOW) for blk in pair]
        vgn = [_gm_norm(uvg_ref[r, GM_WIDTH:], lng_ref[...], lnb_ref[...]) for r in rows]
        if pair[1] == blocks - 1:
            gmv_ref[...] = vgn[1]
        vgb = [v.astype(BF16) for v in vgn]
        gu = [_gelu(uvg_ref[r, :GM_WIDTH]) for r in rows]
        yield
        for g in range(GM_GROUPS):
            cols = slice(g * GM_DIM, (g + 1) * GM_DIM)
            w = jnp.where(tril, ws_ref[g], 0.0).astype(BF16)
            sm = _dot(w, jnp.concatenate([v[:, cols] for v in vgb], axis=1)) + bst_ref[:, g:g + 1]
            for i, r in enumerate(rows):
                mix_ref[r, ATTN_WIDTH + g * GM_DIM:ATTN_WIDTH + (g + 1) * GM_DIM] = (
                    gu[i][:, cols] * sm[:, i * GM_DIM:(i + 1) * GM_DIM]).astype(BF16)

    assert blocks % 2 == 0
    units = []
    for blk0 in range(0, blocks, 2):
        units += [attn_unit(blk, h) for blk in (blk0, blk0 + 1) for h in range(N_KV_HEADS)]
        units.append(gm_unit(blk0))
    return units


KV_COL = ATTN_WIDTH
UVG_COL = ATTN_WIDTH + 2 * KV_WIDTH


N_NEXT = 6


def _in_mix_kernel(tiles_per_seq, blocks, n_tiles,
                   x_ref, g1_ref, w1c_ref, w3c_ref, w2c_ref, gm_ref, winc_ref, bin_ref,
                   sinks_ref, lng_ref, lnb_ref, ws_ref, bst_ref, xs_ref, wq8_ref, bq8_ref,
                   *rest):
    next_f32, rest = rest[:N_NEXT], rest[N_NEXT:]
    (x1_ref, mix_ref, kvw_ref, gmv_ref, x1s_ref, q8s_ref, kvs_ref, uvgs_ref), rest = (
        rest[:8], rest[8:])
    next_bf16, rest = rest[:N_NEXT], rest[N_NEXT:]
    w1_s, w3_s, w2_s, win_s, q_s, kv_s, uvg_s, kvp_s = rest
    s = pl.program_id(0)
    t = s - CAST_STEPS

    @pl.when(s < CAST_STEPS)
    def _():
        for src, dst in ((w1c_ref, w1_s), (w3c_ref, w3_s), (w2c_ref, w2_s), (winc_ref, win_s)):
            _cast_chunk(s, src, dst)

    @pl.when(t == 0)
    def _():
        q_s[...] = jnp.zeros_like(q_s)
        kv_s[...] = jnp.zeros_like(kv_s)
        uvg_s[...] = jnp.zeros_like(uvg_s)
        kvp_s[...] = jnp.zeros_like(kvp_s)

    def ffn(x, before_chunk=None):
        h = _rms(x, g1_ref[...]).astype(BF16)
        return x + 0.5 * _swiglu(h, w1_s, w3_s, w2_s, before_chunk)

    def project(h2, cols):
        return _dot(h2, win_s[:, cols]) + bin_ref[:, cols]

    def ffn_with_previous_mix(x):
        first_of_seq = (jnp.maximum(t - 1, 0) & (tiles_per_seq - 1)) == 0
        units = _mix_units(blocks, first_of_seq, sinks_ref, q_s, kv_s, kvp_s, uvg_s, lng_ref,
                           lnb_ref, ws_ref, bst_ref, mix_ref, gmv_ref)
        in_flight = []

        def before_chunk(c):
            for unit in in_flight:
                next(unit, None)
            first, last = c * len(units) // N_FF_CHUNKS, (c + 1) * len(units) // N_FF_CHUNKS
            in_flight[:] = units[first:last]
            for unit in in_flight:
                next(unit)

        x1 = ffn(x, before_chunk)
        for unit in in_flight:
            next(unit, None)
        return x1

    @pl.when((t >= 0) & (t < n_tiles))
    def _():
        for src, dst in zip(next_f32, next_bf16):
            dst[...] = src[...].astype(BF16)
        x1 = ffn_with_previous_mix(x_ref[...])
        kvp_s[...] = kv_s[(blocks - 1) * WINDOW:, :]
        x1_ref[...] = x1
        h2 = _rms(x1, gm_ref[...]).astype(BF16)
        q_s[...] = (project(h2, slice(0, KV_COL)) * ATTN_SCALE).astype(BF16)
        kv = project(h2, slice(KV_COL, UVG_COL))
        kv_s[...] = kv
        kvw_ref[...] = kv[(blocks - 1) * WINDOW:, :]
        uvg_s[...] = project(h2, slice(UVG_COL, None))

    @pl.when(t == n_tiles)
    def _():
        x1 = ffn_with_previous_mix(xs_ref[...])
        x1s_ref[...] = x1
        h2 = _rms(x1, gm_ref[...]).astype(BF16)
        q8s_ref[...] = ((_dot(h2, wq8_ref[...]) + bq8_ref[...]) * ATTN_SCALE).astype(BF16)
        kvs_ref[...] = project(h2, slice(KV_COL, UVG_COL))
        uvgs_ref[...] = project(h2, slice(UVG_COL, None))


def _in_mix(x, xs, g1, w1, w3, w2, gmix, w_in, b_in, sinks, ln_g, ln_b, w_s, b_st, wq8, bq8,
            next_weights, tm):
    assert len(next_weights) == N_NEXT
    b, s, _ = x.shape
    n = xs.shape[0]
    nj = s // tm
    nt = b * nj
    blocks = tm // WINDOW
    seq_shift = nj.bit_length() - 1
    assert nj == 1 << seq_shift
    x = x.reshape(b * s, D_MODEL)

    def cur(step):
        return jnp.clip(step - CAST_STEPS, 0, nt - 1)

    def prev(step):
        return jnp.clip(step - CAST_STEPS - 1, 0, nt - 1)

    def tile(width, which):
        return pl.BlockSpec((tm, width), lambda step: (which(step), 0))

    def per_seq(rows, width, which):
        return pl.BlockSpec((None, rows, width),
                            lambda step: (lax.shift_right_logical(which(step), seq_shift), 0, 0))

    def whole(shape):
        return pl.BlockSpec(shape, lambda step: (0,) * len(shape))

    gconsts = (ln_g, ln_b, w_s, b_st)
    sample_out = ((n, D_MODEL), F32), ((n, N_Q_HEADS * LANES), BF16), ((n, 2 * KV_WIDTH), F32), (
        (n, 2 * GM_WIDTH), F32)
    return pl.pallas_call(
        functools.partial(_in_mix_kernel, nj, blocks, nt),
        out_shape=(jax.ShapeDtypeStruct((b * s, D_MODEL), F32),
                   jax.ShapeDtypeStruct((b * s, D_MODEL), BF16),
                   jax.ShapeDtypeStruct((b, WINDOW, 2 * KV_WIDTH), F32),
                   jax.ShapeDtypeStruct((b, CHUNK, GM_WIDTH), F32))
        + tuple(jax.ShapeDtypeStruct(shape, dtype) for shape, dtype in sample_out)
        + tuple(jax.ShapeDtypeStruct(w.shape, BF16) for w in next_weights),
        grid=(CAST_STEPS + nt + 1,),
        in_specs=[tile(D_MODEL, cur), _resident(g1.shape), _cast_spec(w1), _cast_spec(w3),
                  _cast_spec(w2), _resident(gmix.shape), _cast_spec(w_in), _resident(b_in.shape),
                  pl.BlockSpec(memory_space=pltpu.SMEM)]
        + [_resident(c.shape) for c in gconsts + (xs, wq8, bq8)]
        + [_trickle_spec(w, nt) for w in next_weights],
        out_specs=(tile(D_MODEL, cur), tile(D_MODEL, prev),
                   per_seq(WINDOW, 2 * KV_WIDTH, cur), per_seq(CHUNK, GM_WIDTH, prev))
        + tuple(whole(shape) for shape, _ in sample_out)
        + tuple(_trickle_spec(w, nt) for w in next_weights),
        scratch_shapes=[_bf16_scratch(w1), _bf16_scratch(w3), _bf16_scratch(w2),
                        _bf16_scratch(w_in),
                        pltpu.VMEM((tm, ATTN_WIDTH), BF16), pltpu.VMEM((tm, 2 * KV_WIDTH), F32),
                        pltpu.VMEM((tm, 2 * GM_WIDTH), F32),
                        pltpu.VMEM((WINDOW, 2 * KV_WIDTH), F32)],
        compiler_params=pltpu.CompilerParams(
            dimension_semantics=("arbitrary",), vmem_limit_bytes=VMEM_LIMIT_BYTES),
        name="in_mix",
    )(x, g1, w1, w3, w2, gmix, w_in, b_in, sinks, *gconsts, xs, wq8, bq8, *next_weights)


def _mix_sample_kernel(q8_ref, kn_ref, vn_ref, kc_ref, vc_ref, sinks_ref, uvg_ref, lng_ref,
                       lnb_ref, w00_ref, b0_ref, o8_ref, gm_ref, gmv_ref, kw_ref, vw_ref):
    q8 = q8_ref[...]
    kc = kc_ref[...]
    vc = vc_ref[...]
    kn = kn_ref[...]
    vn = vn_ref[...]
    sink = sinks_ref[...]
    s_c = jnp.einsum("bqd,bkd->bqk", q8, kc.astype(BF16), preferred_element_type=F32)
    s_n = jnp.sum(q8.astype(F32) * kn.astype(BF16).astype(F32), axis=-1, keepdims=True)
    m = jnp.maximum(jnp.maximum(jnp.max(s_c, axis=-1, keepdims=True), s_n), sink)
    e_c = jnp.exp(s_c - m)
    e_n = jnp.exp(s_n - m)
    inv = 1.0 / (jnp.sum(e_c, axis=-1, keepdims=True) + e_n + jnp.exp(sink - m))
    p_c = (e_c * inv).astype(BF16)
    p_n = (e_n * inv).astype(BF16).astype(F32)
    o = jnp.einsum("bqk,bkd->bqd", p_c, vc.astype(BF16), preferred_element_type=F32)
    o = o + p_n * vn.astype(BF16).astype(F32)
    row = lax.broadcasted_iota(jnp.int32, o.shape, 1)
    lane = lax.broadcasted_iota(jnp.int32, o.shape, 2)
    own = (lane < HEAD_DIM) == (row < N_Q_HEADS // N_KV_HEADS)
    o8_ref[...] = jnp.where(own, o, 0.0).astype(BF16)

    kw_ref[:, :WINDOW - 1, :] = kc[:, 1:, :]
    kw_ref[:, WINDOW - 1:, :] = kn
    vw_ref[:, :WINDOW - 1, :] = vc[:, 1:, :]
    vw_ref[:, WINDOW - 1:, :] = vn

    vgn = _gm_norm(uvg_ref[:, GM_WIDTH:], lng_ref[...], lnb_ref[...])
    gmv_ref[...] = vgn
    sm = w00_ref[...].astype(BF16).astype(F32) * vgn.astype(BF16).astype(F32) + b0_ref[...]
    gm_ref[...] = (_gelu(uvg_ref[:, :GM_WIDTH]) * sm).astype(BF16)


def _mix_sample(q8, kn, vn, kc, vc, sinks, uvg, ln_g, ln_b, w00, b0, bb):
    n = q8.shape[0]

    def b3(d1, d2):
        return pl.BlockSpec((bb, d1, d2), lambda i: (i, 0, 0))

    def b2(d):
        return pl.BlockSpec((bb, d), lambda i: (i, 0))

    return pl.pallas_call(
        _mix_sample_kernel,
        out_shape=(jax.ShapeDtypeStruct((n, N_Q_HEADS, LANES), BF16),
                   jax.ShapeDtypeStruct((n, GM_WIDTH), BF16),
                   jax.ShapeDtypeStruct((n, GM_WIDTH), F32),
                   jax.ShapeDtypeStruct((n, WINDOW, KV_WIDTH), F32),
                   jax.ShapeDtypeStruct((n, WINDOW, KV_WIDTH), F32)),
        grid=(n // bb,),
        in_specs=[b3(N_Q_HEADS, LANES), b3(1, KV_WIDTH), b3(1, KV_WIDTH),
                  b3(WINDOW, KV_WIDTH), b3(WINDOW, KV_WIDTH),
                  _resident(sinks.shape), b2(2 * GM_WIDTH), _resident(ln_g.shape),
                  _resident(ln_b.shape), _resident(w00.shape), _resident(b0.shape)],
        out_specs=(b3(N_Q_HEADS, LANES), b2(GM_WIDTH), b2(GM_WIDTH),
                   b3(WINDOW, KV_WIDTH), b3(WINDOW, KV_WIDTH)),
        compiler_params=pltpu.CompilerParams(
            dimension_semantics=("arbitrary",), vmem_limit_bytes=VMEM_LIMIT_BYTES),
        name="mix_sample",
    )(q8, kn, vn, kc, vc, sinks, uvg, ln_g, ln_b, w00, b0)


def _out_stage_kernel(n_tiles, x1_ref, mix_ref, p_ref, wo_s, bo_ref, g2_ref, w1_s, w3_s, w2_s,
                      gp_ref, wg_s, wp_s, gf_ref, x1s_ref, mixs_ref, ps_ref, wo8_ref,
                      y_ref, ys_ref):
    t = pl.program_id(0)

    def out_stage(x1, mix, p, wo_ref):
        x2 = x1 + _dot(mix, wo_ref[...]) + bo_ref[...]
        h = _rms(x2, g2_ref[...]).astype(BF16)
        x3 = x2 + 0.5 * _swiglu(h, w1_s, w3_s, w2_s)
        gate = _sigmoid(_dot(_rms(x3, gp_ref[...]).astype(BF16), wg_s[...]))
        x4 = x3 + gate * _dot(p.astype(BF16), wp_s[...])
        return _rms(x4, gf_ref[...])

    @pl.when(t < n_tiles)
    def _():
        y_ref[...] = out_stage(x1_ref[...], mix_ref[...], p_ref[...], wo_s)

    @pl.when(t == n_tiles)
    def _():
        ys_ref[...] = out_stage(x1s_ref[...], mixs_ref[...], ps_ref[...], wo8_ref)


def _out_stage(x1, mix, p, x1s, mixs, ps, wo, bo, g2, w1, w3, w2, gp, wg, wp, gf, wo8, tm):
    r = x1.shape[0]
    n = x1s.shape[0]
    nt = r // tm

    def tile(width):
        return pl.BlockSpec((tm, width), lambda step: (jnp.minimum(step, nt - 1), 0))

    consts = (wo, bo, g2, w1, w3, w2, gp, wg, wp, gf, x1s, mixs, ps, wo8)
    return pl.pallas_call(
        functools.partial(_out_stage_kernel, nt),
        out_shape=(jax.ShapeDtypeStruct((r, D_MODEL), F32),
                   jax.ShapeDtypeStruct((n, D_MODEL), F32)),
        grid=(nt + 1,),
        in_specs=[tile(D_MODEL), tile(D_MODEL), tile(PLE_DIM)]
        + [_resident(c.shape) for c in consts],
        out_specs=(tile(D_MODEL), pl.BlockSpec((n, D_MODEL), lambda step: (0, 0))),
        compiler_params=pltpu.CompilerParams(
            dimension_semantics=("arbitrary",), vmem_limit_bytes=VMEM_LIMIT_BYTES),
        name="out_stage",
    )(x1, mix, p, wo, bo, g2, w1, w3, w2, gp, wg, wp, gf, x1s, mixs, ps, wo8)


PROMPT_TILE = 512
SAMPLE_BLOCK = 32


def _pad_heads(w):
    lead = w.shape[:-1]
    wh = w.reshape(lead + (N_Q_HEADS, HEAD_DIM))
    z = jnp.zeros_like(wh[..., :N_Q_HEADS // 2, :])
    first = jnp.concatenate([wh[..., :N_Q_HEADS // 2, :], z], axis=-1)
    second = jnp.concatenate([z, wh[..., N_Q_HEADS // 2:, :]], axis=-1)
    return jnp.concatenate([first, second], axis=-2).reshape(lead + (N_Q_HEADS * LANES,))


def kernel(x_prompt, x_sample, cache_k_win, cache_v_win, p_prompt, p_sample, g_ffn1, w1_ffn1, w3_ffn1, w2_ffn1, g_mix, w_in, b_in, attn_sinks, gm_ln_g, gm_ln_b, gm_w_s, gm_b_s, w_out, b_out, g_ffn2, w1_ffn2, w3_ffn2, w2_ffn2, g_ple, w_ple_gate, w_ple_proj, g_final):
    depth = g_ffn1.shape[0]
    assert depth == 1
    i = 0
    nb, seq, _ = x_prompt.shape
    nd = x_sample.shape[0]

    row = lambda a: a.reshape(1, -1)
    g1, gmx, g2, gp, gf = row(g_ffn1[i]), row(g_mix[i]), row(g_ffn2[i]), row(g_ple[i]), row(g_final)
    ln_g, ln_b = row(gm_ln_g[i]), row(gm_ln_b[i])
    sinks = attn_sinks[i]
    wq8, bq8 = _pad_heads(w_in[i][:, :ATTN_WIDTH]).astype(BF16), row(_pad_heads(b_in[i][:ATTN_WIDTH]))
    wo8 = jnp.concatenate([_pad_heads(w_out[i][:ATTN_WIDTH].T).T, w_out[i][ATTN_WIDTH:]],
                          axis=0).astype(BF16)

    out_weights = (w_out[i], w1_ffn2[i], w3_ffn2[i], w2_ffn2[i], w_ple_gate[i], w_ple_proj[i])
    x1, mix, kv_last, gmv_p, x1s, q8, kvs, uvgs, wo, w1b, w3b, w2b, wg, wp = _in_mix(
        x_prompt, x_sample.reshape(nd, D_MODEL), g1, w1_ffn1[i], w3_ffn1[i], w2_ffn1[i], gmx,
        w_in[i], row(b_in[i]), sinks, ln_g, ln_b, gm_w_s[i], gm_b_s[i].T, wq8, bq8,
        out_weights, PROMPT_TILE)
    k_win_p = kv_last[:, :, :KV_WIDTH].reshape(1, nb, WINDOW, N_KV_HEADS, HEAD_DIM)
    v_win_p = kv_last[:, :, KV_WIDTH:].reshape(1, nb, WINDOW, N_KV_HEADS, HEAD_DIM)

    kvs = kvs.reshape(nd, 1, 2 * KV_WIDTH)
    o8, gm_s, gmv_s, kw_s, vw_s = _mix_sample(
        q8.reshape(nd, N_Q_HEADS, LANES), kvs[:, :, :KV_WIDTH], kvs[:, :, KV_WIDTH:],
        cache_k_win[i].reshape(nd, WINDOW, KV_WIDTH), cache_v_win[i].reshape(nd, WINDOW, KV_WIDTH),
        sinks.reshape(N_Q_HEADS, 1), uvgs, ln_g, ln_b,
        jnp.repeat(gm_w_s[i][:, 0, 0], GM_DIM).reshape(1, GM_WIDTH),
        jnp.repeat(gm_b_s[i][:, 0], GM_DIM).reshape(1, GM_WIDTH), SAMPLE_BLOCK)
    mix_s = jnp.concatenate([o8.reshape(nd, N_Q_HEADS * LANES), gm_s], axis=-1)

    y_prompt, y_sample = _out_stage(
        x1, mix, p_prompt[i].reshape(nb * seq, PLE_DIM), x1s, mix_s,
        p_sample[i].reshape(nd, PLE_DIM), wo, row(b_out[i]),
        g2, w1b, w3b, w2b, gp, wg, wp, gf, wo8, 2 * PROMPT_TILE)

    return (y_prompt.reshape(nb, seq, D_MODEL), y_sample.reshape(nd, 1, D_MODEL), k_win_p, v_win_p,
            kw_s.reshape(1, nd, WINDOW, N_KV_HEADS, HEAD_DIM),
            vw_s.reshape(1, nd, WINDOW, N_KV_HEADS, HEAD_DIM),
            gmv_p.reshape(1, nb, CHUNK, GM_WIDTH), gmv_s.reshape(1, nd, 1, GM_WIDTH))
```

```python
import functools
import math

import jax
import jax.numpy as jnp
from jax import lax
from jax.experimental import pallas as pl
from jax.experimental.pallas import tpu as pltpu

F32 = jnp.float32
BF16 = jnp.bfloat16

D_MODEL = 1024
HEAD_DIM = 64
N_Q_HEADS = 8
N_KV_HEADS = 2
ATTN_WIDTH = N_Q_HEADS * HEAD_DIM
KV_WIDTH = N_KV_HEADS * HEAD_DIM
WINDOW = 128
GM_WIDTH = 512
GM_GROUPS = 4
GM_DIM = 128
CHUNK = 128
D_FF = 2816
PLE_DIM = 256
RMS_EPS = 1e-6
LN_EPS = 1e-5
ATTN_SCALE = HEAD_DIM ** -0.5

LANES = 128
FF_CHUNK = 512
FF_BOUNDS = tuple((lo, min(lo + FF_CHUNK, D_FF)) for lo in range(0, D_FF, FF_CHUNK))
N_FF_CHUNKS = len(FF_BOUNDS)
VMEM_LIMIT_BYTES = 62 * 1024 * 1024

_GELU_C = math.sqrt(2.0 / math.pi)


def _rms(x, g):
    return x * lax.rsqrt(jnp.mean(x * x, axis=-1, keepdims=True) + RMS_EPS) * g


def _sigmoid(x):
    return 1.0 / (1.0 + jnp.exp(-x))


def _gelu(x):
    return 0.5 * x * (1.0 + jnp.tanh(_GELU_C * (x + 0.044715 * (x * x * x))))


def _dot(a, b):
    return jnp.dot(a, b, preferred_element_type=F32)


def _dot_nt(a, b):
    return lax.dot_general(a, b, (((1,), (1,)), ((), ())), preferred_element_type=F32)


def _swiglu(h, w1_ref, w3_ref, w2_ref, before_chunk=None):
    acc = jnp.zeros((h.shape[0], D_MODEL), F32)
    for c, (lo, hi) in enumerate(FF_BOUNDS):
        if before_chunk is not None:
            before_chunk(c)
        cols = slice(lo, hi)
        a = _dot(h, w1_ref[:, cols])
        b = _dot(h, w3_ref[:, cols])
        act = (a * _sigmoid(a) * b).astype(BF16)
        acc = acc + _dot(act, w2_ref[cols, :])
    return acc


def _resident(shape):
    zeros = (0,) * len(shape)
    return pl.BlockSpec(shape, lambda *_: zeros, pipeline_mode=pl.Buffered(1))


CAST_STEPS = 4
BF16_SUBLANES = 16


CAST_BUFFERS = 1


def _cast_spec(w):
    rows = w.shape[0] // CAST_STEPS
    assert rows * CAST_STEPS == w.shape[0] and rows % BF16_SUBLANES == 0
    return pl.BlockSpec((rows, w.shape[1]), lambda s: (jnp.minimum(s, CAST_STEPS - 1), 0),
                        pipeline_mode=pl.Buffered(CAST_BUFFERS))


def _trickle_rows(w, max_chunks):
    rows = BF16_SUBLANES
    while w.shape[0] % rows or w.shape[0] // rows > max_chunks:
        rows += BF16_SUBLANES
    return rows


def _trickle_spec(w, max_chunks):
    rows = _trickle_rows(w, max_chunks)
    last = w.shape[0] // rows - 1
    return pl.BlockSpec((rows, w.shape[1]), lambda s: (jnp.clip(s - CAST_STEPS, 0, last), 0))


def _cast_chunk(s, src_ref, dst_ref):
    rows = src_ref.shape[0]
    start = pl.multiple_of(s * rows, BF16_SUBLANES)
    dst_ref[pl.ds(start, rows), :] = src_ref[...].astype(BF16)


def _bf16_scratch(w):
    return pltpu.VMEM(w.shape, BF16)


def _gm_norm(vg, ln_g, ln_b):
    a = _gelu(vg)
    mu = jnp.mean(a, axis=-1, keepdims=True)
    ac = a - mu
    return ac * lax.rsqrt(jnp.mean(ac * ac, axis=-1, keepdims=True) + LN_EPS) * ln_g + ln_b


def _mix_units(blocks, first_of_seq, sinks_ref, q_ref, kv_ref, kvp_ref, uvg_ref, lng_ref,
               lnb_ref, ws_ref, bst_ref, mix_ref, gmv_ref):
    lane = lax.broadcasted_iota(jnp.int32, (2 * WINDOW, LANES), 1)
    lo = lane < HEAD_DIM
    qi = lax.broadcasted_iota(jnp.int32, (WINDOW, 2 * WINDOW), 0)
    kj = lax.broadcasted_iota(jnp.int32, (WINDOW, 2 * WINDOW), 1)
    band = (kj >= qi) & (kj <= qi + WINDOW)
    ti = lax.broadcasted_iota(jnp.int32, (CHUNK, CHUNK), 0)
    si = lax.broadcasted_iota(jnp.int32, (CHUNK, CHUNK), 1)
    tril = si <= ti

    block_cache = {}

    def block_operands(blk):
        if blk not in block_cache:
            if blk == 0:
                kv2 = jnp.concatenate([kvp_ref[...], kv_ref[:WINDOW, :]], axis=0)
                mask = band & (kj >= jnp.where(first_of_seq, WINDOW, 0))
            else:
                kv2 = kv_ref[(blk - 1) * WINDOW:(blk + 1) * WINDOW, :]
                mask = band
            kt = kv2[:, :KV_WIDTH].T.astype(BF16)
            v2 = kv2[:, KV_WIDTH:].astype(BF16)
            v2r = pltpu.roll(v2, HEAD_DIM, 1)
            zero_v = jnp.zeros_like(v2)
            zero_k = jnp.zeros((HEAD_DIM, 2 * WINDOW), BF16)
            heads = [kt[h * HEAD_DIM:(h + 1) * HEAD_DIM, :] for h in range(N_KV_HEADS)]
            k_lo = [jnp.concatenate([kh, zero_k], axis=0) for kh in heads]
            k_hi = [jnp.concatenate([zero_k, kh], axis=0) for kh in heads]
            v_lo = (jnp.where(lo, v2, zero_v), jnp.where(lo, v2r, zero_v))
            v_hi = (jnp.where(lo, zero_v, v2r), jnp.where(lo, zero_v, v2))
            mask2 = jnp.concatenate([mask, mask], axis=0)
            block_cache[blk] = (mask2, k_lo, k_hi, v_lo, v_hi)
        return block_cache[blk]

    upper = lax.broadcasted_iota(jnp.int32, (2 * WINDOW, 1), 0) < WINDOW

    def attn_unit(blk, h):
        rows = slice(blk * WINDOW, (blk + 1) * WINDOW)
        mask2, k_lo, k_hi, v_lo, v_hi = block_operands(blk)
        slots = (2 * h, 2 * h + 1)
        q2 = jnp.concatenate([q_ref[rows, s * LANES:(s + 1) * LANES] for s in slots], axis=0)
        scores = [_dot(q2, kk) for kk in (k_lo[h], k_hi[h])]
        yield
        out = None
        for half, vv in enumerate((v_lo[h], v_hi[h])):
            sink = jnp.where(upper, sinks_ref[2 * slots[0] + half], sinks_ref[2 * slots[1] + half])
            s = jnp.where(mask2, scores[half], -jnp.inf)
            m = jnp.maximum(jnp.max(s, axis=-1, keepdims=True), sink)
            e = jnp.exp(s - m)
            denom = jnp.sum(e, axis=-1, keepdims=True) + jnp.exp(sink - m)
            p = (e * (1.0 / denom)).astype(BF16)
            o = _dot(p, vv)
            out = o if out is None else out + o
        out = out.astype(BF16)
        mix_ref[rows, slots[0] * LANES:(slots[0] + 1) * LANES] = out[:WINDOW]
        mix_ref[rows, slots[1] * LANES:(slots[1] + 1) * LANES] = out[WINDOW:]

    def gm_unit(blk0):
        pair = (blk0, blk0 + 1)
        rows = [slice(blk * WINDOW, (blk + 1) * WINDOW) for blk in pair]
        vgn = [_gm_norm(uvg_ref[r, GM_WIDTH:], lng_ref[...], lnb_ref[...]) for r in rows]
        if pair[1] == blocks - 1:
            gmv_ref[...] = vgn[1]
        vgb = [v.astype(BF16) for v in vgn]
        gu = [_gelu(uvg_ref[r, :GM_WIDTH]) for r in rows]
        yield
        for g in range(GM_GROUPS):
            cols = slice(g * GM_DIM, (g + 1) * GM_DIM)
            w = jnp.where(tril, ws_ref[g], 0.0).astype(BF16)
            sm = _dot(w, jnp.concatenate([v[:, cols] for v in vgb], axis=1)) + bst_ref[:, g:g + 1]
            for i, r in enumerate(rows):
                mix_ref[r, ATTN_WIDTH + g * GM_DIM:ATTN_WIDTH + (g + 1) * GM_DIM] = (
                    gu[i][:, cols] * sm[:, i * GM_DIM:(i + 1) * GM_DIM]).astype(BF16)

    assert blocks % 2 == 0
    units = []
    for blk0 in range(0, blocks, 2):
        units += [attn_unit(blk, h) for blk in (blk0, blk0 + 1) for h in range(N_KV_HEADS)]
        units.append(gm_unit(blk0))
    return units


KV_COL = ATTN_WIDTH
UVG_COL = ATTN_WIDTH + 2 * KV_WIDTH


N_NEXT = 6


def _in_mix_kernel(tiles_per_seq, blocks, n_tiles,
                   x_ref, g1_ref, w1c_ref, w3c_ref, w2c_ref, gm_ref, winc_ref, bin_ref,
                   sinks_ref, lng_ref, lnb_ref, ws_ref, bst_ref, xs_ref, wq8_ref, bq8_ref,
                   *rest):
    next_f32, rest = rest[:N_NEXT], rest[N_NEXT:]
    (x1_ref, mix_ref, kvw_ref, gmv_ref, x1s_ref, q8s_ref, kvs_ref, uvgs_ref), rest = (
        rest[:8], rest[8:])
    next_bf16, rest = rest[:N_NEXT], rest[N_NEXT:]
    w1_s, w3_s, w2_s, win_s, q_s, kv_s, uvg_s, kvp_s = rest
    s = pl.program_id(0)
    t = s - CAST_STEPS

    @pl.when(s < CAST_STEPS)
    def _():
        for src, dst in ((w1c_ref, w1_s), (w3c_ref, w3_s), (w2c_ref, w2_s), (winc_ref, win_s)):
            _cast_chunk(s, src, dst)

    @pl.when(t == 0)
    def _():
        q_s[...] = jnp.zeros_like(q_s)
        kv_s[...] = jnp.zeros_like(kv_s)
        uvg_s[...] = jnp.zeros_like(uvg_s)
        kvp_s[...] = jnp.zeros_like(kvp_s)

    def ffn(x, before_chunk=None):
        h = _rms(x, g1_ref[...]).astype(BF16)
        return x + 0.5 * _swiglu(h, w1_s, w3_s, w2_s, before_chunk)

    def project(h2, cols):
        return _dot(h2, win_s[:, cols]) + bin_ref[:, cols]

    def ffn_with_previous_mix(x):
        first_of_seq = (jnp.maximum(t - 1, 0) & (tiles_per_seq - 1)) == 0
        units = _mix_units(blocks, first_of_seq, sinks_ref, q_s, kv_s, kvp_s, uvg_s, lng_ref,
                           lnb_ref, ws_ref, bst_ref, mix_ref, gmv_ref)
        in_flight = []

        def before_chunk(c):
            for unit in in_flight:
                next(unit, None)
            first, last = c * len(units) // N_FF_CHUNKS, (c + 1) * len(units) // N_FF_CHUNKS
            in_flight[:] = units[first:last]
            for unit in in_flight:
                next(unit)

        x1 = ffn(x, before_chunk)
        for unit in in_flight:
            next(unit, None)
        return x1

    @pl.when((t >= 0) & (t < n_tiles))
    def _():
        for src, dst in zip(next_f32, next_bf16):
            dst[...] = src[...].astype(BF16)
        x1 = ffn_with_previous_mix(x_ref[...])
        kvp_s[...] = kv_s[(blocks - 1) * WINDOW:, :]
        x1_ref[...] = x1
        h2 = _rms(x1, gm_ref[...]).astype(BF16)
        q_s[...] = (project(h2, slice(0, KV_COL)) * ATTN_SCALE).astype(BF16)
        kv = project(h2, slice(KV_COL, UVG_COL))
        kv_s[...] = kv
        kvw_ref[...] = kv[(blocks - 1) * WINDOW:, :]
        uvg_s[...] = project(h2, slice(UVG_COL, None))

    @pl.when(t == n_tiles)
    def _():
        x1 = ffn_with_previous_mix(xs_ref[...])
        x1s_ref[...] = x1
        h2 = _rms(x1, gm_ref[...]).astype(BF16)
        q8s_ref[...] = ((_dot(h2, wq8_ref[...]) + bq8_ref[...]) * ATTN_SCALE).astype(BF16)
        kvs_ref[...] = project(h2, slice(KV_COL, UVG_COL))
        uvgs_ref[...] = project(h2, slice(UVG_COL, None))


def _in_mix(x, xs, g1, w1, w3, w2, gmix, w_in, b_in, sinks, ln_g, ln_b, w_s, b_st, wq8, bq8,
            next_weights, tm):
    assert len(next_weights) == N_NEXT
    b, s, _ = x.shape
    n = xs.shape[0]
    nj = s // tm
    nt = b * nj
    blocks = tm // WINDOW
    seq_shift = nj.bit_length() - 1
    assert nj == 1 << seq_shift
    x = x.reshape(b * s, D_MODEL)

    def cur(step):
        return jnp.clip(step - CAST_STEPS, 0, nt - 1)

    def prev(step):
        return jnp.clip(step - CAST_STEPS - 1, 0, nt - 1)

    def tile(width, which):
        return pl.BlockSpec((tm, width), lambda step: (which(step), 0))

    def per_seq(rows, width, which):
        return pl.BlockSpec((None, rows, width),
                            lambda step: (lax.shift_right_logical(which(step), seq_shift), 0, 0))

    def whole(shape):
        return pl.BlockSpec(shape, lambda step: (0,) * len(shape))

    gconsts = (ln_g, ln_b, w_s, b_st)
    sample_out = ((n, D_MODEL), F32), ((n, N_Q_HEADS * LANES), BF16), ((n, 2 * KV_WIDTH), F32), (
        (n, 2 * GM_WIDTH), F32)
    return pl.pallas_call(
        functools.partial(_in_mix_kernel, nj, blocks, nt),
        out_shape=(jax.ShapeDtypeStruct((b * s, D_MODEL), F32),
                   jax.ShapeDtypeStruct((b * s, D_MODEL), BF16),
                   jax.ShapeDtypeStruct((b, WINDOW, 2 * KV_WIDTH), F32),
                   jax.ShapeDtypeStruct((b, CHUNK, GM_WIDTH), F32))
        + tuple(jax.ShapeDtypeStruct(shape, dtype) for shape, dtype in sample_out)
        + tuple(jax.ShapeDtypeStruct(w.shape, BF16) for w in next_weights),
        grid=(CAST_STEPS + nt + 1,),
        in_specs=[tile(D_MODEL, cur), _resident(g1.shape), _cast_spec(w1), _cast_spec(w3),
                  _cast_spec(w2), _resident(gmix.shape), _cast_spec(w_in), _resident(b_in.shape),
                  pl.BlockSpec(memory_space=pltpu.SMEM)]
        + [_resident(c.shape) for c in gconsts + (xs, wq8, bq8)]
        + [_trickle_spec(w, nt) for w in next_weights],
        out_specs=(tile(D_MODEL, cur), tile(D_MODEL, prev),
                   per_seq(WINDOW, 2 * KV_WIDTH, cur), per_seq(CHUNK, GM_WIDTH, prev))
        + tuple(whole(shape) for shape, _ in sample_out)
        + tuple(_trickle_spec(w, nt) for w in next_weights),
        scratch_shapes=[_bf16_scratch(w1), _bf16_scratch(w3), _bf16_scratch(w2),
                        _bf16_scratch(w_in),
                        pltpu.VMEM((tm, ATTN_WIDTH), BF16), pltpu.VMEM((tm, 2 * KV_WIDTH), F32),
                        pltpu.VMEM((tm, 2 * GM_WIDTH), F32),
                        pltpu.VMEM((WINDOW, 2 * KV_WIDTH), F32)],
        compiler_params=pltpu.CompilerParams(
            dimension_semantics=("arbitrary",), vmem_limit_bytes=VMEM_LIMIT_BYTES),
        name="in_mix",
    )(x, g1, w1, w3, w2, gmix, w_in, b_in, sinks, *gconsts, xs, wq8, bq8, *next_weights)


def _mix_sample_kernel(q8_ref, kn_ref, vn_ref, kvn_ref, kc_ref, vc_ref, sinks_ref, uvg_ref, lng_ref,
                       lnb_ref, w00_ref, b0_ref, o8_ref, gm_ref, gmv_ref, kw_ref, vw_ref):
    bb = q8_ref.shape[0]
    q8 = q8_ref[...]
    kc = kc_ref[...]
    vc = vc_ref[...]
    kn = kn_ref[...]
    vn = vn_ref[...]
    sink = sinks_ref[...]
    s_c = jnp.einsum("bqd,bdk->bqk", q8, kc.astype(BF16), preferred_element_type=F32)
    s_n = jnp.sum(q8.astype(F32) * kn.astype(BF16).astype(F32), axis=-1, keepdims=True)
    m = jnp.maximum(jnp.maximum(jnp.max(s_c, axis=-1, keepdims=True), s_n), sink)
    e_c = jnp.exp(s_c - m)
    e_n = jnp.exp(s_n - m)
    inv = 1.0 / (jnp.sum(e_c, axis=-1, keepdims=True) + e_n + jnp.exp(sink - m))
    p_c = (e_c * inv).astype(BF16)
    p_n = (e_n * inv).astype(BF16).astype(F32)
    o = jnp.einsum("bqk,bdk->bqd", p_c, vc.astype(BF16), preferred_element_type=F32)
    o = o + p_n * vn.astype(BF16).astype(F32)
    row = lax.broadcasted_iota(jnp.int32, o.shape, 1)
    lane = lax.broadcasted_iota(jnp.int32, o.shape, 2)
    own = (lane < HEAD_DIM) == (row < N_Q_HEADS // N_KV_HEADS)
    o8_ref[...] = jnp.where(own, o, 0.0).astype(BF16)

    pad = jnp.zeros((LANES - bb, 2 * KV_WIDTH), F32)
    new_t = jnp.concatenate([kvn_ref[...], pad], axis=0).T
    last = lax.broadcasted_iota(jnp.int32, (KV_WIDTH, WINDOW), 1) == WINDOW - 1
    for j in range(bb):
        col = new_t[:, j:j + 1]
        kw_ref[j] = jnp.where(last, col[:KV_WIDTH], pltpu.roll(kc[j], WINDOW - 1, 1))
        vw_ref[j] = jnp.where(last, col[KV_WIDTH:], pltpu.roll(vc[j], WINDOW - 1, 1))

    vgn = _gm_norm(uvg_ref[:, GM_WIDTH:], lng_ref[...], lnb_ref[...])
    gmv_ref[...] = vgn
    sm = w00_ref[...].astype(BF16).astype(F32) * vgn.astype(BF16).astype(F32) + b0_ref[...]
    gm_ref[...] = (_gelu(uvg_ref[:, :GM_WIDTH]) * sm).astype(BF16)


def _mix_sample(q8, kn, vn, kvn, kc, vc, sinks, uvg, ln_g, ln_b, w00, b0, bb):
    n = q8.shape[0]

    def b3(d1, d2):
        return pl.BlockSpec((bb, d1, d2), lambda i: (i, 0, 0))

    def b2(d):
        return pl.BlockSpec((bb, d), lambda i: (i, 0))

    return pl.pallas_call(
        _mix_sample_kernel,
        out_shape=(jax.ShapeDtypeStruct((n, N_Q_HEADS, LANES), BF16),
                   jax.ShapeDtypeStruct((n, GM_WIDTH), BF16),
                   jax.ShapeDtypeStruct((n, GM_WIDTH), F32),
                   jax.ShapeDtypeStruct((n, KV_WIDTH, WINDOW), F32),
                   jax.ShapeDtypeStruct((n, KV_WIDTH, WINDOW), F32)),
        grid=(n // bb,),
        in_specs=[b3(N_Q_HEADS, LANES), b3(1, KV_WIDTH), b3(1, KV_WIDTH), b2(2 * KV_WIDTH),
                  b3(KV_WIDTH, WINDOW), b3(KV_WIDTH, WINDOW),
                  _resident(sinks.shape), b2(2 * GM_WIDTH), _resident(ln_g.shape),
                  _resident(ln_b.shape), _resident(w00.shape), _resident(b0.shape)],
        out_specs=(b3(N_Q_HEADS, LANES), b2(GM_WIDTH), b2(GM_WIDTH),
                   b3(KV_WIDTH, WINDOW), b3(KV_WIDTH, WINDOW)),
        compiler_params=pltpu.CompilerParams(
            dimension_semantics=("arbitrary",), vmem_limit_bytes=VMEM_LIMIT_BYTES),
        name="mix_sample",
    )(q8, kn, vn, kvn, kc, vc, sinks, uvg, ln_g, ln_b, w00, b0)


def _out_stage_kernel(n_tiles, x1_ref, mix_ref, p_ref, wo_s, bo_ref, g2_ref, w1_s, w3_s, w2_s,
                      gp_ref, wg_s, wp_s, gf_ref, x1s_ref, mixs_ref, ps_ref, wo8_ref,
                      y_ref, ys_ref):
    t = pl.program_id(0)

    def out_stage(x1, mix, p, wo_ref):
        x2 = x1 + _dot(mix, wo_ref[...]) + bo_ref[...]
        h = _rms(x2, g2_ref[...]).astype(BF16)
        x3 = x2 + 0.5 * _swiglu(h, w1_s, w3_s, w2_s)
        gate = _sigmoid(_dot(_rms(x3, gp_ref[...]).astype(BF16), wg_s[...]))
        x4 = x3 + gate * _dot(p.astype(BF16), wp_s[...])
        return _rms(x4, gf_ref[...])

    @pl.when(t < n_tiles)
    def _():
        y_ref[...] = out_stage(x1_ref[...], mix_ref[...], p_ref[...], wo_s)

    @pl.when(t == n_tiles)
    def _():
        ys_ref[...] = out_stage(x1s_ref[...], mixs_ref[...], ps_ref[...], wo8_ref)


def _out_stage(x1, mix, p, x1s, mixs, ps, wo, bo, g2, w1, w3, w2, gp, wg, wp, gf, wo8, tm):
    r = x1.shape[0]
    n = x1s.shape[0]
    nt = r // tm

    def tile(width):
        return pl.BlockSpec((tm, width), lambda step: (jnp.minimum(step, nt - 1), 0))

    consts = (wo, bo, g2, w1, w3, w2, gp, wg, wp, gf, x1s, mixs, ps, wo8)
    return pl.pallas_call(
        functools.partial(_out_stage_kernel, nt),
        out_shape=(jax.ShapeDtypeStruct((r, D_MODEL), F32),
                   jax.ShapeDtypeStruct((n, D_MODEL), F32)),
        grid=(nt + 1,),
        in_specs=[tile(D_MODEL), tile(D_MODEL), tile(PLE_DIM)]
        + [_resident(c.shape) for c in consts],
        out_specs=(tile(D_MODEL), pl.BlockSpec((n, D_MODEL), lambda step: (0, 0))),
        compiler_params=pltpu.CompilerParams(
            dimension_semantics=("arbitrary",), vmem_limit_bytes=VMEM_LIMIT_BYTES),
        name="out_stage",
    )(x1, mix, p, wo, bo, g2, w1, w3, w2, gp, wg, wp, gf, x1s, mixs, ps, wo8)


PROMPT_TILE = 512
SAMPLE_BLOCK = 32


def _pad_heads(w):
    lead = w.shape[:-1]
    wh = w.reshape(lead + (N_Q_HEADS, HEAD_DIM))
    z = jnp.zeros_like(wh[..., :N_Q_HEADS // 2, :])
    first = jnp.concatenate([wh[..., :N_Q_HEADS // 2, :], z], axis=-1)
    second = jnp.concatenate([z, wh[..., N_Q_HEADS // 2:, :]], axis=-1)
    return jnp.concatenate([first, second], axis=-2).reshape(lead + (N_Q_HEADS * LANES,))


def kernel(x_prompt, x_sample, cache_k_win, cache_v_win, p_prompt, p_sample, g_ffn1, w1_ffn1, w3_ffn1, w2_ffn1, g_mix, w_in, b_in, attn_sinks, gm_ln_g, gm_ln_b, gm_w_s, gm_b_s, w_out, b_out, g_ffn2, w1_ffn2, w3_ffn2, w2_ffn2, g_ple, w_ple_gate, w_ple_proj, g_final):
    depth = g_ffn1.shape[0]
    assert depth == 1
    i = 0
    nb, seq, _ = x_prompt.shape
    nd = x_sample.shape[0]

    row = lambda a: a.reshape(1, -1)
    g1, gmx, g2, gp, gf = row(g_ffn1[i]), row(g_mix[i]), row(g_ffn2[i]), row(g_ple[i]), row(g_final)
    ln_g, ln_b = row(gm_ln_g[i]), row(gm_ln_b[i])
    sinks = attn_sinks[i]
    wq8, bq8 = _pad_heads(w_in[i][:, :ATTN_WIDTH]).astype(BF16), row(_pad_heads(b_in[i][:ATTN_WIDTH]))
    wo8 = jnp.concatenate([_pad_heads(w_out[i][:ATTN_WIDTH].T).T, w_out[i][ATTN_WIDTH:]],
                          axis=0).astype(BF16)

    out_weights = (w_out[i], w1_ffn2[i], w3_ffn2[i], w2_ffn2[i], w_ple_gate[i], w_ple_proj[i])
    x1, mix, kv_last, gmv_p, x1s, q8, kvs, uvgs, wo, w1b, w3b, w2b, wg, wp = _in_mix(
        x_prompt, x_sample.reshape(nd, D_MODEL), g1, w1_ffn1[i], w3_ffn1[i], w2_ffn1[i], gmx,
        w_in[i], row(b_in[i]), sinks, ln_g, ln_b, gm_w_s[i], gm_b_s[i].T, wq8, bq8,
        out_weights, PROMPT_TILE)
    k_win_p = kv_last[:, :, :KV_WIDTH].reshape(1, nb, WINDOW, N_KV_HEADS, HEAD_DIM)
    v_win_p = kv_last[:, :, KV_WIDTH:].reshape(1, nb, WINDOW, N_KV_HEADS, HEAD_DIM)

    def position_minor(c):
        return jnp.transpose(c, (0, 2, 3, 1)).reshape(nd, KV_WIDTH, WINDOW)

    def position_major(c):
        c = jnp.transpose(c.reshape(nd, N_KV_HEADS, HEAD_DIM, WINDOW), (0, 3, 1, 2))
        return c.reshape(1, nd, WINDOW, N_KV_HEADS, HEAD_DIM)

    kvs3 = kvs.reshape(nd, 1, 2 * KV_WIDTH)
    o8, gm_s, gmv_s, kw_s, vw_s = _mix_sample(
        q8.reshape(nd, N_Q_HEADS, LANES), kvs3[:, :, :KV_WIDTH], kvs3[:, :, KV_WIDTH:], kvs,
        position_minor(cache_k_win[i]), position_minor(cache_v_win[i]),
        sinks.reshape(N_Q_HEADS, 1), uvgs, ln_g, ln_b,
        jnp.repeat(gm_w_s[i][:, 0, 0], GM_DIM).reshape(1, GM_WIDTH),
        jnp.repeat(gm_b_s[i][:, 0], GM_DIM).reshape(1, GM_WIDTH), SAMPLE_BLOCK)
    mix_s = jnp.concatenate([o8.reshape(nd, N_Q_HEADS * LANES), gm_s], axis=-1)

    y_prompt, y_sample = _out_stage(
        x1, mix, p_prompt[i].reshape(nb * seq, PLE_DIM), x1s, mix_s,
        p_sample[i].reshape(nd, PLE_DIM), wo, row(b_out[i]),
        g2, w1b, w3b, w2b, gp, wg, wp, gf, wo8, 2 * PROMPT_TILE)

    return (y_prompt.reshape(nb, seq, D_MODEL), y_sample.reshape(nd, 1, D_MODEL), k_win_p, v_win_p,
            position_major(kw_s), position_major(vw_s),
            gmv_p.reshape(1, nb, CHUNK, GM_WIDTH), gmv_s.reshape(1, nd, 1, GM_WIDTH))
```

```python
import functools
import math

import jax
import jax.numpy as jnp
from jax import lax
from jax.experimental import pallas as pl
from jax.experimental.pallas import tpu as pltpu

F32 = jnp.float32
BF16 = jnp.bfloat16

D_MODEL = 1024
HEAD_DIM = 64
N_Q_HEADS = 8
N_KV_HEADS = 2
ATTN_WIDTH = N_Q_HEADS * HEAD_DIM
KV_WIDTH = N_KV_HEADS * HEAD_DIM
WINDOW = 128
GM_WIDTH = 512
GM_GROUPS = 4
GM_DIM = 128
CHUNK = 128
D_FF = 2816
PLE_DIM = 256
RMS_EPS = 1e-6
LN_EPS = 1e-5
ATTN_SCALE = HEAD_DIM ** -0.5

LANES = 128
FF_CHUNK = 512
FF_BOUNDS = tuple((lo, min(lo + FF_CHUNK, D_FF)) for lo in range(0, D_FF, FF_CHUNK))
N_FF_CHUNKS = len(FF_BOUNDS)
VMEM_LIMIT_BYTES = 62 * 1024 * 1024

_GELU_C = math.sqrt(2.0 / math.pi)


def _rms(x, g):
    return x * lax.rsqrt(jnp.mean(x * x, axis=-1, keepdims=True) + RMS_EPS) * g


def _sigmoid(x):
    return 1.0 / (1.0 + jnp.exp(-x))


def _gelu(x):
    return 0.5 * x * (1.0 + jnp.tanh(_GELU_C * (x + 0.044715 * (x * x * x))))


def _dot(a, b):
    return jnp.dot(a, b, preferred_element_type=F32)


def _dot_nt(a, b):
    return lax.dot_general(a, b, (((1,), (1,)), ((), ())), preferred_element_type=F32)


def _swiglu(h, w1_ref, w3_ref, w2_ref, before_chunk=None):
    acc = jnp.zeros((h.shape[0], D_MODEL), F32)
    for c, (lo, hi) in enumerate(FF_BOUNDS):
        if before_chunk is not None:
            before_chunk(c)
        cols = slice(lo, hi)
        a = _dot(h, w1_ref[:, cols])
        b = _dot(h, w3_ref[:, cols])
        act = (a * _sigmoid(a) * b).astype(BF16)
        acc = acc + _dot(act, w2_ref[cols, :])
    return acc


def _resident(shape):
    zeros = (0,) * len(shape)
    return pl.BlockSpec(shape, lambda *_: zeros, pipeline_mode=pl.Buffered(1))


CAST_STEPS = 4
BF16_SUBLANES = 16


CAST_BUFFERS = 1


def _cast_spec(w):
    rows = w.shape[0] // CAST_STEPS
    assert rows * CAST_STEPS == w.shape[0] and rows % BF16_SUBLANES == 0
    return pl.BlockSpec((rows, w.shape[1]), lambda s: (jnp.minimum(s, CAST_STEPS - 1), 0),
                        pipeline_mode=pl.Buffered(CAST_BUFFERS))


def _trickle_rows(w, max_chunks):
    rows = BF16_SUBLANES
    while w.shape[0] % rows or w.shape[0] // rows > max_chunks:
        rows += BF16_SUBLANES
    return rows


def _trickle_spec(w, max_chunks):
    rows = _trickle_rows(w, max_chunks)
    last = w.shape[0] // rows - 1
    return pl.BlockSpec((rows, w.shape[1]), lambda s: (jnp.clip(s - CAST_STEPS, 0, last), 0))


def _cast_chunk(s, src_ref, dst_ref):
    rows = src_ref.shape[0]
    start = pl.multiple_of(s * rows, BF16_SUBLANES)
    dst_ref[pl.ds(start, rows), :] = src_ref[...].astype(BF16)


def _bf16_scratch(w):
    return pltpu.VMEM(w.shape, BF16)


def _gm_norm(vg, ln_g, ln_b):
    a = _gelu(vg)
    mu = jnp.mean(a, axis=-1, keepdims=True)
    ac = a - mu
    return ac * lax.rsqrt(jnp.mean(ac * ac, axis=-1, keepdims=True) + LN_EPS) * ln_g + ln_b


def _mix_units(blocks, first_of_seq, sinks_ref, q_ref, kv_ref, kvp_ref, uvg_ref, lng_ref,
               lnb_ref, ws_ref, bst_ref, mix_ref, gmv_ref):
    lane = lax.broadcasted_iota(jnp.int32, (2 * WINDOW, LANES), 1)
    lo = lane < HEAD_DIM
    qi = lax.broadcasted_iota(jnp.int32, (WINDOW, 2 * WINDOW), 0)
    kj = lax.broadcasted_iota(jnp.int32, (WINDOW, 2 * WINDOW), 1)
    band = (kj >= qi) & (kj <= qi + WINDOW)
    ti = lax.broadcasted_iota(jnp.int32, (CHUNK, CHUNK), 0)
    si = lax.broadcasted_iota(jnp.int32, (CHUNK, CHUNK), 1)
    tril = si <= ti

    block_cache = {}

    def block_operands(blk):
        if blk not in block_cache:
            if blk == 0:
                kv2 = jnp.concatenate([kvp_ref[...], kv_ref[:WINDOW, :]], axis=0)
                mask = band & (kj >= jnp.where(first_of_seq, WINDOW, 0))
            else:
                kv2 = kv_ref[(blk - 1) * WINDOW:(blk + 1) * WINDOW, :]
                mask = band
            kt = kv2[:, :KV_WIDTH].T.astype(BF16)
            v2 = kv2[:, KV_WIDTH:].astype(BF16)
            v2r = pltpu.roll(v2, HEAD_DIM, 1)
            zero_v = jnp.zeros_like(v2)
            zero_k = jnp.zeros((HEAD_DIM, 2 * WINDOW), BF16)
            heads = [kt[h * HEAD_DIM:(h + 1) * HEAD_DIM, :] for h in range(N_KV_HEADS)]
            k_lo = [jnp.concatenate([kh, zero_k], axis=0) for kh in heads]
            k_hi = [jnp.concatenate([zero_k, kh], axis=0) for kh in heads]
            v_lo = (jnp.where(lo, v2, zero_v), jnp.where(lo, v2r, zero_v))
            v_hi = (jnp.where(lo, zero_v, v2r), jnp.where(lo, zero_v, v2))
            mask2 = jnp.concatenate([mask, mask], axis=0)
            block_cache[blk] = (mask2, k_lo, k_hi, v_lo, v_hi)
        return block_cache[blk]

    upper = lax.broadcasted_iota(jnp.int32, (2 * WINDOW, 1), 0) < WINDOW

    def attn_unit(blk, h):
        rows = slice(blk * WINDOW, (blk + 1) * WINDOW)
        mask2, k_lo, k_hi, v_lo, v_hi = block_operands(blk)
        slots = (2 * h, 2 * h + 1)
        q2 = jnp.concatenate([q_ref[rows, s * LANES:(s + 1) * LANES] for s in slots], axis=0)
        scores = [_dot(q2, kk) for kk in (k_lo[h], k_hi[h])]
        yield
        out = None
        for half, vv in enumerate((v_lo[h], v_hi[h])):
            sink = jnp.where(upper, sinks_ref[2 * slots[0] + half], sinks_ref[2 * slots[1] + half])
            s = jnp.where(mask2, scores[half], -jnp.inf)
            m = jnp.maximum(jnp.max(s, axis=-1, keepdims=True), sink)
            e = jnp.exp(s - m)
            denom = jnp.sum(e, axis=-1, keepdims=True) + jnp.exp(sink - m)
            p = (e * (1.0 / denom)).astype(BF16)
            o = _dot(p, vv)
            out = o if out is None else out + o
        out = out.astype(BF16)
        mix_ref[rows, slots[0] * LANES:(slots[0] + 1) * LANES] = out[:WINDOW]
        mix_ref[rows, slots[1] * LANES:(slots[1] + 1) * LANES] = out[WINDOW:]

    def gm_unit(blk0):
        pair = (blk0, blk0 + 1)
        rows = [slice(blk * WINDOW, (blk + 1) * WINDOW) for blk in pair]
        vgn = [_gm_norm(uvg_ref[r, GM_WIDTH:], lng_ref[...], lnb_ref[...]) for r in rows]
        if pair[1] == blocks - 1:
            gmv_ref[...] = vgn[1]
        vgb = [v.astype(BF16) for v in vgn]
        gu = [_gelu(uvg_ref[r, :GM_WIDTH]) for r in rows]
        yield
        for g in range(GM_GROUPS):
            cols = slice(g * GM_DIM, (g + 1) * GM_DIM)
            w = jnp.where(tril, ws_ref[g], 0.0).astype(BF16)
            sm = _dot(w, jnp.concatenate([v[:, cols] for v in vgb], axis=1)) + bst_ref[:, g:g + 1]
            for i, r in enumerate(rows):
                mix_ref[r, ATTN_WIDTH + g * GM_DIM:ATTN_WIDTH + (g + 1) * GM_DIM] = (
                    gu[i][:, cols] * sm[:, i * GM_DIM:(i + 1) * GM_DIM]).astype(BF16)

    assert blocks % 2 == 0
    units = []
    for blk0 in range(0, blocks, 2):
        units += [attn_unit(blk, h) for blk in (blk0, blk0 + 1) for h in range(N_KV_HEADS)]
        units.append(gm_unit(blk0))
    return units


KV_COL = ATTN_WIDTH
UVG_COL = ATTN_WIDTH + 2 * KV_WIDTH


def _heads_to_slots(q):
    lo = lax.broadcasted_iota(jnp.int32, (q.shape[0], LANES), 1) < HEAD_DIM
    slots = []
    for r in range(N_Q_HEADS):
        src = q[:, (r // 2) * LANES:(r // 2 + 1) * LANES]
        if r % 2 != r // (N_Q_HEADS // N_KV_HEADS):
            src = pltpu.roll(src, HEAD_DIM, 1)
        keep = lo if r // (N_Q_HEADS // N_KV_HEADS) == 0 else ~lo
        slots.append(jnp.where(keep, src, jnp.zeros_like(src)))
    return jnp.concatenate(slots, axis=1)


def _slots_to_heads(o):
    lo = lax.broadcasted_iota(jnp.int32, (o.shape[0], LANES), 1) < HEAD_DIM
    chunks = []
    for j in range(N_Q_HEADS // 2):
        halves = []
        for half, r in enumerate((2 * j, 2 * j + 1)):
            src = o[:, r * LANES:(r + 1) * LANES]
            if r // (N_Q_HEADS // N_KV_HEADS) != half:
                src = pltpu.roll(src, HEAD_DIM, 1)
            halves.append(src)
        chunks.append(jnp.where(lo, halves[0], halves[1]))
    return jnp.concatenate(chunks, axis=1)


N_NEXT = 6


def _in_mix_kernel(tiles_per_seq, blocks, n_tiles,
                   x_ref, g1_ref, w1c_ref, w3c_ref, w2c_ref, gm_ref, winc_ref, bin_ref,
                   sinks_ref, lng_ref, lnb_ref, ws_ref, bst_ref, xs_ref, *rest):
    next_f32, rest = rest[:N_NEXT], rest[N_NEXT:]
    (x1_ref, mix_ref, kvw_ref, gmv_ref, x1s_ref, q8s_ref, kvs_ref, uvgs_ref), rest = (
        rest[:8], rest[8:])
    next_bf16, rest = rest[:N_NEXT], rest[N_NEXT:]
    w1_s, w3_s, w2_s, win_s, q_s, kv_s, uvg_s, kvp_s = rest
    s = pl.program_id(0)
    t = s - CAST_STEPS

    @pl.when(s < CAST_STEPS)
    def _():
        for src, dst in ((w1c_ref, w1_s), (w3c_ref, w3_s), (w2c_ref, w2_s), (winc_ref, win_s)):
            _cast_chunk(s, src, dst)

    @pl.when(t == 0)
    def _():
        q_s[...] = jnp.zeros_like(q_s)
        kv_s[...] = jnp.zeros_like(kv_s)
        uvg_s[...] = jnp.zeros_like(uvg_s)
        kvp_s[...] = jnp.zeros_like(kvp_s)

    def ffn(x, before_chunk=None):
        h = _rms(x, g1_ref[...]).astype(BF16)
        return x + 0.5 * _swiglu(h, w1_s, w3_s, w2_s, before_chunk)

    def project(h2, cols):
        return _dot(h2, win_s[:, cols]) + bin_ref[:, cols]

    def ffn_with_previous_mix(x):
        first_of_seq = (jnp.maximum(t - 1, 0) & (tiles_per_seq - 1)) == 0
        units = _mix_units(blocks, first_of_seq, sinks_ref, q_s, kv_s, kvp_s, uvg_s, lng_ref,
                           lnb_ref, ws_ref, bst_ref, mix_ref, gmv_ref)
        in_flight = []

        def before_chunk(c):
            for unit in in_flight:
                next(unit, None)
            first, last = c * len(units) // N_FF_CHUNKS, (c + 1) * len(units) // N_FF_CHUNKS
            in_flight[:] = units[first:last]
            for unit in in_flight:
                next(unit)

        x1 = ffn(x, before_chunk)
        for unit in in_flight:
            next(unit, None)
        return x1

    @pl.when((t >= 0) & (t < n_tiles))
    def _():
        for src, dst in zip(next_f32, next_bf16):
            dst[...] = src[...].astype(BF16)
        x1 = ffn_with_previous_mix(x_ref[...])
        kvp_s[...] = kv_s[(blocks - 1) * WINDOW:, :]
        x1_ref[...] = x1
        h2 = _rms(x1, gm_ref[...]).astype(BF16)
        q_s[...] = (project(h2, slice(0, KV_COL)) * ATTN_SCALE).astype(BF16)
        kv = project(h2, slice(KV_COL, UVG_COL))
        kv_s[...] = kv
        kvw_ref[...] = kv[(blocks - 1) * WINDOW:, :]
        uvg_s[...] = project(h2, slice(UVG_COL, None))

    @pl.when(t == n_tiles)
    def _():
        x1 = ffn_with_previous_mix(xs_ref[...])
        x1s_ref[...] = x1
        h2 = _rms(x1, gm_ref[...]).astype(BF16)
        q8s_ref[...] = _heads_to_slots((project(h2, slice(0, KV_COL)) * ATTN_SCALE).astype(BF16))
        kvs_ref[...] = project(h2, slice(KV_COL, UVG_COL))
        uvgs_ref[...] = project(h2, slice(UVG_COL, None))


def _in_mix(x, xs, g1, w1, w3, w2, gmix, w_in, b_in, sinks, ln_g, ln_b, w_s, b_st,
            next_weights, tm):
    assert len(next_weights) == N_NEXT
    b, s, _ = x.shape
    n = xs.shape[0]
    nj = s // tm
    nt = b * nj
    blocks = tm // WINDOW
    seq_shift = nj.bit_length() - 1
    assert nj == 1 << seq_shift
    x = x.reshape(b * s, D_MODEL)

    def cur(step):
        return jnp.clip(step - CAST_STEPS, 0, nt - 1)

    def prev(step):
        return jnp.clip(step - CAST_STEPS - 1, 0, nt - 1)

    def tile(width, which):
        return pl.BlockSpec((tm, width), lambda step: (which(step), 0))

    def per_seq(rows, width, which):
        return pl.BlockSpec((None, rows, width),
                            lambda step: (lax.shift_right_logical(which(step), seq_shift), 0, 0))

    def whole(shape):
        return pl.BlockSpec(shape, lambda step: (0,) * len(shape))

    gconsts = (ln_g, ln_b, w_s, b_st)
    sample_out = ((n, D_MODEL), F32), ((n, N_Q_HEADS * LANES), BF16), ((n, 2 * KV_WIDTH), F32), (
        (n, 2 * GM_WIDTH), F32)
    return pl.pallas_call(
        functools.partial(_in_mix_kernel, nj, blocks, nt),
        out_shape=(jax.ShapeDtypeStruct((b * s, D_MODEL), F32),
                   jax.ShapeDtypeStruct((b * s, D_MODEL), BF16),
                   jax.ShapeDtypeStruct((b, WINDOW, 2 * KV_WIDTH), F32),
                   jax.ShapeDtypeStruct((b, CHUNK, GM_WIDTH), F32))
        + tuple(jax.ShapeDtypeStruct(shape, dtype) for shape, dtype in sample_out)
        + tuple(jax.ShapeDtypeStruct(w.shape, BF16) for w in next_weights),
        grid=(CAST_STEPS + nt + 1,),
        in_specs=[tile(D_MODEL, cur), _resident(g1.shape), _cast_spec(w1), _cast_spec(w3),
                  _cast_spec(w2), _resident(gmix.shape), _cast_spec(w_in), _resident(b_in.shape),
                  pl.BlockSpec(memory_space=pltpu.SMEM)]
        + [_resident(c.shape) for c in gconsts + (xs,)]
        + [_trickle_spec(w, nt) for w in next_weights],
        out_specs=(tile(D_MODEL, cur), tile(D_MODEL, prev),
                   per_seq(WINDOW, 2 * KV_WIDTH, cur), per_seq(CHUNK, GM_WIDTH, prev))
        + tuple(whole(shape) for shape, _ in sample_out)
        + tuple(_trickle_spec(w, nt) for w in next_weights),
        scratch_shapes=[_bf16_scratch(w1), _bf16_scratch(w3), _bf16_scratch(w2),
                        _bf16_scratch(w_in),
                        pltpu.VMEM((tm, ATTN_WIDTH), BF16), pltpu.VMEM((tm, 2 * KV_WIDTH), F32),
                        pltpu.VMEM((tm, 2 * GM_WIDTH), F32),
                        pltpu.VMEM((WINDOW, 2 * KV_WIDTH), F32)],
        compiler_params=pltpu.CompilerParams(
            dimension_semantics=("arbitrary",), vmem_limit_bytes=VMEM_LIMIT_BYTES),
        name="in_mix",
    )(x, g1, w1, w3, w2, gmix, w_in, b_in, sinks, *gconsts, xs, *next_weights)


def _mix_sample_kernel(q8_ref, kn_ref, vn_ref, kvn_ref, kc_ref, vc_ref, sinks_ref, uvg_ref, lng_ref,
                       lnb_ref, w00_ref, b0_ref, o8_ref, gm_ref, gmv_ref, kw_ref, vw_ref):
    bb = q8_ref.shape[0]
    q8 = q8_ref[...]
    kc = kc_ref[...]
    vc = vc_ref[...]
    kn = kn_ref[...]
    vn = vn_ref[...]
    sink = sinks_ref[...]
    s_c = jnp.einsum("bqd,bdk->bqk", q8, kc.astype(BF16), preferred_element_type=F32)
    s_n = jnp.sum(q8.astype(F32) * kn.astype(BF16).astype(F32), axis=-1, keepdims=True)
    m = jnp.maximum(jnp.maximum(jnp.max(s_c, axis=-1, keepdims=True), s_n), sink)
    e_c = jnp.exp(s_c - m)
    e_n = jnp.exp(s_n - m)
    inv = 1.0 / (jnp.sum(e_c, axis=-1, keepdims=True) + e_n + jnp.exp(sink - m))
    p_c = (e_c * inv).astype(BF16)
    p_n = (e_n * inv).astype(BF16).astype(F32)
    o = jnp.einsum("bqk,bdk->bqd", p_c, vc.astype(BF16), preferred_element_type=F32)
    o = o + p_n * vn.astype(BF16).astype(F32)
    row = lax.broadcasted_iota(jnp.int32, o.shape, 1)
    lane = lax.broadcasted_iota(jnp.int32, o.shape, 2)
    own = (lane < HEAD_DIM) == (row < N_Q_HEADS // N_KV_HEADS)
    o8_ref[...] = jnp.where(own, o, 0.0).astype(BF16)

    pad = jnp.zeros((LANES - bb, 2 * KV_WIDTH), F32)
    new_t = jnp.concatenate([kvn_ref[...], pad], axis=0).T
    last = lax.broadcasted_iota(jnp.int32, (KV_WIDTH, WINDOW), 1) == WINDOW - 1
    for j in range(bb):
        col = new_t[:, j:j + 1]
        kw_ref[j] = jnp.where(last, col[:KV_WIDTH], pltpu.roll(kc[j], WINDOW - 1, 1))
        vw_ref[j] = jnp.where(last, col[KV_WIDTH:], pltpu.roll(vc[j], WINDOW - 1, 1))

    vgn = _gm_norm(uvg_ref[:, GM_WIDTH:], lng_ref[...], lnb_ref[...])
    gmv_ref[...] = vgn
    sm = w00_ref[...].astype(BF16).astype(F32) * vgn.astype(BF16).astype(F32) + b0_ref[...]
    gm_ref[...] = (_gelu(uvg_ref[:, :GM_WIDTH]) * sm).astype(BF16)


def _mix_sample(q8, kn, vn, kvn, kc, vc, sinks, uvg, ln_g, ln_b, w00, b0, bb):
    n = q8.shape[0]

    def b3(d1, d2):
        return pl.BlockSpec((bb, d1, d2), lambda i: (i, 0, 0))

    def b2(d):
        return pl.BlockSpec((bb, d), lambda i: (i, 0))

    return pl.pallas_call(
        _mix_sample_kernel,
        out_shape=(jax.ShapeDtypeStruct((n, N_Q_HEADS, LANES), BF16),
                   jax.ShapeDtypeStruct((n, GM_WIDTH), BF16),
                   jax.ShapeDtypeStruct((n, GM_WIDTH), F32),
                   jax.ShapeDtypeStruct((n, KV_WIDTH, WINDOW), F32),
                   jax.ShapeDtypeStruct((n, KV_WIDTH, WINDOW), F32)),
        grid=(n // bb,),
        in_specs=[b3(N_Q_HEADS, LANES), b3(1, KV_WIDTH), b3(1, KV_WIDTH), b2(2 * KV_WIDTH),
                  b3(KV_WIDTH, WINDOW), b3(KV_WIDTH, WINDOW),
                  _resident(sinks.shape), b2(2 * GM_WIDTH), _resident(ln_g.shape),
                  _resident(ln_b.shape), _resident(w00.shape), _resident(b0.shape)],
        out_specs=(b3(N_Q_HEADS, LANES), b2(GM_WIDTH), b2(GM_WIDTH),
                   b3(KV_WIDTH, WINDOW), b3(KV_WIDTH, WINDOW)),
        compiler_params=pltpu.CompilerParams(
            dimension_semantics=("arbitrary",), vmem_limit_bytes=VMEM_LIMIT_BYTES),
        name="mix_sample",
    )(q8, kn, vn, kvn, kc, vc, sinks, uvg, ln_g, ln_b, w00, b0)


def _out_stage_kernel(n_tiles, x1_ref, mix_ref, p_ref, wo_s, bo_ref, g2_ref, w1_s, w3_s, w2_s,
                      gp_ref, wg_s, wp_s, gf_ref, x1s_ref, mixs_ref, ps_ref,
                      y_ref, ys_ref):
    t = pl.program_id(0)

    def out_stage(x1, mix, p):
        x2 = x1 + _dot(mix, wo_s[...]) + bo_ref[...]
        h = _rms(x2, g2_ref[...]).astype(BF16)
        x3 = x2 + 0.5 * _swiglu(h, w1_s, w3_s, w2_s)
        gate = _sigmoid(_dot(_rms(x3, gp_ref[...]).astype(BF16), wg_s[...]))
        x4 = x3 + gate * _dot(p.astype(BF16), wp_s[...])
        return _rms(x4, gf_ref[...])

    @pl.when(t < n_tiles)
    def _():
        y_ref[...] = out_stage(x1_ref[...], mix_ref[...], p_ref[...])

    @pl.when(t == n_tiles)
    def _():
        slots = N_Q_HEADS * LANES
        mix = jnp.concatenate([_slots_to_heads(mixs_ref[:, :slots]), mixs_ref[:, slots:]], axis=1)
        ys_ref[...] = out_stage(x1s_ref[...], mix, ps_ref[...])


def _out_stage(x1, mix, p, x1s, mixs, ps, wo, bo, g2, w1, w3, w2, gp, wg, wp, gf, tm):
    r = x1.shape[0]
    n = x1s.shape[0]
    nt = r // tm

    def tile(width):
        return pl.BlockSpec((tm, width), lambda step: (jnp.minimum(step, nt - 1), 0))

    consts = (wo, bo, g2, w1, w3, w2, gp, wg, wp, gf, x1s, mixs, ps)
    return pl.pallas_call(
        functools.partial(_out_stage_kernel, nt),
        out_shape=(jax.ShapeDtypeStruct((r, D_MODEL), F32),
                   jax.ShapeDtypeStruct((n, D_MODEL), F32)),
        grid=(nt + 1,),
        in_specs=[tile(D_MODEL), tile(D_MODEL), tile(PLE_DIM)]
        + [_resident(c.shape) for c in consts],
        out_specs=(tile(D_MODEL), pl.BlockSpec((n, D_MODEL), lambda step: (0, 0))),
        compiler_params=pltpu.CompilerParams(
            dimension_semantics=("arbitrary",), vmem_limit_bytes=VMEM_LIMIT_BYTES),
        name="out_stage",
    )(x1, mix, p, wo, bo, g2, w1, w3, w2, gp, wg, wp, gf, x1s, mixs, ps)


PROMPT_TILE = 512
SAMPLE_BLOCK = 32


def kernel(x_prompt, x_sample, cache_k_win, cache_v_win, p_prompt, p_sample, g_ffn1, w1_ffn1, w3_ffn1, w2_ffn1, g_mix, w_in, b_in, attn_sinks, gm_ln_g, gm_ln_b, gm_w_s, gm_b_s, w_out, b_out, g_ffn2, w1_ffn2, w3_ffn2, w2_ffn2, g_ple, w_ple_gate, w_ple_proj, g_final):
    depth = g_ffn1.shape[0]
    assert depth == 1
    i = 0
    nb, seq, _ = x_prompt.shape
    nd = x_sample.shape[0]

    row = lambda a: a.reshape(1, -1)
    g1, gmx, g2, gp, gf = row(g_ffn1[i]), row(g_mix[i]), row(g_ffn2[i]), row(g_ple[i]), row(g_final)
    ln_g, ln_b = row(gm_ln_g[i]), row(gm_ln_b[i])
    sinks = attn_sinks[i]
    out_weights = (w_out[i], w1_ffn2[i], w3_ffn2[i], w2_ffn2[i], w_ple_gate[i], w_ple_proj[i])
    x1, mix, kv_last, gmv_p, x1s, q8, kvs, uvgs, wo, w1b, w3b, w2b, wg, wp = _in_mix(
        x_prompt, x_sample.reshape(nd, D_MODEL), g1, w1_ffn1[i], w3_ffn1[i], w2_ffn1[i], gmx,
        w_in[i], row(b_in[i]), sinks, ln_g, ln_b, gm_w_s[i], gm_b_s[i].T,
        out_weights, PROMPT_TILE)
    k_win_p = kv_last[:, :, :KV_WIDTH].reshape(1, nb, WINDOW, N_KV_HEADS, HEAD_DIM)
    v_win_p = kv_last[:, :, KV_WIDTH:].reshape(1, nb, WINDOW, N_KV_HEADS, HEAD_DIM)

    def position_minor(c):
        return jnp.transpose(c, (0, 2, 3, 1)).reshape(nd, KV_WIDTH, WINDOW)

    def position_major(c):
        c = jnp.transpose(c.reshape(nd, N_KV_HEADS, HEAD_DIM, WINDOW), (0, 3, 1, 2))
        return c.reshape(1, nd, WINDOW, N_KV_HEADS, HEAD_DIM)

    kvs3 = kvs.reshape(nd, 1, 2 * KV_WIDTH)
    o8, gm_s, gmv_s, kw_s, vw_s = _mix_sample(
        q8.reshape(nd, N_Q_HEADS, LANES), kvs3[:, :, :KV_WIDTH], kvs3[:, :, KV_WIDTH:], kvs,
        position_minor(cache_k_win[i]), position_minor(cache_v_win[i]),
        sinks.reshape(N_Q_HEADS, 1), uvgs, ln_g, ln_b,
        jnp.repeat(gm_w_s[i][:, 0, 0], GM_DIM).reshape(1, GM_WIDTH),
        jnp.repeat(gm_b_s[i][:, 0], GM_DIM).reshape(1, GM_WIDTH), SAMPLE_BLOCK)
    mix_s = jnp.concatenate([o8.reshape(nd, N_Q_HEADS * LANES), gm_s], axis=-1)

    y_prompt, y_sample = _out_stage(
        x1, mix, p_prompt[i].reshape(nb * seq, PLE_DIM), x1s, mix_s,
        p_sample[i].reshape(nd, PLE_DIM), wo, row(b_out[i]),
        g2, w1b, w3b, w2b, gp, wg, wp, gf, 2 * PROMPT_TILE)

    return (y_prompt.reshape(nb, seq, D_MODEL), y_sample.reshape(nd, 1, D_MODEL), k_win_p, v_win_p,
            position_major(kw_s), position_major(vw_s),
            gmv_p.reshape(1, nb, CHUNK, GM_WIDTH), gmv_s.reshape(1, nd, 1, GM_WIDTH))
```

```python
import functools
import math

import jax
import jax.numpy as jnp
from jax import lax
from jax.experimental import pallas as pl
from jax.experimental.pallas import tpu as pltpu

F32 = jnp.float32
BF16 = jnp.bfloat16

D_MODEL = 1024
HEAD_DIM = 64
N_Q_HEADS = 8
N_KV_HEADS = 2
ATTN_WIDTH = N_Q_HEADS * HEAD_DIM
KV_WIDTH = N_KV_HEADS * HEAD_DIM
WINDOW = 128
GM_WIDTH = 512
GM_GROUPS = 4
GM_DIM = 128
CHUNK = 128
D_FF = 2816
PLE_DIM = 256
RMS_EPS = 1e-6
LN_EPS = 1e-5
ATTN_SCALE = HEAD_DIM ** -0.5

LANES = 128
FF_CHUNK = 512
FF_BOUNDS = tuple((lo, min(lo + FF_CHUNK, D_FF)) for lo in range(0, D_FF, FF_CHUNK))
N_FF_CHUNKS = len(FF_BOUNDS)
VMEM_LIMIT_BYTES = 62 * 1024 * 1024

_GELU_C = math.sqrt(2.0 / math.pi)


def _rms(x, g):
    return x * lax.rsqrt(jnp.mean(x * x, axis=-1, keepdims=True) + RMS_EPS) * g


def _sigmoid(x):
    return 1.0 / (1.0 + jnp.exp(-x))


def _gelu(x):
    return 0.5 * x * (1.0 + jnp.tanh(_GELU_C * (x + 0.044715 * (x * x * x))))


def _dot(a, b):
    return jnp.dot(a, b, preferred_element_type=F32)


def _swiglu(h, w1_ref, w3_ref, w2_ref, before_chunk=None):
    acc = jnp.zeros((h.shape[0], D_MODEL), F32)
    for c, (lo, hi) in enumerate(FF_BOUNDS):
        if before_chunk is not None:
            before_chunk(c)
        cols = slice(lo, hi)
        a = _dot(h, w1_ref[:, cols])
        b = _dot(h, w3_ref[:, cols])
        act = (a * _sigmoid(a) * b).astype(BF16)
        acc = acc + _dot(act, w2_ref[cols, :])
    return acc


def _resident(shape):
    zeros = (0,) * len(shape)
    return pl.BlockSpec(shape, lambda *_: zeros, pipeline_mode=pl.Buffered(1))


CAST_STEPS = 4
CAST_BUFFERS = 1
BF16_SUBLANES = 16


def _cast_spec(w):
    rows = w.shape[0] // CAST_STEPS
    assert rows * CAST_STEPS == w.shape[0] and rows % BF16_SUBLANES == 0
    return pl.BlockSpec((rows, w.shape[1]), lambda s: (jnp.minimum(s, CAST_STEPS - 1), 0),
                        pipeline_mode=pl.Buffered(CAST_BUFFERS))


def _trickle_rows(w, max_chunks):
    rows = BF16_SUBLANES
    while w.shape[0] % rows or w.shape[0] // rows > max_chunks:
        rows += BF16_SUBLANES
    return rows


def _trickle_spec(w, max_chunks):
    rows = _trickle_rows(w, max_chunks)
    last = w.shape[0] // rows - 1
    return pl.BlockSpec((rows, w.shape[1]), lambda s: (jnp.clip(s - CAST_STEPS, 0, last), 0))


def _cast_chunk(s, src_ref, dst_ref):
    rows = src_ref.shape[0]
    start = pl.multiple_of(s * rows, BF16_SUBLANES)
    dst_ref[pl.ds(start, rows), :] = src_ref[...].astype(BF16)


def _bf16_scratch(w):
    return pltpu.VMEM(w.shape, BF16)


def _gm_norm(vg, ln_g, ln_b):
    a = _gelu(vg)
    mu = jnp.mean(a, axis=-1, keepdims=True)
    ac = a - mu
    return ac * lax.rsqrt(jnp.mean(ac * ac, axis=-1, keepdims=True) + LN_EPS) * ln_g + ln_b


def _mix_units(blocks, first_of_seq, sinks_ref, q_ref, kv_ref, kvp_ref, uvg_ref, lng_ref,
               lnb_ref, ws_ref, bst_ref, mix_ref, gmv_ref):
    lane = lax.broadcasted_iota(jnp.int32, (2 * WINDOW, LANES), 1)
    lo = lane < HEAD_DIM
    qi = lax.broadcasted_iota(jnp.int32, (WINDOW, 2 * WINDOW), 0)
    kj = lax.broadcasted_iota(jnp.int32, (WINDOW, 2 * WINDOW), 1)
    band = (kj >= qi) & (kj <= qi + WINDOW)
    ti = lax.broadcasted_iota(jnp.int32, (CHUNK, CHUNK), 0)
    si = lax.broadcasted_iota(jnp.int32, (CHUNK, CHUNK), 1)
    tril = si <= ti

    block_cache = {}

    def block_operands(blk):
        if blk not in block_cache:
            if blk == 0:
                kvp = jnp.where(first_of_seq, 0.0, kvp_ref[...])
                kv2 = jnp.concatenate([kvp, kv_ref[:WINDOW, :]], axis=0)
                mask = band & (kj >= jnp.where(first_of_seq, WINDOW, 0))
            else:
                kv2 = kv_ref[(blk - 1) * WINDOW:(blk + 1) * WINDOW, :]
                mask = band
            kt = kv2[:, :KV_WIDTH].T.astype(BF16)
            v2 = kv2[:, KV_WIDTH:].astype(BF16)
            v2r = pltpu.roll(v2, HEAD_DIM, 1)
            zero_v = jnp.zeros_like(v2)
            zero_k = jnp.zeros((HEAD_DIM, 2 * WINDOW), BF16)
            heads = [kt[h * HEAD_DIM:(h + 1) * HEAD_DIM, :] for h in range(N_KV_HEADS)]
            k_lo = [jnp.concatenate([kh, zero_k], axis=0) for kh in heads]
            k_hi = [jnp.concatenate([zero_k, kh], axis=0) for kh in heads]
            v_lo = (jnp.where(lo, v2, zero_v), jnp.where(lo, v2r, zero_v))
            v_hi = (jnp.where(lo, zero_v, v2r), jnp.where(lo, zero_v, v2))
            mask2 = jnp.concatenate([mask, mask], axis=0)
            block_cache[blk] = (mask2, k_lo, k_hi, v_lo, v_hi)
        return block_cache[blk]

    upper = lax.broadcasted_iota(jnp.int32, (2 * WINDOW, 1), 0) < WINDOW

    def attn_unit(blk, h):
        rows = slice(blk * WINDOW, (blk + 1) * WINDOW)
        mask2, k_lo, k_hi, v_lo, v_hi = block_operands(blk)
        slots = (2 * h, 2 * h + 1)
        q2 = jnp.concatenate([q_ref[rows, s * LANES:(s + 1) * LANES] for s in slots], axis=0)
        scores = [_dot(q2, kk) for kk in (k_lo[h], k_hi[h])]
        yield
        out = None
        for half, vv in enumerate((v_lo[h], v_hi[h])):
            sink = jnp.where(upper, sinks_ref[2 * slots[0] + half], sinks_ref[2 * slots[1] + half])
            s = jnp.where(mask2, scores[half], -jnp.inf)
            m = jnp.maximum(jnp.max(s, axis=-1, keepdims=True), sink)
            e = jnp.exp(s - m)
            denom = jnp.sum(e, axis=-1, keepdims=True) + jnp.exp(sink - m)
            p = (e * (1.0 / denom)).astype(BF16)
            o = _dot(p, vv)
            out = o if out is None else out + o
        out = out.astype(BF16)
        mix_ref[rows, slots[0] * LANES:(slots[0] + 1) * LANES] = out[:WINDOW]
        mix_ref[rows, slots[1] * LANES:(slots[1] + 1) * LANES] = out[WINDOW:]

    def gm_unit(blk0):
        pair = (blk0, blk0 + 1)
        rows = [slice(blk * WINDOW, (blk + 1) * WINDOW) for blk in pair]
        vgn = [_gm_norm(uvg_ref[r, GM_WIDTH:], lng_ref[...], lnb_ref[...]) for r in rows]
        if pair[1] == blocks - 1:
            gmv_ref[...] = vgn[1]
        vgb = [v.astype(BF16) for v in vgn]
        gu = [_gelu(uvg_ref[r, :GM_WIDTH]) for r in rows]
        yield
        for g in range(GM_GROUPS):
            cols = slice(g * GM_DIM, (g + 1) * GM_DIM)
            w = jnp.where(tril, ws_ref[g], 0.0).astype(BF16)
            sm = _dot(w, jnp.concatenate([v[:, cols] for v in vgb], axis=1)) + bst_ref[:, g:g + 1]
            for i, r in enumerate(rows):
                mix_ref[r, ATTN_WIDTH + g * GM_DIM:ATTN_WIDTH + (g + 1) * GM_DIM] = (
                    gu[i][:, cols] * sm[:, i * GM_DIM:(i + 1) * GM_DIM]).astype(BF16)

    assert blocks % 2 == 0
    units = []
    for blk0 in range(0, blocks, 2):
        units += [attn_unit(blk, h) for blk in (blk0, blk0 + 1) for h in range(N_KV_HEADS)]
        units.append(gm_unit(blk0))
    return units


KV_COL = ATTN_WIDTH
UVG_COL = ATTN_WIDTH + 2 * KV_WIDTH


def _heads_to_slots(q):
    lo = lax.broadcasted_iota(jnp.int32, (q.shape[0], LANES), 1) < HEAD_DIM
    slots = []
    for r in range(N_Q_HEADS):
        src = q[:, (r // 2) * LANES:(r // 2 + 1) * LANES]
        if r % 2 != r // (N_Q_HEADS // N_KV_HEADS):
            src = pltpu.roll(src, HEAD_DIM, 1)
        keep = lo if r // (N_Q_HEADS // N_KV_HEADS) == 0 else ~lo
        slots.append(jnp.where(keep, src, jnp.zeros_like(src)))
    return jnp.concatenate(slots, axis=1)


def _slots_to_heads(o):
    lo = lax.broadcasted_iota(jnp.int32, (o.shape[0], LANES), 1) < HEAD_DIM
    chunks = []
    for j in range(N_Q_HEADS // 2):
        halves = []
        for half, r in enumerate((2 * j, 2 * j + 1)):
            src = o[:, r * LANES:(r + 1) * LANES]
            if r // (N_Q_HEADS // N_KV_HEADS) != half:
                src = pltpu.roll(src, HEAD_DIM, 1)
            halves.append(src)
        chunks.append(jnp.where(lo, halves[0], halves[1]))
    return jnp.concatenate(chunks, axis=1)


N_NEXT = 6


def _in_mix_kernel(tiles_per_seq, blocks, n_tiles,
                   x_ref, g1_ref, w1c_ref, w3c_ref, w2c_ref, gm_ref, winc_ref, bin_ref,
                   sinks_ref, lng_ref, lnb_ref, ws_ref, bst_ref, xs_ref, *rest):
    next_f32, rest = rest[:N_NEXT], rest[N_NEXT:]
    (x1_ref, mix_ref, kvw_ref, gmv_ref, x1s_ref, q8s_ref, kvs_ref, uvgs_ref), rest = (
        rest[:8], rest[8:])
    next_bf16, rest = rest[:N_NEXT], rest[N_NEXT:]
    w1_s, w3_s, w2_s, win_s, q_s, kv_s, uvg_s, kvp_s = rest
    s = pl.program_id(0)
    t = s - CAST_STEPS

    @pl.when(s < CAST_STEPS)
    def _():
        for src, dst in ((w1c_ref, w1_s), (w3c_ref, w3_s), (w2c_ref, w2_s), (winc_ref, win_s)):
            _cast_chunk(s, src, dst)

    @pl.when(t == 0)
    def _():
        q_s[...] = jnp.zeros_like(q_s)
        kv_s[...] = jnp.zeros_like(kv_s)
        uvg_s[...] = jnp.zeros_like(uvg_s)
        kvp_s[...] = jnp.zeros_like(kvp_s)

    def ffn(x, before_chunk=None):
        h = _rms(x, g1_ref[...]).astype(BF16)
        return x + 0.5 * _swiglu(h, w1_s, w3_s, w2_s, before_chunk)

    def project(h2, cols):
        return _dot(h2, win_s[:, cols]) + bin_ref[:, cols]

    def ffn_with_previous_mix(x):
        first_of_seq = (jnp.maximum(t - 1, 0) & (tiles_per_seq - 1)) == 0
        units = _mix_units(blocks, first_of_seq, sinks_ref, q_s, kv_s, kvp_s, uvg_s, lng_ref,
                           lnb_ref, ws_ref, bst_ref, mix_ref, gmv_ref)
        in_flight = []

        def before_chunk(c):
            for unit in in_flight:
                next(unit, None)
            first, last = c * len(units) // N_FF_CHUNKS, (c + 1) * len(units) // N_FF_CHUNKS
            in_flight[:] = units[first:last]
            for unit in in_flight:
                next(unit)

        x1 = ffn(x, before_chunk)
        for unit in in_flight:
            next(unit, None)
        return x1

    @pl.when((t >= 0) & (t < n_tiles))
    def _():
        for src, dst in zip(next_f32, next_bf16):
            dst[...] = src[...].astype(BF16)
        x1 = ffn_with_previous_mix(x_ref[...])
        kvp_s[...] = kv_s[(blocks - 1) * WINDOW:, :]
        x1_ref[...] = x1
        h2 = _rms(x1, gm_ref[...]).astype(BF16)
        q_s[...] = (project(h2, slice(0, KV_COL)) * ATTN_SCALE).astype(BF16)
        kv = project(h2, slice(KV_COL, UVG_COL))
        kv_s[...] = kv
        kvw_ref[...] = kv[(blocks - 1) * WINDOW:, :]
        uvg_s[...] = project(h2, slice(UVG_COL, None))

    @pl.when(t == n_tiles)
    def _():
        x1 = ffn_with_previous_mix(xs_ref[...])
        x1s_ref[...] = x1
        h2 = _rms(x1, gm_ref[...]).astype(BF16)
        q8s_ref[...] = _heads_to_slots((project(h2, slice(0, KV_COL)) * ATTN_SCALE).astype(BF16))
        kvs_ref[...] = project(h2, slice(KV_COL, UVG_COL))
        uvgs_ref[...] = project(h2, slice(UVG_COL, None))


def _in_mix(x, xs, g1, w1, w3, w2, gmix, w_in, b_in, sinks, ln_g, ln_b, w_s, b_st,
            next_weights, tm):
    assert len(next_weights) == N_NEXT
    b, s, _ = x.shape
    n = xs.shape[0]
    nj = s // tm
    nt = b * nj
    blocks = tm // WINDOW
    seq_shift = nj.bit_length() - 1
    assert nj == 1 << seq_shift
    x = x.reshape(b * s, D_MODEL)

    def cur(step):
        return jnp.clip(step - CAST_STEPS, 0, nt - 1)

    def prev(step):
        return jnp.clip(step - CAST_STEPS - 1, 0, nt - 1)

    def tile(width, which):
        return pl.BlockSpec((tm, width), lambda step: (which(step), 0))

    def per_seq(rows, width, which):
        return pl.BlockSpec((None, rows, width),
                            lambda step: (lax.shift_right_logical(which(step), seq_shift), 0, 0))

    def whole(shape):
        return pl.BlockSpec(shape, lambda step: (0,) * len(shape))

    gconsts = (ln_g, ln_b, w_s, b_st)
    sample_out = ((n, D_MODEL), F32), ((n, N_Q_HEADS * LANES), BF16), ((n, 2 * KV_WIDTH), F32), (
        (n, 2 * GM_WIDTH), F32)
    return pl.pallas_call(
        functools.partial(_in_mix_kernel, nj, blocks, nt),
        out_shape=(jax.ShapeDtypeStruct((b * s, D_MODEL), F32),
                   jax.ShapeDtypeStruct((b * s, D_MODEL), BF16),
                   jax.ShapeDtypeStruct((b, WINDOW, 2 * KV_WIDTH), F32),
                   jax.ShapeDtypeStruct((b, CHUNK, GM_WIDTH), F32))
        + tuple(jax.ShapeDtypeStruct(shape, dtype) for shape, dtype in sample_out)
        + tuple(jax.ShapeDtypeStruct(w.shape, BF16) for w in next_weights),
        grid=(CAST_STEPS + nt + 1,),
        in_specs=[tile(D_MODEL, cur), _resident(g1.shape), _cast_spec(w1), _cast_spec(w3),
                  _cast_spec(w2), _resident(gmix.shape), _cast_spec(w_in), _resident(b_in.shape),
                  pl.BlockSpec(memory_space=pltpu.SMEM)]
        + [_resident(c.shape) for c in gconsts + (xs,)]
        + [_trickle_spec(w, nt) for w in next_weights],
        out_specs=(tile(D_MODEL, cur), tile(D_MODEL, prev),
                   per_seq(WINDOW, 2 * KV_WIDTH, cur), per_seq(CHUNK, GM_WIDTH, prev))
        + tuple(whole(shape) for shape, _ in sample_out)
        + tuple(_trickle_spec(w, nt) for w in next_weights),
        scratch_shapes=[_bf16_scratch(w1), _bf16_scratch(w3), _bf16_scratch(w2),
                        _bf16_scratch(w_in),
                        pltpu.VMEM((tm, ATTN_WIDTH), BF16), pltpu.VMEM((tm, 2 * KV_WIDTH), F32),
                        pltpu.VMEM((tm, 2 * GM_WIDTH), F32),
                        pltpu.VMEM((WINDOW, 2 * KV_WIDTH), F32)],
        compiler_params=pltpu.CompilerParams(
            dimension_semantics=("arbitrary",), vmem_limit_bytes=VMEM_LIMIT_BYTES),
        name="in_mix",
    )(x, g1, w1, w3, w2, gmix, w_in, b_in, sinks, *gconsts, xs, *next_weights)


def _mix_sample_kernel(q8_ref, kn_ref, vn_ref, kvn_ref, kc_ref, vc_ref, sinks_ref, uvg_ref, lng_ref,
                       lnb_ref, w00_ref, b0_ref, o8_ref, gm_ref, gmv_ref, kw_ref, vw_ref):
    bb = q8_ref.shape[0]
    q8 = q8_ref[...]
    kc = kc_ref[...]
    vc = vc_ref[...]
    kn = kn_ref[...]
    vn = vn_ref[...]
    sink = sinks_ref[...]
    s_c = jnp.einsum("bqd,bdk->bqk", q8, kc.astype(BF16), preferred_element_type=F32)
    s_n = jnp.sum(q8.astype(F32) * kn.astype(BF16).astype(F32), axis=-1, keepdims=True)
    m = jnp.maximum(jnp.maximum(jnp.max(s_c, axis=-1, keepdims=True), s_n), sink)
    e_c = jnp.exp(s_c - m)
    e_n = jnp.exp(s_n - m)
    inv = 1.0 / (jnp.sum(e_c, axis=-1, keepdims=True) + e_n + jnp.exp(sink - m))
    p_c = (e_c * inv).astype(BF16)
    p_n = (e_n * inv).astype(BF16).astype(F32)
    o = jnp.einsum("bqk,bdk->bqd", p_c, vc.astype(BF16), preferred_element_type=F32)
    o = o + p_n * vn.astype(BF16).astype(F32)
    row = lax.broadcasted_iota(jnp.int32, o.shape, 1)
    lane = lax.broadcasted_iota(jnp.int32, o.shape, 2)
    own = (lane < HEAD_DIM) == (row < N_Q_HEADS // N_KV_HEADS)
    o8_ref[...] = jnp.where(own, o, 0.0).astype(BF16)

    pad = jnp.zeros((LANES - bb, 2 * KV_WIDTH), F32)
    new_t = jnp.concatenate([kvn_ref[...], pad], axis=0).T
    last = lax.broadcasted_iota(jnp.int32, (KV_WIDTH, WINDOW), 1) == WINDOW - 1
    for j in range(bb):
        col = new_t[:, j:j + 1]
        kw_ref[j] = jnp.where(last, col[:KV_WIDTH], pltpu.roll(kc[j], WINDOW - 1, 1))
        vw_ref[j] = jnp.where(last, col[KV_WIDTH:], pltpu.roll(vc[j], WINDOW - 1, 1))

    vgn = _gm_norm(uvg_ref[:, GM_WIDTH:], lng_ref[...], lnb_ref[...])
    gmv_ref[...] = vgn
    sm = w00_ref[...].astype(BF16).astype(F32) * vgn.astype(BF16).astype(F32) + b0_ref[...]
    gm_ref[...] = (_gelu(uvg_ref[:, :GM_WIDTH]) * sm).astype(BF16)


def _mix_sample(q8, kn, vn, kvn, kc, vc, sinks, uvg, ln_g, ln_b, w00, b0, bb):
    n = q8.shape[0]

    def b3(d1, d2):
        return pl.BlockSpec((bb, d1, d2), lambda i: (i, 0, 0))

    def b2(d):
        return pl.BlockSpec((bb, d), lambda i: (i, 0))

    return pl.pallas_call(
        _mix_sample_kernel,
        out_shape=(jax.ShapeDtypeStruct((n, N_Q_HEADS, LANES), BF16),
                   jax.ShapeDtypeStruct((n, GM_WIDTH), BF16),
                   jax.ShapeDtypeStruct((n, GM_WIDTH), F32),
                   jax.ShapeDtypeStruct((n, KV_WIDTH, WINDOW), F32),
                   jax.ShapeDtypeStruct((n, KV_WIDTH, WINDOW), F32)),
        grid=(n // bb,),
        in_specs=[b3(N_Q_HEADS, LANES), b3(1, KV_WIDTH), b3(1, KV_WIDTH), b2(2 * KV_WIDTH),
                  b3(KV_WIDTH, WINDOW), b3(KV_WIDTH, WINDOW),
                  _resident(sinks.shape), b2(2 * GM_WIDTH), _resident(ln_g.shape),
                  _resident(ln_b.shape), _resident(w00.shape), _resident(b0.shape)],
        out_specs=(b3(N_Q_HEADS, LANES), b2(GM_WIDTH), b2(GM_WIDTH),
                   b3(KV_WIDTH, WINDOW), b3(KV_WIDTH, WINDOW)),
        compiler_params=pltpu.CompilerParams(
            dimension_semantics=("arbitrary",), vmem_limit_bytes=VMEM_LIMIT_BYTES),
        name="mix_sample",
    )(q8, kn, vn, kvn, kc, vc, sinks, uvg, ln_g, ln_b, w00, b0)


def _out_stage_kernel(n_tiles, x1_ref, mix_ref, p_ref, wo_s, bo_ref, g2_ref, w1_s, w3_s, w2_s,
                      gp_ref, wg_s, wp_s, gf_ref, x1s_ref, mixs_ref, ps_ref,
                      y_ref, ys_ref):
    t = pl.program_id(0)

    def out_stage(x1, mix, p):
        x2 = x1 + _dot(mix, wo_s[...]) + bo_ref[...]
        h = _rms(x2, g2_ref[...]).astype(BF16)
        x3 = x2 + 0.5 * _swiglu(h, w1_s, w3_s, w2_s)
        gate = _sigmoid(_dot(_rms(x3, gp_ref[...]).astype(BF16), wg_s[...]))
        x4 = x3 + gate * _dot(p.astype(BF16), wp_s[...])
        return _rms(x4, gf_ref[...])

    @pl.when(t < n_tiles)
    def _():
        y_ref[...] = out_stage(x1_ref[...], mix_ref[...], p_ref[...])

    @pl.when(t == n_tiles)
    def _():
        slots = N_Q_HEADS * LANES
        mix = jnp.concatenate([_slots_to_heads(mixs_ref[:, :slots]), mixs_ref[:, slots:]], axis=1)
        ys_ref[...] = out_stage(x1s_ref[...], mix, ps_ref[...])


def _out_stage(x1, mix, p, x1s, mixs, ps, wo, bo, g2, w1, w3, w2, gp, wg, wp, gf, tm):
    r = x1.shape[0]
    n = x1s.shape[0]
    nt = r // tm

    def tile(width):
        return pl.BlockSpec((tm, width), lambda step: (jnp.minimum(step, nt - 1), 0))

    consts = (wo, bo, g2, w1, w3, w2, gp, wg, wp, gf, x1s, mixs, ps)
    return pl.pallas_call(
        functools.partial(_out_stage_kernel, nt),
        out_shape=(jax.ShapeDtypeStruct((r, D_MODEL), F32),
                   jax.ShapeDtypeStruct((n, D_MODEL), F32)),
        grid=(nt + 1,),
        in_specs=[tile(D_MODEL), tile(D_MODEL), tile(PLE_DIM)]
        + [_resident(c.shape) for c in consts],
        out_specs=(tile(D_MODEL), pl.BlockSpec((n, D_MODEL), lambda step: (0, 0))),
        compiler_params=pltpu.CompilerParams(
            dimension_semantics=("arbitrary",), vmem_limit_bytes=VMEM_LIMIT_BYTES),
        name="out_stage",
    )(x1, mix, p, wo, bo, g2, w1, w3, w2, gp, wg, wp, gf, x1s, mixs, ps)


PROMPT_TILE = 512
SAMPLE_BLOCK = 32


def kernel(x_prompt, x_sample, cache_k_win, cache_v_win, p_prompt, p_sample, g_ffn1, w1_ffn1, w3_ffn1, w2_ffn1, g_mix, w_in, b_in, attn_sinks, gm_ln_g, gm_ln_b, gm_w_s, gm_b_s, w_out, b_out, g_ffn2, w1_ffn2, w3_ffn2, w2_ffn2, g_ple, w_ple_gate, w_ple_proj, g_final):
    depth = g_ffn1.shape[0]
    assert depth == 1
    i = 0
    nb, seq, _ = x_prompt.shape
    nd = x_sample.shape[0]

    row = lambda a: a.reshape(1, -1)
    g1, gmx, g2, gp, gf = row(g_ffn1[i]), row(g_mix[i]), row(g_ffn2[i]), row(g_ple[i]), row(g_final)
    ln_g, ln_b = row(gm_ln_g[i]), row(gm_ln_b[i])
    sinks = attn_sinks[i]
    out_weights = (w_out[i], w1_ffn2[i], w3_ffn2[i], w2_ffn2[i], w_ple_gate[i], w_ple_proj[i])
    x1, mix, kv_last, gmv_p, x1s, q8, kvs, uvgs, wo, w1b, w3b, w2b, wg, wp = _in_mix(
        x_prompt, x_sample.reshape(nd, D_MODEL), g1, w1_ffn1[i], w3_ffn1[i], w2_ffn1[i], gmx,
        w_in[i], row(b_in[i]), sinks, ln_g, ln_b, gm_w_s[i], gm_b_s[i].T,
        out_weights, PROMPT_TILE)
    k_win_p = kv_last[:, :, :KV_WIDTH].reshape(1, nb, WINDOW, N_KV_HEADS, HEAD_DIM)
    v_win_p = kv_last[:, :, KV_WIDTH:].reshape(1, nb, WINDOW, N_KV_HEADS, HEAD_DIM)

    def position_minor(c):
        return jnp.transpose(c, (0, 2, 3, 1)).reshape(nd, KV_WIDTH, WINDOW)

    def position_major(c):
        c = jnp.transpose(c.reshape(nd, N_KV_HEADS, HEAD_DIM, WINDOW), (0, 3, 1, 2))
        return c.reshape(1, nd, WINDOW, N_KV_HEADS, HEAD_DIM)

    kvs3 = kvs.reshape(nd, 1, 2 * KV_WIDTH)
    o8, gm_s, gmv_s, kw_s, vw_s = _mix_sample(
        q8.reshape(nd, N_Q_HEADS, LANES), kvs3[:, :, :KV_WIDTH], kvs3[:, :, KV_WIDTH:], kvs,
        position_minor(cache_k_win[i]), position_minor(cache_v_win[i]),
        sinks.reshape(N_Q_HEADS, 1), uvgs, ln_g, ln_b,
        jnp.repeat(gm_w_s[i][:, 0, 0], GM_DIM).reshape(1, GM_WIDTH),
        jnp.repeat(gm_b_s[i][:, 0], GM_DIM).reshape(1, GM_WIDTH), SAMPLE_BLOCK)
    mix_s = jnp.concatenate([o8.reshape(nd, N_Q_HEADS * LANES), gm_s], axis=-1)

    y_prompt, y_sample = _out_stage(
        x1, mix, p_prompt[i].reshape(nb * seq, PLE_DIM), x1s, mix_s,
        p_sample[i].reshape(nd, PLE_DIM), wo, row(b_out[i]),
        g2, w1b, w3b, w2b, gp, wg, wp, gf, 2 * PROMPT_TILE)

    return (y_prompt.reshape(nb, seq, D_MODEL), y_sample.reshape(nd, 1, D_MODEL), k_win_p, v_win_p,
            position_major(kw_s), position_major(vw_s),
            gmv_p.reshape(1, nb, CHUNK, GM_WIDTH), gmv_s.reshape(1, nd, 1, GM_WIDTH))
```

```python
import functools
import math

import jax
import jax.numpy as jnp
from jax import lax
from jax.experimental import pallas as pl
from jax.experimental.pallas import tpu as pltpu

F32 = jnp.float32
BF16 = jnp.bfloat16

D_MODEL = 1024
HEAD_DIM = 64
N_Q_HEADS = 8
N_KV_HEADS = 2
ATTN_WIDTH = N_Q_HEADS * HEAD_DIM
KV_WIDTH = N_KV_HEADS * HEAD_DIM
WINDOW = 128
GM_WIDTH = 512
GM_GROUPS = 4
GM_DIM = 128
CHUNK = 128
D_FF = 2816
PLE_DIM = 256
RMS_EPS = 1e-6
LN_EPS = 1e-5
ATTN_SCALE = HEAD_DIM ** -0.5

LANES = 128
FF_CHUNK = 512
FF_BOUNDS = tuple((lo, min(lo + FF_CHUNK, D_FF)) for lo in range(0, D_FF, FF_CHUNK))
N_FF_CHUNKS = len(FF_BOUNDS)
VMEM_LIMIT_BYTES = 62 * 1024 * 1024

_GELU_C = math.sqrt(2.0 / math.pi)


def _rms(x, g):
    return x * lax.rsqrt(jnp.mean(x * x, axis=-1, keepdims=True) + RMS_EPS) * g


def _sigmoid(x):
    return 1.0 / (1.0 + jnp.exp(-x))


def _gelu(x):
    return 0.5 * x * (1.0 + jnp.tanh(_GELU_C * (x + 0.044715 * (x * x * x))))


def _dot(a, b):
    return jnp.dot(a, b, preferred_element_type=F32)


def _swiglu(h, w1_ref, w3_ref, w2_ref, before_chunk=None):
    acc = jnp.zeros((h.shape[0], D_MODEL), F32)
    for c, (lo, hi) in enumerate(FF_BOUNDS):
        if before_chunk is not None:
            before_chunk(c)
        cols = slice(lo, hi)
        a = _dot(h, w1_ref[:, cols])
        b = _dot(h, w3_ref[:, cols])
        act = (a * _sigmoid(a) * b).astype(BF16)
        acc = acc + _dot(act, w2_ref[cols, :])
    return acc


def _resident(shape):
    zeros = (0,) * len(shape)
    return pl.BlockSpec(shape, lambda *_: zeros, pipeline_mode=pl.Buffered(1))


CAST_STEPS = 1
STAGE_ROWS = 128
STAGE_SLOTS = 4
BF16_SUBLANES = 16


def _stage_weights(pairs, stage, sems):
    chunks = [(src, dst, r0) for src, dst in pairs for r0 in range(0, src.shape[0], STAGE_ROWS)]

    def copy(i):
        src, _, r0 = chunks[i]
        slot = i % STAGE_SLOTS
        return pltpu.make_async_copy(src.at[pl.ds(r0, STAGE_ROWS), :],
                                     stage.at[slot, :, pl.ds(0, src.shape[1])], sems.at[slot])

    for i in range(min(STAGE_SLOTS, len(chunks))):
        copy(i).start()
    for i, (src, dst, r0) in enumerate(chunks):
        copy(i).wait()
        dst[pl.ds(r0, STAGE_ROWS), :] = stage[i % STAGE_SLOTS, :, :src.shape[1]].astype(BF16)
        if i + STAGE_SLOTS < len(chunks):
            copy(i + STAGE_SLOTS).start()


def _trickle_rows(w, max_chunks):
    rows = BF16_SUBLANES
    while w.shape[0] % rows or w.shape[0] // rows > max_chunks:
        rows += BF16_SUBLANES
    return rows


def _trickle_spec(w, max_chunks):
    rows = _trickle_rows(w, max_chunks)
    last = w.shape[0] // rows - 1
    return pl.BlockSpec((rows, w.shape[1]), lambda s: (jnp.clip(s - CAST_STEPS, 0, last), 0))


def _bf16_scratch(w):
    return pltpu.VMEM(w.shape, BF16)


def _gm_norm(vg, ln_g, ln_b):
    a = _gelu(vg)
    mu = jnp.mean(a, axis=-1, keepdims=True)
    ac = a - mu
    return ac * lax.rsqrt(jnp.mean(ac * ac, axis=-1, keepdims=True) + LN_EPS) * ln_g + ln_b


def _mix_units(blocks, first_of_seq, sinks_ref, q_ref, kv_ref, kvp_ref, uvg_ref, lng_ref,
               lnb_ref, ws_ref, bst_ref, mix_ref, gmv_ref):
    lane = lax.broadcasted_iota(jnp.int32, (2 * WINDOW, LANES), 1)
    lo = lane < HEAD_DIM
    qi = lax.broadcasted_iota(jnp.int32, (WINDOW, 2 * WINDOW), 0)
    kj = lax.broadcasted_iota(jnp.int32, (WINDOW, 2 * WINDOW), 1)
    band = (kj >= qi) & (kj <= qi + WINDOW)
    ti = lax.broadcasted_iota(jnp.int32, (CHUNK, CHUNK), 0)
    si = lax.broadcasted_iota(jnp.int32, (CHUNK, CHUNK), 1)
    tril = si <= ti

    block_cache = {}

    def block_operands(blk):
        if blk not in block_cache:
            if blk == 0:
                kvp = jnp.where(first_of_seq, 0.0, kvp_ref[...])
                kv2 = jnp.concatenate([kvp, kv_ref[:WINDOW, :]], axis=0)
                mask = band & (kj >= jnp.where(first_of_seq, WINDOW, 0))
            else:
                kv2 = kv_ref[(blk - 1) * WINDOW:(blk + 1) * WINDOW, :]
                mask = band
            kt = kv2[:, :KV_WIDTH].T.astype(BF16)
            v2 = kv2[:, KV_WIDTH:].astype(BF16)
            v2r = pltpu.roll(v2, HEAD_DIM, 1)
            zero_v = jnp.zeros_like(v2)
            zero_k = jnp.zeros((HEAD_DIM, 2 * WINDOW), BF16)
            heads = [kt[h * HEAD_DIM:(h + 1) * HEAD_DIM, :] for h in range(N_KV_HEADS)]
            k_lo = [jnp.concatenate([kh, zero_k], axis=0) for kh in heads]
            k_hi = [jnp.concatenate([zero_k, kh], axis=0) for kh in heads]
            v_lo = (jnp.where(lo, v2, zero_v), jnp.where(lo, v2r, zero_v))
            v_hi = (jnp.where(lo, zero_v, v2r), jnp.where(lo, zero_v, v2))
            mask2 = jnp.concatenate([mask, mask], axis=0)
            block_cache[blk] = (mask2, k_lo, k_hi, v_lo, v_hi)
        return block_cache[blk]

    upper = lax.broadcasted_iota(jnp.int32, (2 * WINDOW, 1), 0) < WINDOW

    def attn_unit(blk, h):
        rows = slice(blk * WINDOW, (blk + 1) * WINDOW)
        mask2, k_lo, k_hi, v_lo, v_hi = block_operands(blk)
        slots = (2 * h, 2 * h + 1)
        q2 = jnp.concatenate([q_ref[rows, s * LANES:(s + 1) * LANES] for s in slots], axis=0)
        scores = [_dot(q2, kk) for kk in (k_lo[h], k_hi[h])]
        yield
        out = None
        for half, vv in enumerate((v_lo[h], v_hi[h])):
            sink = jnp.where(upper, sinks_ref[2 * slots[0] + half], sinks_ref[2 * slots[1] + half])
            s = jnp.where(mask2, scores[half], -jnp.inf)
            m = jnp.maximum(jnp.max(s, axis=-1, keepdims=True), sink)
            e = jnp.exp(s - m)
            denom = jnp.sum(e, axis=-1, keepdims=True) + jnp.exp(sink - m)
            p = (e * (1.0 / denom)).astype(BF16)
            o = _dot(p, vv)
            out = o if out is None else out + o
        out = out.astype(BF16)
        mix_ref[rows, slots[0] * LANES:(slots[0] + 1) * LANES] = out[:WINDOW]
        mix_ref[rows, slots[1] * LANES:(slots[1] + 1) * LANES] = out[WINDOW:]

    def gm_unit(blk0):
        pair = (blk0, blk0 + 1)
        rows = [slice(blk * WINDOW, (blk + 1) * WINDOW) for blk in pair]
        vgn = [_gm_norm(uvg_ref[r, GM_WIDTH:], lng_ref[...], lnb_ref[...]) for r in rows]
        if pair[1] == blocks - 1:
            gmv_ref[...] = vgn[1]
        vgb = [v.astype(BF16) for v in vgn]
        gu = [_gelu(uvg_ref[r, :GM_WIDTH]) for r in rows]
        yield
        for g in range(GM_GROUPS):
            cols = slice(g * GM_DIM, (g + 1) * GM_DIM)
            w = jnp.where(tril, ws_ref[g], 0.0).astype(BF16)
            sm = _dot(w, jnp.concatenate([v[:, cols] for v in vgb], axis=1)) + bst_ref[:, g:g + 1]
            for i, r in enumerate(rows):
                mix_ref[r, ATTN_WIDTH + g * GM_DIM:ATTN_WIDTH + (g + 1) * GM_DIM] = (
                    gu[i][:, cols] * sm[:, i * GM_DIM:(i + 1) * GM_DIM]).astype(BF16)

    assert blocks % 2 == 0
    units = []
    for blk0 in range(0, blocks, 2):
        units += [attn_unit(blk, h) for blk in (blk0, blk0 + 1) for h in range(N_KV_HEADS)]
        units.append(gm_unit(blk0))
    return units


KV_COL = ATTN_WIDTH
UVG_COL = ATTN_WIDTH + 2 * KV_WIDTH


def _heads_to_slots(q):
    lo = lax.broadcasted_iota(jnp.int32, (q.shape[0], LANES), 1) < HEAD_DIM
    slots = []
    for r in range(N_Q_HEADS):
        src = q[:, (r // 2) * LANES:(r // 2 + 1) * LANES]
        if r % 2 != r // (N_Q_HEADS // N_KV_HEADS):
            src = pltpu.roll(src, HEAD_DIM, 1)
        keep = lo if r // (N_Q_HEADS // N_KV_HEADS) == 0 else ~lo
        slots.append(jnp.where(keep, src, jnp.zeros_like(src)))
    return jnp.concatenate(slots, axis=1)


def _slots_to_heads(o):
    lo = lax.broadcasted_iota(jnp.int32, (o.shape[0], LANES), 1) < HEAD_DIM
    chunks = []
    for j in range(N_Q_HEADS // 2):
        halves = []
        for half, r in enumerate((2 * j, 2 * j + 1)):
            src = o[:, r * LANES:(r + 1) * LANES]
            if r // (N_Q_HEADS // N_KV_HEADS) != half:
                src = pltpu.roll(src, HEAD_DIM, 1)
            halves.append(src)
        chunks.append(jnp.where(lo, halves[0], halves[1]))
    return jnp.concatenate(chunks, axis=1)


N_NEXT = 6


def _in_mix_kernel(tiles_per_seq, blocks, n_tiles,
                   x_ref, g1_ref, w1_hbm, w3_hbm, w2_hbm, gm_ref, win_hbm, bin_ref,
                   sinks_ref, lng_ref, lnb_ref, ws_ref, bst_ref, xs_ref, *rest):
    next_f32, rest = rest[:N_NEXT], rest[N_NEXT:]
    (x1_ref, mix_ref, kvw_ref, gmv_ref, x1s_ref, q8s_ref, kvs_ref, uvgs_ref), rest = (
        rest[:8], rest[8:])
    next_bf16, rest = rest[:N_NEXT], rest[N_NEXT:]
    w1_s, w3_s, w2_s, win_s, q_s, kv_s, uvg_s, kvp_s, stage, stage_sems = rest
    s = pl.program_id(0)
    t = s - CAST_STEPS

    @pl.when(s < CAST_STEPS)
    def _():
        _stage_weights(((w1_hbm, w1_s), (w3_hbm, w3_s), (w2_hbm, w2_s), (win_hbm, win_s)),
                       stage, stage_sems)

    @pl.when(t == 0)
    def _():
        q_s[...] = jnp.zeros_like(q_s)
        kv_s[...] = jnp.zeros_like(kv_s)
        uvg_s[...] = jnp.zeros_like(uvg_s)
        kvp_s[...] = jnp.zeros_like(kvp_s)

    def ffn(x, before_chunk=None):
        h = _rms(x, g1_ref[...]).astype(BF16)
        return x + 0.5 * _swiglu(h, w1_s, w3_s, w2_s, before_chunk)

    def project(h2, cols):
        return _dot(h2, win_s[:, cols]) + bin_ref[:, cols]

    def ffn_with_previous_mix(x):
        first_of_seq = (jnp.maximum(t - 1, 0) & (tiles_per_seq - 1)) == 0
        units = _mix_units(blocks, first_of_seq, sinks_ref, q_s, kv_s, kvp_s, uvg_s, lng_ref,
                           lnb_ref, ws_ref, bst_ref, mix_ref, gmv_ref)
        in_flight = []

        def before_chunk(c):
            for unit in in_flight:
                next(unit, None)
            first, last = c * len(units) // N_FF_CHUNKS, (c + 1) * len(units) // N_FF_CHUNKS
            in_flight[:] = units[first:last]
            for unit in in_flight:
                next(unit)

        x1 = ffn(x, before_chunk)
        for unit in in_flight:
            next(unit, None)
        return x1

    @pl.when((t >= 0) & (t < n_tiles))
    def _():
        for src, dst in zip(next_f32, next_bf16):
            dst[...] = src[...].astype(BF16)
        x1 = ffn_with_previous_mix(x_ref[...])
        kvp_s[...] = kv_s[(blocks - 1) * WINDOW:, :]
        x1_ref[...] = x1
        h2 = _rms(x1, gm_ref[...]).astype(BF16)
        q_s[...] = (project(h2, slice(0, KV_COL)) * ATTN_SCALE).astype(BF16)
        kv = project(h2, slice(KV_COL, UVG_COL))
        kv_s[...] = kv
        kvw_ref[...] = kv[(blocks - 1) * WINDOW:, :]
        uvg_s[...] = project(h2, slice(UVG_COL, None))

    @pl.when(t == n_tiles)
    def _():
        x1 = ffn_with_previous_mix(xs_ref[...])
        x1s_ref[...] = x1
        h2 = _rms(x1, gm_ref[...]).astype(BF16)
        q8s_ref[...] = _heads_to_slots((project(h2, slice(0, KV_COL)) * ATTN_SCALE).astype(BF16))
        kvs_ref[...] = project(h2, slice(KV_COL, UVG_COL))
        uvgs_ref[...] = project(h2, slice(UVG_COL, None))


def _in_mix(x, xs, g1, w1, w3, w2, gmix, w_in, b_in, sinks, ln_g, ln_b, w_s, b_st,
            next_weights, tm):
    assert len(next_weights) == N_NEXT
    b, s, _ = x.shape
    n = xs.shape[0]
    nj = s // tm
    nt = b * nj
    blocks = tm // WINDOW
    seq_shift = nj.bit_length() - 1
    assert nj == 1 << seq_shift
    x = x.reshape(b * s, D_MODEL)

    def cur(step):
        return jnp.clip(step - CAST_STEPS, 0, nt - 1)

    def prev(step):
        return jnp.clip(step - CAST_STEPS - 1, 0, nt - 1)

    def tile(width, which):
        return pl.BlockSpec((tm, width), lambda step: (which(step), 0))

    def per_seq(rows, width, which):
        return pl.BlockSpec((None, rows, width),
                            lambda step: (lax.shift_right_logical(which(step), seq_shift), 0, 0))

    def whole(shape):
        return pl.BlockSpec(shape, lambda step: (0,) * len(shape))

    in_hbm = pl.BlockSpec(memory_space=pl.ANY)
    staged = (w1, w3, w2, w_in)
    stage_width = max(w.shape[1] for w in staged)
    assert all(w.shape[0] % STAGE_ROWS == 0 and w.shape[1] % LANES == 0 for w in staged)

    gconsts = (ln_g, ln_b, w_s, b_st)
    sample_out = ((n, D_MODEL), F32), ((n, N_Q_HEADS * LANES), BF16), ((n, 2 * KV_WIDTH), F32), (
        (n, 2 * GM_WIDTH), F32)
    return pl.pallas_call(
        functools.partial(_in_mix_kernel, nj, blocks, nt),
        out_shape=(jax.ShapeDtypeStruct((b * s, D_MODEL), F32),
                   jax.ShapeDtypeStruct((b * s, D_MODEL), BF16),
                   jax.ShapeDtypeStruct((b, WINDOW, 2 * KV_WIDTH), F32),
                   jax.ShapeDtypeStruct((b, CHUNK, GM_WIDTH), F32))
        + tuple(jax.ShapeDtypeStruct(shape, dtype) for shape, dtype in sample_out)
        + tuple(jax.ShapeDtypeStruct(w.shape, BF16) for w in next_weights),
        grid=(CAST_STEPS + nt + 1,),
        in_specs=[tile(D_MODEL, cur), _resident(g1.shape), in_hbm, in_hbm, in_hbm,
                  _resident(gmix.shape), in_hbm, _resident(b_in.shape),
                  pl.BlockSpec(memory_space=pltpu.SMEM)]
        + [_resident(c.shape) for c in gconsts + (xs,)]
        + [_trickle_spec(w, nt) for w in next_weights],
        out_specs=(tile(D_MODEL, cur), tile(D_MODEL, prev),
                   per_seq(WINDOW, 2 * KV_WIDTH, cur), per_seq(CHUNK, GM_WIDTH, prev))
        + tuple(whole(shape) for shape, _ in sample_out)
        + tuple(_trickle_spec(w, nt) for w in next_weights),
        scratch_shapes=[_bf16_scratch(w1), _bf16_scratch(w3), _bf16_scratch(w2),
                        _bf16_scratch(w_in),
                        pltpu.VMEM((tm, ATTN_WIDTH), BF16), pltpu.VMEM((tm, 2 * KV_WIDTH), F32),
                        pltpu.VMEM((tm, 2 * GM_WIDTH), F32),
                        pltpu.VMEM((WINDOW, 2 * KV_WIDTH), F32),
                        pltpu.VMEM((STAGE_SLOTS, STAGE_ROWS, stage_width), F32),
                        pltpu.SemaphoreType.DMA((STAGE_SLOTS,))],
        compiler_params=pltpu.CompilerParams(
            dimension_semantics=("arbitrary",), vmem_limit_bytes=VMEM_LIMIT_BYTES),
        name="in_mix",
    )(x, g1, w1, w3, w2, gmix, w_in, b_in, sinks, *gconsts, xs, *next_weights)


def _mix_sample_kernel(q8_ref, kn_ref, vn_ref, kvn_ref, kc_ref, vc_ref, sinks_ref, uvg_ref, lng_ref,
                       lnb_ref, w00_ref, b0_ref, o8_ref, gm_ref, gmv_ref, kw_ref, vw_ref):
    bb = q8_ref.shape[0]
    q8 = q8_ref[...]
    kc = kc_ref[...]
    vc = vc_ref[...]
    kn = kn_ref[...]
    vn = vn_ref[...]
    sink = sinks_ref[...]
    s_c = jnp.einsum("bqd,bdk->bqk", q8, kc.astype(BF16), preferred_element_type=F32)
    s_n = jnp.sum(q8.astype(F32) * kn.astype(BF16).astype(F32), axis=-1, keepdims=True)
    m = jnp.maximum(jnp.maximum(jnp.max(s_c, axis=-1, keepdims=True), s_n), sink)
    e_c = jnp.exp(s_c - m)
    e_n = jnp.exp(s_n - m)
    inv = 1.0 / (jnp.sum(e_c, axis=-1, keepdims=True) + e_n + jnp.exp(sink - m))
    p_c = (e_c * inv).astype(BF16)
    p_n = (e_n * inv).astype(BF16).astype(F32)
    o = jnp.einsum("bqk,bdk->bqd", p_c, vc.astype(BF16), preferred_element_type=F32)
    o = o + p_n * vn.astype(BF16).astype(F32)
    row = lax.broadcasted_iota(jnp.int32, o.shape, 1)
    lane = lax.broadcasted_iota(jnp.int32, o.shape, 2)
    own = (lane < HEAD_DIM) == (row < N_Q_HEADS // N_KV_HEADS)
    o8_ref[...] = jnp.where(own, o, 0.0).astype(BF16)

    pad = jnp.zeros((LANES - bb, 2 * KV_WIDTH), F32)
    new_t = jnp.concatenate([kvn_ref[...], pad], axis=0).T
    last = lax.broadcasted_iota(jnp.int32, (KV_WIDTH, WINDOW), 1) == WINDOW - 1
    for j in range(bb):
        col = new_t[:, j:j + 1]
        kw_ref[j] = jnp.where(last, col[:KV_WIDTH], pltpu.roll(kc[j], WINDOW - 1, 1))
        vw_ref[j] = jnp.where(last, col[KV_WIDTH:], pltpu.roll(vc[j], WINDOW - 1, 1))

    vgn = _gm_norm(uvg_ref[:, GM_WIDTH:], lng_ref[...], lnb_ref[...])
    gmv_ref[...] = vgn
    sm = w00_ref[...].astype(BF16).astype(F32) * vgn.astype(BF16).astype(F32) + b0_ref[...]
    gm_ref[...] = (_gelu(uvg_ref[:, :GM_WIDTH]) * sm).astype(BF16)


def _mix_sample(q8, kn, vn, kvn, kc, vc, sinks, uvg, ln_g, ln_b, w00, b0, bb):
    n = q8.shape[0]

    def b3(d1, d2):
        return pl.BlockSpec((bb, d1, d2), lambda i: (i, 0, 0))

    def b2(d):
        return pl.BlockSpec((bb, d), lambda i: (i, 0))

    return pl.pallas_call(
        _mix_sample_kernel,
        out_shape=(jax.ShapeDtypeStruct((n, N_Q_HEADS, LANES), BF16),
                   jax.ShapeDtypeStruct((n, GM_WIDTH), BF16),
                   jax.ShapeDtypeStruct((n, GM_WIDTH), F32),
                   jax.ShapeDtypeStruct((n, KV_WIDTH, WINDOW), F32),
                   jax.ShapeDtypeStruct((n, KV_WIDTH, WINDOW), F32)),
        grid=(n // bb,),
        in_specs=[b3(N_Q_HEADS, LANES), b3(1, KV_WIDTH), b3(1, KV_WIDTH), b2(2 * KV_WIDTH),
                  b3(KV_WIDTH, WINDOW), b3(KV_WIDTH, WINDOW),
                  _resident(sinks.shape), b2(2 * GM_WIDTH), _resident(ln_g.shape),
                  _resident(ln_b.shape), _resident(w00.shape), _resident(b0.shape)],
        out_specs=(b3(N_Q_HEADS, LANES), b2(GM_WIDTH), b2(GM_WIDTH),
                   b3(KV_WIDTH, WINDOW), b3(KV_WIDTH, WINDOW)),
        compiler_params=pltpu.CompilerParams(
            dimension_semantics=("arbitrary",), vmem_limit_bytes=VMEM_LIMIT_BYTES),
        name="mix_sample",
    )(q8, kn, vn, kvn, kc, vc, sinks, uvg, ln_g, ln_b, w00, b0)


def _out_stage_kernel(n_tiles, x1_ref, mix_ref, p_ref, wo_s, bo_ref, g2_ref, w1_s, w3_s, w2_s,
                      gp_ref, wg_s, wp_s, gf_ref, x1s_ref, mixs_ref, ps_ref,
                      y_ref, ys_ref):
    t = pl.program_id(0)

    def out_stage(x1, mix, p):
        x2 = x1 + _dot(mix, wo_s[...]) + bo_ref[...]
        h = _rms(x2, g2_ref[...]).astype(BF16)
        x3 = x2 + 0.5 * _swiglu(h, w1_s, w3_s, w2_s)
        gate = _sigmoid(_dot(_rms(x3, gp_ref[...]).astype(BF16), wg_s[...]))
        x4 = x3 + gate * _dot(p.astype(BF16), wp_s[...])
        return _rms(x4, gf_ref[...])

    @pl.when(t < n_tiles)
    def _():
        y_ref[...] = out_stage(x1_ref[...], mix_ref[...], p_ref[...])

    @pl.when(t == n_tiles)
    def _():
        slots = N_Q_HEADS * LANES
        mix = jnp.concatenate([_slots_to_heads(mixs_ref[:, :slots]), mixs_ref[:, slots:]], axis=1)
        ys_ref[...] = out_stage(x1s_ref[...], mix, ps_ref[...])


def _out_stage(x1, mix, p, x1s, mixs, ps, wo, bo, g2, w1, w3, w2, gp, wg, wp, gf, tm):
    r = x1.shape[0]
    n = x1s.shape[0]
    nt = r // tm

    def tile(width):
        return pl.BlockSpec((tm, width), lambda step: (jnp.minimum(step, nt - 1), 0))

    consts = (wo, bo, g2, w1, w3, w2, gp, wg, wp, gf, x1s, mixs, ps)
    return pl.pallas_call(
        functools.partial(_out_stage_kernel, nt),
        out_shape=(jax.ShapeDtypeStruct((r, D_MODEL), F32),
                   jax.ShapeDtypeStruct((n, D_MODEL), F32)),
        grid=(nt + 1,),
        in_specs=[tile(D_MODEL), tile(D_MODEL), tile(PLE_DIM)]
        + [_resident(c.shape) for c in consts],
        out_specs=(tile(D_MODEL), pl.BlockSpec((n, D_MODEL), lambda step: (0, 0))),
        compiler_params=pltpu.CompilerParams(
            dimension_semantics=("arbitrary",), vmem_limit_bytes=VMEM_LIMIT_BYTES),
        name="out_stage",
    )(x1, mix, p, wo, bo, g2, w1, w3, w2, gp, wg, wp, gf, x1s, mixs, ps)


PROMPT_TILE = 512
SAMPLE_BLOCK = 32


def kernel(x_prompt, x_sample, cache_k_win, cache_v_win, p_prompt, p_sample, g_ffn1, w1_ffn1, w3_ffn1, w2_ffn1, g_mix, w_in, b_in, attn_sinks, gm_ln_g, gm_ln_b, gm_w_s, gm_b_s, w_out, b_out, g_ffn2, w1_ffn2, w3_ffn2, w2_ffn2, g_ple, w_ple_gate, w_ple_proj, g_final):
    depth = g_ffn1.shape[0]
    assert depth == 1
    i = 0
    nb, seq, _ = x_prompt.shape
    nd = x_sample.shape[0]

    row = lambda a: a.reshape(1, -1)
    g1, gmx, g2, gp, gf = row(g_ffn1[i]), row(g_mix[i]), row(g_ffn2[i]), row(g_ple[i]), row(g_final)
    ln_g, ln_b = row(gm_ln_g[i]), row(gm_ln_b[i])
    sinks = attn_sinks[i]
    out_weights = (w_out[i], w1_ffn2[i], w3_ffn2[i], w2_ffn2[i], w_ple_gate[i], w_ple_proj[i])
    x1, mix, kv_last, gmv_p, x1s, q8, kvs, uvgs, wo, w1b, w3b, w2b, wg, wp = _in_mix(
        x_prompt, x_sample.reshape(nd, D_MODEL), g1, w1_ffn1[i], w3_ffn1[i], w2_ffn1[i], gmx,
        w_in[i], row(b_in[i]), sinks, ln_g, ln_b, gm_w_s[i], gm_b_s[i].T,
        out_weights, PROMPT_TILE)
    k_win_p = kv_last[:, :, :KV_WIDTH].reshape(1, nb, WINDOW, N_KV_HEADS, HEAD_DIM)
    v_win_p = kv_last[:, :, KV_WIDTH:].reshape(1, nb, WINDOW, N_KV_HEADS, HEAD_DIM)

    def position_minor(c):
        return jnp.transpose(c, (0, 2, 3, 1)).reshape(nd, KV_WIDTH, WINDOW)

    def position_major(c):
        c = jnp.transpose(c.reshape(nd, N_KV_HEADS, HEAD_DIM, WINDOW), (0, 3, 1, 2))
        return c.reshape(1, nd, WINDOW, N_KV_HEADS, HEAD_DIM)

    kvs3 = kvs.reshape(nd, 1, 2 * KV_WIDTH)
    o8, gm_s, gmv_s, kw_s, vw_s = _mix_sample(
        q8.reshape(nd, N_Q_HEADS, LANES), kvs3[:, :, :KV_WIDTH], kvs3[:, :, KV_WIDTH:], kvs,
        position_minor(cache_k_win[i]), position_minor(cache_v_win[i]),
        sinks.reshape(N_Q_HEADS, 1), uvgs, ln_g, ln_b,
        jnp.repeat(gm_w_s[i][:, 0, 0], GM_DIM).reshape(1, GM_WIDTH),
        jnp.repeat(gm_b_s[i][:, 0], GM_DIM).reshape(1, GM_WIDTH), SAMPLE_BLOCK)
    mix_s = jnp.concatenate([o8.reshape(nd, N_Q_HEADS * LANES), gm_s], axis=-1)

    y_prompt, y_sample = _out_stage(
        x1, mix, p_prompt[i].reshape(nb * seq, PLE_DIM), x1s, mix_s,
        p_sample[i].reshape(nd, PLE_DIM), wo, row(b_out[i]),
        g2, w1b, w3b, w2b, gp, wg, wp, gf, 2 * PROMPT_TILE)

    return (y_prompt.reshape(nb, seq, D_MODEL), y_sample.reshape(nd, 1, D_MODEL), k_win_p, v_win_p,
            position_major(kw_s), position_major(vw_s),
            gmv_p.reshape(1, nb, CHUNK, GM_WIDTH), gmv_s.reshape(1, nd, 1, GM_WIDTH))
```

```python
import functools
import math

import jax
import jax.numpy as jnp
from jax import lax
from jax.experimental import pallas as pl
from jax.experimental.pallas import tpu as pltpu

F32 = jnp.float32
BF16 = jnp.bfloat16

D_MODEL = 1024
HEAD_DIM = 64
N_Q_HEADS = 8
N_KV_HEADS = 2
ATTN_WIDTH = N_Q_HEADS * HEAD_DIM
KV_WIDTH = N_KV_HEADS * HEAD_DIM
WINDOW = 128
GM_WIDTH = 512
GM_GROUPS = 4
GM_DIM = 128
CHUNK = 128
D_FF = 2816
PLE_DIM = 256
RMS_EPS = 1e-6
LN_EPS = 1e-5
ATTN_SCALE = HEAD_DIM ** -0.5

LANES = 128
FF_CHUNK = 768
FF_BOUNDS = tuple((lo, min(lo + FF_CHUNK, D_FF)) for lo in range(0, D_FF, FF_CHUNK))
N_FF_CHUNKS = len(FF_BOUNDS)
VMEM_LIMIT_BYTES = 62 * 1024 * 1024

_GELU_C = math.sqrt(2.0 / math.pi)


def _rms(x, g):
    return x * lax.rsqrt(jnp.mean(x * x, axis=-1, keepdims=True) + RMS_EPS) * g


def _sigmoid(x):
    return 1.0 / (1.0 + jnp.exp(-x))


def _gelu(x):
    return 0.5 * x * (1.0 + jnp.tanh(_GELU_C * (x + 0.044715 * (x * x * x))))


def _dot(a, b):
    return jnp.dot(a, b, preferred_element_type=F32)


def _swiglu(h, w1_ref, w3_ref, w2_ref, before_chunk=None):
    acc = jnp.zeros((h.shape[0], D_MODEL), F32)
    for c, (lo, hi) in enumerate(FF_BOUNDS):
        if before_chunk is not None:
            before_chunk(c)
        cols = slice(lo, hi)
        a = _dot(h, w1_ref[:, cols])
        b = _dot(h, w3_ref[:, cols])
        act = (a * _sigmoid(a) * b).astype(BF16)
        acc = acc + _dot(act, w2_ref[cols, :])
    return acc


def _resident(shape):
    zeros = (0,) * len(shape)
    return pl.BlockSpec(shape, lambda *_: zeros, pipeline_mode=pl.Buffered(1))


CAST_STEPS = 4
CAST_BUFFERS = 1
BF16_SUBLANES = 16


def _cast_spec(w):
    rows = w.shape[0] // CAST_STEPS
    assert rows * CAST_STEPS == w.shape[0] and rows % BF16_SUBLANES == 0
    return pl.BlockSpec((rows, w.shape[1]), lambda s: (jnp.minimum(s, CAST_STEPS - 1), 0),
                        pipeline_mode=pl.Buffered(CAST_BUFFERS))


def _trickle_rows(w, max_chunks):
    rows = BF16_SUBLANES
    while w.shape[0] % rows or w.shape[0] // rows > max_chunks:
        rows += BF16_SUBLANES
    return rows


def _trickle_spec(w, max_chunks):
    rows = _trickle_rows(w, max_chunks)
    last = w.shape[0] // rows - 1
    return pl.BlockSpec((rows, w.shape[1]), lambda s: (jnp.clip(s - CAST_STEPS, 0, last), 0))


def _cast_chunk(s, src_ref, dst_ref):
    rows = src_ref.shape[0]
    start = pl.multiple_of(s * rows, BF16_SUBLANES)
    dst_ref[pl.ds(start, rows), :] = src_ref[...].astype(BF16)


def _bf16_scratch(w):
    return pltpu.VMEM(w.shape, BF16)


def _gm_norm(vg, ln_g, ln_b):
    a = _gelu(vg)
    mu = jnp.mean(a, axis=-1, keepdims=True)
    ac = a - mu
    return ac * lax.rsqrt(jnp.mean(ac * ac, axis=-1, keepdims=True) + LN_EPS) * ln_g + ln_b


def _mix_units(blocks, first_of_seq, sinks_ref, q_ref, kv_ref, kvp_ref, uvg_ref, lng_ref,
               lnb_ref, ws_ref, bst_ref, mix_ref, gmv_ref):
    lane = lax.broadcasted_iota(jnp.int32, (2 * WINDOW, LANES), 1)
    lo = lane < HEAD_DIM
    qi = lax.broadcasted_iota(jnp.int32, (WINDOW, 2 * WINDOW), 0)
    kj = lax.broadcasted_iota(jnp.int32, (WINDOW, 2 * WINDOW), 1)
    band = (kj >= qi) & (kj <= qi + WINDOW)
    ti = lax.broadcasted_iota(jnp.int32, (CHUNK, CHUNK), 0)
    si = lax.broadcasted_iota(jnp.int32, (CHUNK, CHUNK), 1)
    tril = si <= ti

    block_cache = {}

    def block_operands(blk):
        if blk not in block_cache:
            if blk == 0:
                kvp = jnp.where(first_of_seq, 0.0, kvp_ref[...])
                kv2 = jnp.concatenate([kvp, kv_ref[:WINDOW, :]], axis=0)
                mask = band & (kj >= jnp.where(first_of_seq, WINDOW, 0))
            else:
                kv2 = kv_ref[(blk - 1) * WINDOW:(blk + 1) * WINDOW, :]
                mask = band
            kt = kv2[:, :KV_WIDTH].T.astype(BF16)
            v2 = kv2[:, KV_WIDTH:].astype(BF16)
            v2r = pltpu.roll(v2, HEAD_DIM, 1)
            zero_v = jnp.zeros_like(v2)
            zero_k = jnp.zeros((HEAD_DIM, 2 * WINDOW), BF16)
            heads = [kt[h * HEAD_DIM:(h + 1) * HEAD_DIM, :] for h in range(N_KV_HEADS)]
            k_lo = [jnp.concatenate([kh, zero_k], axis=0) for kh in heads]
            k_hi = [jnp.concatenate([zero_k, kh], axis=0) for kh in heads]
            v_lo = (jnp.where(lo, v2, zero_v), jnp.where(lo, v2r, zero_v))
            v_hi = (jnp.where(lo, zero_v, v2r), jnp.where(lo, zero_v, v2))
            mask2 = jnp.concatenate([mask, mask], axis=0)
            block_cache[blk] = (mask2, k_lo, k_hi, v_lo, v_hi)
        return block_cache[blk]

    upper = lax.broadcasted_iota(jnp.int32, (2 * WINDOW, 1), 0) < WINDOW

    def attn_unit(blk, h):
        rows = slice(blk * WINDOW, (blk + 1) * WINDOW)
        mask2, k_lo, k_hi, v_lo, v_hi = block_operands(blk)
        slots = (2 * h, 2 * h + 1)
        q2 = jnp.concatenate([q_ref[rows, s * LANES:(s + 1) * LANES] for s in slots], axis=0)
        scores = [_dot(q2, kk) for kk in (k_lo[h], k_hi[h])]
        yield
        out = None
        for half, vv in enumerate((v_lo[h], v_hi[h])):
            sink = jnp.where(upper, sinks_ref[2 * slots[0] + half], sinks_ref[2 * slots[1] + half])
            s = jnp.where(mask2, scores[half], -jnp.inf)
            m = jnp.maximum(jnp.max(s, axis=-1, keepdims=True), sink)
            e = jnp.exp(s - m)
            denom = jnp.sum(e, axis=-1, keepdims=True) + jnp.exp(sink - m)
            p = (e * (1.0 / denom)).astype(BF16)
            o = _dot(p, vv)
            out = o if out is None else out + o
        out = out.astype(BF16)
        mix_ref[rows, slots[0] * LANES:(slots[0] + 1) * LANES] = out[:WINDOW]
        mix_ref[rows, slots[1] * LANES:(slots[1] + 1) * LANES] = out[WINDOW:]

    def gm_unit(blk0):
        pair = (blk0, blk0 + 1)
        rows = [slice(blk * WINDOW, (blk + 1) * WINDOW) for blk in pair]
        vgn = [_gm_norm(uvg_ref[r, GM_WIDTH:], lng_ref[...], lnb_ref[...]) for r in rows]
        if pair[1] == blocks - 1:
            gmv_ref[...] = vgn[1]
        vgb = [v.astype(BF16) for v in vgn]
        gu = [_gelu(uvg_ref[r, :GM_WIDTH]) for r in rows]
        yield
        for g in range(GM_GROUPS):
            cols = slice(g * GM_DIM, (g + 1) * GM_DIM)
            w = jnp.where(tril, ws_ref[g], 0.0).astype(BF16)
            sm = _dot(w, jnp.concatenate([v[:, cols] for v in vgb], axis=1)) + bst_ref[:, g:g + 1]
            for i, r in enumerate(rows):
                mix_ref[r, ATTN_WIDTH + g * GM_DIM:ATTN_WIDTH + (g + 1) * GM_DIM] = (
                    gu[i][:, cols] * sm[:, i * GM_DIM:(i + 1) * GM_DIM]).astype(BF16)

    assert blocks % 2 == 0
    units = []
    for blk0 in range(0, blocks, 2):
        units += [attn_unit(blk, h) for blk in (blk0, blk0 + 1) for h in range(N_KV_HEADS)]
        units.append(gm_unit(blk0))
    return units


KV_COL = ATTN_WIDTH
UVG_COL = ATTN_WIDTH + 2 * KV_WIDTH


def _heads_to_slots(q):
    lo = lax.broadcasted_iota(jnp.int32, (q.shape[0], LANES), 1) < HEAD_DIM
    slots = []
    for r in range(N_Q_HEADS):
        src = q[:, (r // 2) * LANES:(r // 2 + 1) * LANES]
        if r % 2 != r // (N_Q_HEADS // N_KV_HEADS):
            src = pltpu.roll(src, HEAD_DIM, 1)
        keep = lo if r // (N_Q_HEADS // N_KV_HEADS) == 0 else ~lo
        slots.append(jnp.where(keep, src, jnp.zeros_like(src)))
    return jnp.concatenate(slots, axis=1)


def _slots_to_heads(o):
    lo = lax.broadcasted_iota(jnp.int32, (o.shape[0], LANES), 1) < HEAD_DIM
    chunks = []
    for j in range(N_Q_HEADS // 2):
        halves = []
        for half, r in enumerate((2 * j, 2 * j + 1)):
            src = o[:, r * LANES:(r + 1) * LANES]
            if r // (N_Q_HEADS // N_KV_HEADS) != half:
                src = pltpu.roll(src, HEAD_DIM, 1)
            halves.append(src)
        chunks.append(jnp.where(lo, halves[0], halves[1]))
    return jnp.concatenate(chunks, axis=1)


N_NEXT = 6


def _in_mix_kernel(tiles_per_seq, blocks, n_tiles,
                   x_ref, g1_ref, w1c_ref, w3c_ref, w2c_ref, gm_ref, winc_ref, bin_ref,
                   sinks_ref, lng_ref, lnb_ref, ws_ref, bst_ref, xs_ref, *rest):
    next_f32, rest = rest[:N_NEXT], rest[N_NEXT:]
    (x1_ref, mix_ref, kvw_ref, gmv_ref, x1s_ref, q8s_ref, kvs_ref, uvgs_ref), rest = (
        rest[:8], rest[8:])
    next_bf16, rest = rest[:N_NEXT], rest[N_NEXT:]
    w1_s, w3_s, w2_s, win_s, q_s, kv_s, uvg_s, kvp_s = rest
    s = pl.program_id(0)
    t = s - CAST_STEPS

    @pl.when(s < CAST_STEPS)
    def _():
        for src, dst in ((w1c_ref, w1_s), (w3c_ref, w3_s), (w2c_ref, w2_s), (winc_ref, win_s)):
            _cast_chunk(s, src, dst)

    @pl.when(t == 0)
    def _():
        q_s[...] = jnp.zeros_like(q_s)
        kv_s[...] = jnp.zeros_like(kv_s)
        uvg_s[...] = jnp.zeros_like(uvg_s)
        kvp_s[...] = jnp.zeros_like(kvp_s)

    def ffn(x, before_chunk=None):
        h = _rms(x, g1_ref[...]).astype(BF16)
        return x + 0.5 * _swiglu(h, w1_s, w3_s, w2_s, before_chunk)

    def project(h2, cols):
        return _dot(h2, win_s[:, cols]) + bin_ref[:, cols]

    def ffn_with_previous_mix(x):
        first_of_seq = (jnp.maximum(t - 1, 0) & (tiles_per_seq - 1)) == 0
        units = _mix_units(blocks, first_of_seq, sinks_ref, q_s, kv_s, kvp_s, uvg_s, lng_ref,
                           lnb_ref, ws_ref, bst_ref, mix_ref, gmv_ref)
        in_flight = []

        def before_chunk(c):
            for unit in in_flight:
                next(unit, None)
            first, last = c * len(units) // N_FF_CHUNKS, (c + 1) * len(units) // N_FF_CHUNKS
            in_flight[:] = units[first:last]
            for unit in in_flight:
                next(unit)

        x1 = ffn(x, before_chunk)
        for unit in in_flight:
            next(unit, None)
        return x1

    @pl.when((t >= 0) & (t < n_tiles))
    def _():
        for src, dst in zip(next_f32, next_bf16):
            dst[...] = src[...].astype(BF16)
        x1 = ffn_with_previous_mix(x_ref[...])
        kvp_s[...] = kv_s[(blocks - 1) * WINDOW:, :]
        x1_ref[...] = x1
        h2 = _rms(x1, gm_ref[...]).astype(BF16)
        q_s[...] = (project(h2, slice(0, KV_COL)) * ATTN_SCALE).astype(BF16)
        kv = project(h2, slice(KV_COL, UVG_COL))
        kv_s[...] = kv
        kvw_ref[...] = kv[(blocks - 1) * WINDOW:, :]
        uvg_s[...] = project(h2, slice(UVG_COL, None))

    @pl.when(t == n_tiles)
    def _():
        x1 = ffn_with_previous_mix(xs_ref[...])
        x1s_ref[...] = x1
        h2 = _rms(x1, gm_ref[...]).astype(BF16)
        q8s_ref[...] = _heads_to_slots((project(h2, slice(0, KV_COL)) * ATTN_SCALE).astype(BF16))
        kvs_ref[...] = project(h2, slice(KV_COL, UVG_COL))
        uvgs_ref[...] = project(h2, slice(UVG_COL, None))


def _in_mix(x, xs, g1, w1, w3, w2, gmix, w_in, b_in, sinks, ln_g, ln_b, w_s, b_st,
            next_weights, tm):
    assert len(next_weights) == N_NEXT
    b, s, _ = x.shape
    n = xs.shape[0]
    nj = s // tm
    nt = b * nj
    blocks = tm // WINDOW
    seq_shift = nj.bit_length() - 1
    assert nj == 1 << seq_shift
    x = x.reshape(b * s, D_MODEL)

    def cur(step):
        return jnp.clip(step - CAST_STEPS, 0, nt - 1)

    def prev(step):
        return jnp.clip(step - CAST_STEPS - 1, 0, nt - 1)

    def tile(width, which):
        return pl.BlockSpec((tm, width), lambda step: (which(step), 0))

    def per_seq(rows, width, which):
        return pl.BlockSpec((None, rows, width),
                            lambda step: (lax.shift_right_logical(which(step), seq_shift), 0, 0))

    def whole(shape):
        return pl.BlockSpec(shape, lambda step: (0,) * len(shape))

    gconsts = (ln_g, ln_b, w_s, b_st)
    sample_out = ((n, D_MODEL), F32), ((n, N_Q_HEADS * LANES), BF16), ((n, 2 * KV_WIDTH), F32), (
        (n, 2 * GM_WIDTH), F32)
    return pl.pallas_call(
        functools.partial(_in_mix_kernel, nj, blocks, nt),
        out_shape=(jax.ShapeDtypeStruct((b * s, D_MODEL), F32),
                   jax.ShapeDtypeStruct((b * s, D_MODEL), BF16),
                   jax.ShapeDtypeStruct((b, WINDOW, 2 * KV_WIDTH), F32),
                   jax.ShapeDtypeStruct((b, CHUNK, GM_WIDTH), F32))
        + tuple(jax.ShapeDtypeStruct(shape, dtype) for shape, dtype in sample_out)
        + tuple(jax.ShapeDtypeStruct(w.shape, BF16) for w in next_weights),
        grid=(CAST_STEPS + nt + 1,),
        in_specs=[tile(D_MODEL, cur), _resident(g1.shape), _cast_spec(w1), _cast_spec(w3),
                  _cast_spec(w2), _resident(gmix.shape), _cast_spec(w_in), _resident(b_in.shape),
                  pl.BlockSpec(memory_space=pltpu.SMEM)]
        + [_resident(c.shape) for c in gconsts + (xs,)]
        + [_trickle_spec(w, nt) for w in next_weights],
        out_specs=(tile(D_MODEL, cur), tile(D_MODEL, prev),
                   per_seq(WINDOW, 2 * KV_WIDTH, cur), per_seq(CHUNK, GM_WIDTH, prev))
        + tuple(whole(shape) for shape, _ in sample_out)
        + tuple(_trickle_spec(w, nt) for w in next_weights),
        scratch_shapes=[_bf16_scratch(w1), _bf16_scratch(w3), _bf16_scratch(w2),
                        _bf16_scratch(w_in),
                        pltpu.VMEM((tm, ATTN_WIDTH), BF16), pltpu.VMEM((tm, 2 * KV_WIDTH), F32),
                        pltpu.VMEM((tm, 2 * GM_WIDTH), F32),
                        pltpu.VMEM((WINDOW, 2 * KV_WIDTH), F32)],
        compiler_params=pltpu.CompilerParams(
            dimension_semantics=("arbitrary",), vmem_limit_bytes=VMEM_LIMIT_BYTES),
        name="in_mix",
    )(x, g1, w1, w3, w2, gmix, w_in, b_in, sinks, *gconsts, xs, *next_weights)


def _mix_sample_kernel(q8_ref, kn_ref, vn_ref, kvn_ref, kc_ref, vc_ref, sinks_ref, uvg_ref, lng_ref,
                       lnb_ref, w00_ref, b0_ref, o8_ref, gm_ref, gmv_ref, kw_ref, vw_ref):
    bb = q8_ref.shape[0]
    q8 = q8_ref[...]
    kc = kc_ref[...]
    vc = vc_ref[...]
    kn = kn_ref[...]
    vn = vn_ref[...]
    sink = sinks_ref[...]
    s_c = jnp.einsum("bqd,bdk->bqk", q8, kc.astype(BF16), preferred_element_type=F32)
    s_n = jnp.sum(q8.astype(F32) * kn.astype(BF16).astype(F32), axis=-1, keepdims=True)
    m = jnp.maximum(jnp.maximum(jnp.max(s_c, axis=-1, keepdims=True), s_n), sink)
    e_c = jnp.exp(s_c - m)
    e_n = jnp.exp(s_n - m)
    inv = 1.0 / (jnp.sum(e_c, axis=-1, keepdims=True) + e_n + jnp.exp(sink - m))
    p_c = (e_c * inv).astype(BF16)
    p_n = (e_n * inv).astype(BF16).astype(F32)
    o = jnp.einsum("bqk,bdk->bqd", p_c, vc.astype(BF16), preferred_element_type=F32)
    o = o + p_n * vn.astype(BF16).astype(F32)
    row = lax.broadcasted_iota(jnp.int32, o.shape, 1)
    lane = lax.broadcasted_iota(jnp.int32, o.shape, 2)
    own = (lane < HEAD_DIM) == (row < N_Q_HEADS // N_KV_HEADS)
    o8_ref[...] = jnp.where(own, o, 0.0).astype(BF16)

    pad = jnp.zeros((LANES - bb, 2 * KV_WIDTH), F32)
    new_t = jnp.concatenate([kvn_ref[...], pad], axis=0).T
    last = lax.broadcasted_iota(jnp.int32, (KV_WIDTH, WINDOW), 1) == WINDOW - 1
    for j in range(bb):
        col = new_t[:, j:j + 1]
        kw_ref[j] = jnp.where(last, col[:KV_WIDTH], pltpu.roll(kc[j], WINDOW - 1, 1))
        vw_ref[j] = jnp.where(last, col[KV_WIDTH:], pltpu.roll(vc[j], WINDOW - 1, 1))

    vgn = _gm_norm(uvg_ref[:, GM_WIDTH:], lng_ref[...], lnb_ref[...])
    gmv_ref[...] = vgn
    sm = w00_ref[...].astype(BF16).astype(F32) * vgn.astype(BF16).astype(F32) + b0_ref[...]
    gm_ref[...] = (_gelu(uvg_ref[:, :GM_WIDTH]) * sm).astype(BF16)


def _mix_sample(q8, kn, vn, kvn, kc, vc, sinks, uvg, ln_g, ln_b, w00, b0, bb):
    n = q8.shape[0]

    def b3(d1, d2):
        return pl.BlockSpec((bb, d1, d2), lambda i: (i, 0, 0))

    def b2(d):
        return pl.BlockSpec((bb, d), lambda i: (i, 0))

    return pl.pallas_call(
        _mix_sample_kernel,
        out_shape=(jax.ShapeDtypeStruct((n, N_Q_HEADS, LANES), BF16),
                   jax.ShapeDtypeStruct((n, GM_WIDTH), BF16),
                   jax.ShapeDtypeStruct((n, GM_WIDTH), F32),
                   jax.ShapeDtypeStruct((n, KV_WIDTH, WINDOW), F32),
                   jax.ShapeDtypeStruct((n, KV_WIDTH, WINDOW), F32)),
        grid=(n // bb,),
        in_specs=[b3(N_Q_HEADS, LANES), b3(1, KV_WIDTH), b3(1, KV_WIDTH), b2(2 * KV_WIDTH),
                  b3(KV_WIDTH, WINDOW), b3(KV_WIDTH, WINDOW),
                  _resident(sinks.shape), b2(2 * GM_WIDTH), _resident(ln_g.shape),
                  _resident(ln_b.shape), _resident(w00.shape), _resident(b0.shape)],
        out_specs=(b3(N_Q_HEADS, LANES), b2(GM_WIDTH), b2(GM_WIDTH),
                   b3(KV_WIDTH, WINDOW), b3(KV_WIDTH, WINDOW)),
        compiler_params=pltpu.CompilerParams(
            dimension_semantics=("arbitrary",), vmem_limit_bytes=VMEM_LIMIT_BYTES),
        name="mix_sample",
    )(q8, kn, vn, kvn, kc, vc, sinks, uvg, ln_g, ln_b, w00, b0)


def _out_stage_kernel(n_tiles, x1_ref, mix_ref, p_ref, wo_s, bo_ref, g2_ref, w1_s, w3_s, w2_s,
                      gp_ref, wg_s, wp_s, gf_ref, x1s_ref, mixs_ref, ps_ref,
                      y_ref, ys_ref):
    t = pl.program_id(0)

    def out_stage(x1, mix, p):
        x2 = x1 + _dot(mix, wo_s[...]) + bo_ref[...]
        h = _rms(x2, g2_ref[...]).astype(BF16)
        x3 = x2 + 0.5 * _swiglu(h, w1_s, w3_s, w2_s)
        gate = _sigmoid(_dot(_rms(x3, gp_ref[...]).astype(BF16), wg_s[...]))
        x4 = x3 + gate * _dot(p.astype(BF16), wp_s[...])
        return _rms(x4, gf_ref[...])

    @pl.when(t < n_tiles)
    def _():
        y_ref[...] = out_stage(x1_ref[...], mix_ref[...], p_ref[...])

    @pl.when(t == n_tiles)
    def _():
        slots = N_Q_HEADS * LANES
        mix = jnp.concatenate([_slots_to_heads(mixs_ref[:, :slots]), mixs_ref[:, slots:]], axis=1)
        ys_ref[...] = out_stage(x1s_ref[...], mix, ps_ref[...])


def _out_stage(x1, mix, p, x1s, mixs, ps, wo, bo, g2, w1, w3, w2, gp, wg, wp, gf, tm):
    r = x1.shape[0]
    n = x1s.shape[0]
    nt = r // tm

    def tile(width):
        return pl.BlockSpec((tm, width), lambda step: (jnp.minimum(step, nt - 1), 0))

    consts = (wo, bo, g2, w1, w3, w2, gp, wg, wp, gf, x1s, mixs, ps)
    return pl.pallas_call(
        functools.partial(_out_stage_kernel, nt),
        out_shape=(jax.ShapeDtypeStruct((r, D_MODEL), F32),
                   jax.ShapeDtypeStruct((n, D_MODEL), F32)),
        grid=(nt + 1,),
        in_specs=[tile(D_MODEL), tile(D_MODEL), tile(PLE_DIM)]
        + [_resident(c.shape) for c in consts],
        out_specs=(tile(D_MODEL), pl.BlockSpec((n, D_MODEL), lambda step: (0, 0))),
        compiler_params=pltpu.CompilerParams(
            dimension_semantics=("arbitrary",), vmem_limit_bytes=VMEM_LIMIT_BYTES),
        name="out_stage",
    )(x1, mix, p, wo, bo, g2, w1, w3, w2, gp, wg, wp, gf, x1s, mixs, ps)


PROMPT_TILE = 512
SAMPLE_BLOCK = 32


def kernel(x_prompt, x_sample, cache_k_win, cache_v_win, p_prompt, p_sample, g_ffn1, w1_ffn1, w3_ffn1, w2_ffn1, g_mix, w_in, b_in, attn_sinks, gm_ln_g, gm_ln_b, gm_w_s, gm_b_s, w_out, b_out, g_ffn2, w1_ffn2, w3_ffn2, w2_ffn2, g_ple, w_ple_gate, w_ple_proj, g_final):
    depth = g_ffn1.shape[0]
    assert depth == 1
    i = 0
    nb, seq, _ = x_prompt.shape
    nd = x_sample.shape[0]

    row = lambda a: a.reshape(1, -1)
    g1, gmx, g2, gp, gf = row(g_ffn1[i]), row(g_mix[i]), row(g_ffn2[i]), row(g_ple[i]), row(g_final)
    ln_g, ln_b = row(gm_ln_g[i]), row(gm_ln_b[i])
    sinks = attn_sinks[i]
    out_weights = (w_out[i], w1_ffn2[i], w3_ffn2[i], w2_ffn2[i], w_ple_gate[i], w_ple_proj[i])
    x1, mix, kv_last, gmv_p, x1s, q8, kvs, uvgs, wo, w1b, w3b, w2b, wg, wp = _in_mix(
        x_prompt, x_sample.reshape(nd, D_MODEL), g1, w1_ffn1[i], w3_ffn1[i], w2_ffn1[i], gmx,
        w_in[i], row(b_in[i]), sinks, ln_g, ln_b, gm_w_s[i], gm_b_s[i].T,
        out_weights, PROMPT_TILE)
    k_win_p = kv_last[:, :, :KV_WIDTH].reshape(1, nb, WINDOW, N_KV_HEADS, HEAD_DIM)
    v_win_p = kv_last[:, :, KV_WIDTH:].reshape(1, nb, WINDOW, N_KV_HEADS, HEAD_DIM)

    def position_minor(c):
        return jnp.transpose(c, (0, 2, 3, 1)).reshape(nd, KV_WIDTH, WINDOW)

    def position_major(c):
        c = jnp.transpose(c.reshape(nd, N_KV_HEADS, HEAD_DIM, WINDOW), (0, 3, 1, 2))
        return c.reshape(1, nd, WINDOW, N_KV_HEADS, HEAD_DIM)

    kvs3 = kvs.reshape(nd, 1, 2 * KV_WIDTH)
    o8, gm_s, gmv_s, kw_s, vw_s = _mix_sample(
        q8.reshape(nd, N_Q_HEADS, LANES), kvs3[:, :, :KV_WIDTH], kvs3[:, :, KV_WIDTH:], kvs,
        position_minor(cache_k_win[i]), position_minor(cache_v_win[i]),
        sinks.reshape(N_Q_HEADS, 1), uvgs, ln_g, ln_b,
        jnp.repeat(gm_w_s[i][:, 0, 0], GM_DIM).reshape(1, GM_WIDTH),
        jnp.repeat(gm_b_s[i][:, 0], GM_DIM).reshape(1, GM_WIDTH), SAMPLE_BLOCK)
    mix_s = jnp.concatenate([o8.reshape(nd, N_Q_HEADS * LANES), gm_s], axis=-1)

    y_prompt, y_sample = _out_stage(
        x1, mix, p_prompt[i].reshape(nb * seq, PLE_DIM), x1s, mix_s,
        p_sample[i].reshape(nd, PLE_DIM), wo, row(b_out[i]),
        g2, w1b, w3b, w2b, gp, wg, wp, gf, 2 * PROMPT_TILE)

    return (y_prompt.reshape(nb, seq, D_MODEL), y_sample.reshape(nd, 1, D_MODEL), k_win_p, v_win_p,
            position_major(kw_s), position_major(vw_s),
            gmv_p.reshape(1, nb, CHUNK, GM_WIDTH), gmv_s.reshape(1, nd, 1, GM_WIDTH))
```

```python
import functools
import math

import jax
import jax.numpy as jnp
from jax import lax
from jax.experimental import pallas as pl
from jax.experimental.pallas import tpu as pltpu

F32 = jnp.float32
BF16 = jnp.bfloat16

D_MODEL = 1024
HEAD_DIM = 64
N_Q_HEADS = 8
N_KV_HEADS = 2
ATTN_WIDTH = N_Q_HEADS * HEAD_DIM
KV_WIDTH = N_KV_HEADS * HEAD_DIM
WINDOW = 128
GM_WIDTH = 512
GM_GROUPS = 4
GM_DIM = 128
CHUNK = 128
D_FF = 2816
PLE_DIM = 256
RMS_EPS = 1e-6
LN_EPS = 1e-5
ATTN_SCALE = HEAD_DIM ** -0.5

LANES = 128
FF_CHUNK = 512
FF_BOUNDS = tuple((lo, min(lo + FF_CHUNK, D_FF)) for lo in range(0, D_FF, FF_CHUNK))
N_FF_CHUNKS = len(FF_BOUNDS)
VMEM_LIMIT_BYTES = 62 * 1024 * 1024

_GELU_C = math.sqrt(2.0 / math.pi)


def _rms(x, g):
    return x * lax.rsqrt(jnp.mean(x * x, axis=-1, keepdims=True) + RMS_EPS) * g


def _sigmoid(x):
    return 1.0 / (1.0 + jnp.exp(-x))


def _gelu(x):
    return 0.5 * x * (1.0 + jnp.tanh(_GELU_C * (x + 0.044715 * (x * x * x))))


def _dot(a, b):
    return jnp.dot(a, b, preferred_element_type=F32)


def _swiglu(h, w1_ref, w3_ref, w2_ref, before_chunk=None):
    acc = jnp.zeros((h.shape[0], D_MODEL), F32)
    for c, (lo, hi) in enumerate(FF_BOUNDS):
        if before_chunk is not None:
            before_chunk(c)
        cols = slice(lo, hi)
        a = _dot(h, w1_ref[:, cols])
        b = _dot(h, w3_ref[:, cols])
        act = (a * _sigmoid(a) * b).astype(BF16)
        acc = acc + _dot(act, w2_ref[cols, :])
    return acc


def _resident(shape):
    zeros = (0,) * len(shape)
    return pl.BlockSpec(shape, lambda *_: zeros, pipeline_mode=pl.Buffered(1))


CAST_STEPS = 4
CAST_BUFFERS = 1
BF16_SUBLANES = 16


def _cast_spec(w):
    rows = w.shape[0] // CAST_STEPS
    assert rows * CAST_STEPS == w.shape[0] and rows % BF16_SUBLANES == 0
    return pl.BlockSpec((rows, w.shape[1]), lambda s: (jnp.minimum(s, CAST_STEPS - 1), 0),
                        pipeline_mode=pl.Buffered(CAST_BUFFERS))


def _trickle_rows(w, max_chunks):
    rows = BF16_SUBLANES
    while w.shape[0] % rows or w.shape[0] // rows > max_chunks:
        rows += BF16_SUBLANES
    return rows


def _trickle_spec(w, max_chunks):
    rows = _trickle_rows(w, max_chunks)
    last = w.shape[0] // rows - 1
    return pl.BlockSpec((rows, w.shape[1]), lambda s: (jnp.clip(s - CAST_STEPS, 0, last), 0))


def _cast_chunk(s, src_ref, dst_ref):
    rows = src_ref.shape[0]
    start = pl.multiple_of(s * rows, BF16_SUBLANES)
    dst_ref[pl.ds(start, rows), :] = src_ref[...].astype(BF16)


def _bf16_scratch(w):
    return pltpu.VMEM(w.shape, BF16)


def _gm_norm(vg, ln_g, ln_b):
    a = _gelu(vg)
    mu = jnp.mean(a, axis=-1, keepdims=True)
    ac = a - mu
    return ac * lax.rsqrt(jnp.mean(ac * ac, axis=-1, keepdims=True) + LN_EPS) * ln_g + ln_b


def _mix_units(blocks, first_of_seq, sinks_ref, q_ref, kv_ref, kvp_ref, uvg_ref, lng_ref,
               lnb_ref, ws_ref, bst_ref, mix_ref, gmv_ref):
    lane = lax.broadcasted_iota(jnp.int32, (2 * WINDOW, LANES), 1)
    lo = lane < HEAD_DIM
    qi = lax.broadcasted_iota(jnp.int32, (WINDOW, 2 * WINDOW), 0)
    kj = lax.broadcasted_iota(jnp.int32, (WINDOW, 2 * WINDOW), 1)
    band = (kj >= qi) & (kj <= qi + WINDOW)
    ti = lax.broadcasted_iota(jnp.int32, (CHUNK, CHUNK), 0)
    si = lax.broadcasted_iota(jnp.int32, (CHUNK, CHUNK), 1)
    tril = si <= ti

    block_cache = {}

    def block_operands(blk):
        if blk not in block_cache:
            if blk == 0:
                kvp = jnp.where(first_of_seq, 0.0, kvp_ref[...])
                kv2 = jnp.concatenate([kvp, kv_ref[:WINDOW, :]], axis=0)
                mask = band & (kj >= jnp.where(first_of_seq, WINDOW, 0))
            else:
                kv2 = kv_ref[(blk - 1) * WINDOW:(blk + 1) * WINDOW, :]
                mask = band
            kt = kv2[:, :KV_WIDTH].T.astype(BF16)
            v2 = kv2[:, KV_WIDTH:].astype(BF16)
            v2r = pltpu.roll(v2, HEAD_DIM, 1)
            zero_v = jnp.zeros_like(v2)
            zero_k = jnp.zeros((HEAD_DIM, 2 * WINDOW), BF16)
            heads = [kt[h * HEAD_DIM:(h + 1) * HEAD_DIM, :] for h in range(N_KV_HEADS)]
            k_lo = [jnp.concatenate([kh, zero_k], axis=0) for kh in heads]
            k_hi = [jnp.concatenate([zero_k, kh], axis=0) for kh in heads]
            v_lo = (jnp.where(lo, v2, zero_v), jnp.where(lo, v2r, zero_v))
            v_hi = (jnp.where(lo, zero_v, v2r), jnp.where(lo, zero_v, v2))
            mask2 = jnp.concatenate([mask, mask], axis=0)
            block_cache[blk] = (mask2, k_lo, k_hi, v_lo, v_hi)
        return block_cache[blk]

    upper = lax.broadcasted_iota(jnp.int32, (2 * WINDOW, 1), 0) < WINDOW

    def attn_unit(blk, h):
        rows = slice(blk * WINDOW, (blk + 1) * WINDOW)
        mask2, k_lo, k_hi, v_lo, v_hi = block_operands(blk)
        slots = (2 * h, 2 * h + 1)
        q2 = jnp.concatenate([q_ref[rows, s * LANES:(s + 1) * LANES] for s in slots], axis=0)
        scores = [_dot(q2, kk) for kk in (k_lo[h], k_hi[h])]
        yield
        out = None
        for half, vv in enumerate((v_lo[h], v_hi[h])):
            sink = jnp.where(upper, sinks_ref[2 * slots[0] + half], sinks_ref[2 * slots[1] + half])
            s = jnp.where(mask2, scores[half], -jnp.inf)
            m = jnp.maximum(jnp.max(s, axis=-1, keepdims=True), sink)
            e = jnp.exp(s - m)
            denom = jnp.sum(e, axis=-1, keepdims=True) + jnp.exp(sink - m)
            p = (e * (1.0 / denom)).astype(BF16)
            o = _dot(p, vv)
            out = o if out is None else out + o
        out = out.astype(BF16)
        mix_ref[rows, slots[0] * LANES:(slots[0] + 1) * LANES] = out[:WINDOW]
        mix_ref[rows, slots[1] * LANES:(slots[1] + 1) * LANES] = out[WINDOW:]

    def gm_unit(blk0):
        pair = (blk0, blk0 + 1)
        rows = [slice(blk * WINDOW, (blk + 1) * WINDOW) for blk in pair]
        vgn = [_gm_norm(uvg_ref[r, GM_WIDTH:], lng_ref[...], lnb_ref[...]) for r in rows]
        if pair[1] == blocks - 1:
            gmv_ref[...] = vgn[1]
        vgb = [v.astype(BF16) for v in vgn]
        gu = [_gelu(uvg_ref[r, :GM_WIDTH]) for r in rows]
        yield
        for g in range(GM_GROUPS):
            cols = slice(g * GM_DIM, (g + 1) * GM_DIM)
            w = jnp.where(tril, ws_ref[g], 0.0).astype(BF16)
            sm = _dot(w, jnp.concatenate([v[:, cols] for v in vgb], axis=1)) + bst_ref[:, g:g + 1]
            for i, r in enumerate(rows):
                mix_ref[r, ATTN_WIDTH + g * GM_DIM:ATTN_WIDTH + (g + 1) * GM_DIM] = (
                    gu[i][:, cols] * sm[:, i * GM_DIM:(i + 1) * GM_DIM]).astype(BF16)

    assert blocks % 2 == 0
    units = []
    for blk0 in range(0, blocks, 2):
        units += [attn_unit(blk, h) for blk in (blk0, blk0 + 1) for h in range(N_KV_HEADS)]
        units.append(gm_unit(blk0))
    return units


KV_COL = ATTN_WIDTH
UVG_COL = ATTN_WIDTH + 2 * KV_WIDTH


def _heads_to_slots(q):
    lo = lax.broadcasted_iota(jnp.int32, (q.shape[0], LANES), 1) < HEAD_DIM
    slots = []
    for r in range(N_Q_HEADS):
        src = q[:, (r // 2) * LANES:(r // 2 + 1) * LANES]
        if r % 2 != r // (N_Q_HEADS // N_KV_HEADS):
            src = pltpu.roll(src, HEAD_DIM, 1)
        keep = lo if r // (N_Q_HEADS // N_KV_HEADS) == 0 else ~lo
        slots.append(jnp.where(keep, src, jnp.zeros_like(src)))
    return jnp.concatenate(slots, axis=1)


def _slots_to_heads(o):
    lo = lax.broadcasted_iota(jnp.int32, (o.shape[0], LANES), 1) < HEAD_DIM
    chunks = []
    for j in range(N_Q_HEADS // 2):
        halves = []
        for half, r in enumerate((2 * j, 2 * j + 1)):
            src = o[:, r * LANES:(r + 1) * LANES]
            if r // (N_Q_HEADS // N_KV_HEADS) != half:
                src = pltpu.roll(src, HEAD_DIM, 1)
            halves.append(src)
        chunks.append(jnp.where(lo, halves[0], halves[1]))
    return jnp.concatenate(chunks, axis=1)


N_NEXT = 6


def _in_mix_kernel(tiles_per_seq, blocks, n_tiles,
                   x_ref, g1_ref, w1c_ref, w3c_ref, w2c_ref, gm_ref, winc_ref, bin_ref,
                   sinks_ref, lng_ref, lnb_ref, ws_ref, bst_ref, xs_ref, *rest):
    next_f32, rest = rest[:N_NEXT], rest[N_NEXT:]
    (x1_ref, mix_ref, kvw_ref, gmv_ref, x1s_ref, q8s_ref, kvs_ref, uvgs_ref), rest = (
        rest[:8], rest[8:])
    next_bf16, rest = rest[:N_NEXT], rest[N_NEXT:]
    w1_s, w3_s, w2_s, win_s, q_s, kv_s, uvg_s, kvp_s = rest
    s = pl.program_id(0)
    t = s - CAST_STEPS

    @pl.when(s < CAST_STEPS)
    def _():
        for src, dst in ((w1c_ref, w1_s), (w3c_ref, w3_s), (w2c_ref, w2_s), (winc_ref, win_s)):
            _cast_chunk(s, src, dst)

    @pl.when(t == 0)
    def _():
        q_s[...] = jnp.zeros_like(q_s)
        kv_s[...] = jnp.zeros_like(kv_s)
        uvg_s[...] = jnp.zeros_like(uvg_s)
        kvp_s[...] = jnp.zeros_like(kvp_s)

    def ffn(x, before_chunk=None):
        h = _rms(x, g1_ref[...]).astype(BF16)
        return x + 0.5 * _swiglu(h, w1_s, w3_s, w2_s, before_chunk)

    def project(h2, cols):
        return _dot(h2, win_s[:, cols]) + bin_ref[:, cols]

    def ffn_with_previous_mix(x):
        first_of_seq = (jnp.maximum(t - 1, 0) & (tiles_per_seq - 1)) == 0
        units = _mix_units(blocks, first_of_seq, sinks_ref, q_s, kv_s, kvp_s, uvg_s, lng_ref,
                           lnb_ref, ws_ref, bst_ref, mix_ref, gmv_ref)
        in_flight = []

        def before_chunk(c):
            for unit in in_flight:
                next(unit, None)
            first, last = c * len(units) // N_FF_CHUNKS, (c + 1) * len(units) // N_FF_CHUNKS
            in_flight[:] = units[first:last]
            for unit in in_flight:
                next(unit)

        x1 = ffn(x, before_chunk)
        for unit in in_flight:
            next(unit, None)
        return x1

    @pl.when((t >= 0) & (t < n_tiles))
    def _():
        for src, dst in zip(next_f32, next_bf16):
            dst[...] = src[...].astype(BF16)
        x1 = ffn_with_previous_mix(x_ref[...])
        kvp_s[...] = kv_s[(blocks - 1) * WINDOW:, :]
        x1_ref[...] = x1
        h2 = _rms(x1, gm_ref[...]).astype(BF16)
        q_s[...] = (project(h2, slice(0, KV_COL)) * ATTN_SCALE).astype(BF16)
        kv = project(h2, slice(KV_COL, UVG_COL))
        kv_s[...] = kv
        kvw_ref[...] = kv[(blocks - 1) * WINDOW:, :]
        uvg_s[...] = project(h2, slice(UVG_COL, None))

    @pl.when(t == n_tiles)
    def _():
        x1 = ffn_with_previous_mix(xs_ref[...])
        x1s_ref[...] = x1
        h2 = _rms(x1, gm_ref[...]).astype(BF16)
        q8s_ref[...] = _heads_to_slots((project(h2, slice(0, KV_COL)) * ATTN_SCALE).astype(BF16))
        kvs_ref[...] = project(h2, slice(KV_COL, UVG_COL))
        uvgs_ref[...] = project(h2, slice(UVG_COL, None))


def _in_mix(x, xs, g1, w1, w3, w2, gmix, w_in, b_in, sinks, ln_g, ln_b, w_s, b_st,
            next_weights, tm):
    assert len(next_weights) == N_NEXT
    b, s, _ = x.shape
    n = xs.shape[0]
    nj = s // tm
    nt = b * nj
    blocks = tm // WINDOW
    seq_shift = nj.bit_length() - 1
    assert nj == 1 << seq_shift
    x = x.reshape(b * s, D_MODEL)

    def cur(step):
        return jnp.clip(step - CAST_STEPS, 0, nt - 1)

    def prev(step):
        return jnp.clip(step - CAST_STEPS - 1, 0, nt - 1)

    def tile(width, which):
        return pl.BlockSpec((tm, width), lambda step: (which(step), 0))

    def per_seq(rows, width, which):
        return pl.BlockSpec((None, rows, width),
                            lambda step: (lax.shift_right_logical(which(step), seq_shift), 0, 0))

    def whole(shape):
        return pl.BlockSpec(shape, lambda step: (0,) * len(shape))

    gconsts = (ln_g, ln_b, w_s, b_st)
    sample_out = ((n, D_MODEL), F32), ((n, N_Q_HEADS * LANES), BF16), ((n, 2 * KV_WIDTH), F32), (
        (n, 2 * GM_WIDTH), F32)
    return pl.pallas_call(
        functools.partial(_in_mix_kernel, nj, blocks, nt),
        out_shape=(jax.ShapeDtypeStruct((b * s, D_MODEL), F32),
                   jax.ShapeDtypeStruct((b * s, D_MODEL), BF16),
                   jax.ShapeDtypeStruct((b, WINDOW, 2 * KV_WIDTH), F32),
                   jax.ShapeDtypeStruct((b, CHUNK, GM_WIDTH), F32))
        + tuple(jax.ShapeDtypeStruct(shape, dtype) for shape, dtype in sample_out)
        + tuple(jax.ShapeDtypeStruct(w.shape, BF16) for w in next_weights),
        grid=(CAST_STEPS + nt + 1,),
        in_specs=[tile(D_MODEL, cur), _resident(g1.shape), _cast_spec(w1), _cast_spec(w3),
                  _cast_spec(w2), _resident(gmix.shape), _cast_spec(w_in), _resident(b_in.shape),
                  pl.BlockSpec(memory_space=pltpu.SMEM)]
        + [_resident(c.shape) for c in gconsts + (xs,)]
        + [_trickle_spec(w, nt) for w in next_weights],
        out_specs=(tile(D_MODEL, cur), tile(D_MODEL, prev),
                   per_seq(WINDOW, 2 * KV_WIDTH, cur), per_seq(CHUNK, GM_WIDTH, prev))
        + tuple(whole(shape) for shape, _ in sample_out)
        + tuple(_trickle_spec(w, nt) for w in next_weights),
        scratch_shapes=[_bf16_scratch(w1), _bf16_scratch(w3), _bf16_scratch(w2),
                        _bf16_scratch(w_in),
                        pltpu.VMEM((tm, ATTN_WIDTH), BF16), pltpu.VMEM((tm, 2 * KV_WIDTH), F32),
                        pltpu.VMEM((tm, 2 * GM_WIDTH), F32),
                        pltpu.VMEM((WINDOW, 2 * KV_WIDTH), F32)],
        compiler_params=pltpu.CompilerParams(
            dimension_semantics=("arbitrary",), vmem_limit_bytes=VMEM_LIMIT_BYTES),
        name="in_mix",
    )(x, g1, w1, w3, w2, gmix, w_in, b_in, sinks, *gconsts, xs, *next_weights)


def _mix_sample_kernel(q8_ref, kn_ref, vn_ref, kvn_ref, kc_ref, vc_ref, sinks_ref, uvg_ref, lng_ref,
                       lnb_ref, w00_ref, b0_ref, o8_ref, gm_ref, gmv_ref, kw_ref, vw_ref):
    bb = q8_ref.shape[0]
    q8 = q8_ref[...]
    kc = kc_ref[...]
    vc = vc_ref[...]
    kn = kn_ref[...]
    vn = vn_ref[...]
    sink = sinks_ref[...]
    s_c = jnp.einsum("bqd,bdk->bqk", q8, kc.astype(BF16), preferred_element_type=F32)
    s_n = jnp.sum(q8.astype(F32) * kn.astype(BF16).astype(F32), axis=-1, keepdims=True)
    m = jnp.maximum(jnp.maximum(jnp.max(s_c, axis=-1, keepdims=True), s_n), sink)
    e_c = jnp.exp(s_c - m)
    e_n = jnp.exp(s_n - m)
    inv = 1.0 / (jnp.sum(e_c, axis=-1, keepdims=True) + e_n + jnp.exp(sink - m))
    p_c = (e_c * inv).astype(BF16)
    p_n = (e_n * inv).astype(BF16).astype(F32)
    o = jnp.einsum("bqk,bdk->bqd", p_c, vc.astype(BF16), preferred_element_type=F32)
    o = o + p_n * vn.astype(BF16).astype(F32)
    row = lax.broadcasted_iota(jnp.int32, o.shape, 1)
    lane = lax.broadcasted_iota(jnp.int32, o.shape, 2)
    own = (lane < HEAD_DIM) == (row < N_Q_HEADS // N_KV_HEADS)
    o8_ref[...] = jnp.where(own, o, 0.0).astype(BF16)

    pad = jnp.zeros((LANES - bb, 2 * KV_WIDTH), F32)
    new_t = jnp.concatenate([kvn_ref[...], pad], axis=0).T
    last = lax.broadcasted_iota(jnp.int32, (KV_WIDTH, WINDOW), 1) == WINDOW - 1
    for j in range(bb):
        col = new_t[:, j:j + 1]
        kw_ref[j] = jnp.where(last, col[:KV_WIDTH], pltpu.roll(kc[j], WINDOW - 1, 1))
        vw_ref[j] = jnp.where(last, col[KV_WIDTH:], pltpu.roll(vc[j], WINDOW - 1, 1))

    vgn = _gm_norm(uvg_ref[:, GM_WIDTH:], lng_ref[...], lnb_ref[...])
    gmv_ref[...] = vgn
    sm = w00_ref[...].astype(BF16).astype(F32) * vgn.astype(BF16).astype(F32) + b0_ref[...]
    gm_ref[...] = (_gelu(uvg_ref[:, :GM_WIDTH]) * sm).astype(BF16)


def _mix_sample(q8, kn, vn, kvn, kc, vc, sinks, uvg, ln_g, ln_b, w00, b0, bb):
    n = q8.shape[0]

    def b3(d1, d2):
        return pl.BlockSpec((bb, d1, d2), lambda i: (i, 0, 0))

    def b2(d):
        return pl.BlockSpec((bb, d), lambda i: (i, 0))

    return pl.pallas_call(
        _mix_sample_kernel,
        out_shape=(jax.ShapeDtypeStruct((n, N_Q_HEADS, LANES), BF16),
                   jax.ShapeDtypeStruct((n, GM_WIDTH), BF16),
                   jax.ShapeDtypeStruct((n, GM_WIDTH), F32),
                   jax.ShapeDtypeStruct((n, KV_WIDTH, WINDOW), F32),
                   jax.ShapeDtypeStruct((n, KV_WIDTH, WINDOW), F32)),
        grid=(n // bb,),
        in_specs=[b3(N_Q_HEADS, LANES), b3(1, KV_WIDTH), b3(1, KV_WIDTH), b2(2 * KV_WIDTH),
                  b3(KV_WIDTH, WINDOW), b3(KV_WIDTH, WINDOW),
                  _resident(sinks.shape), b2(2 * GM_WIDTH), _resident(ln_g.shape),
                  _resident(ln_b.shape), _resident(w00.shape), _resident(b0.shape)],
        out_specs=(b3(N_Q_HEADS, LANES), b2(GM_WIDTH), b2(GM_WIDTH),
                   b3(KV_WIDTH, WINDOW), b3(KV_WIDTH, WINDOW)),
        compiler_params=pltpu.CompilerParams(
            dimension_semantics=("arbitrary",), vmem_limit_bytes=VMEM_LIMIT_BYTES),
        name="mix_sample",
    )(q8, kn, vn, kvn, kc, vc, sinks, uvg, ln_g, ln_b, w00, b0)


def _out_stage_kernel(n_tiles, x1_ref, mix_ref, p_ref, wo_s, bo_ref, g2_ref, w1_s, w3_s, w2_s,
                      gp_ref, wg_s, wp_s, gf_ref, x1s_ref, mixs_ref, ps_ref,
                      y_ref, ys_ref):
    t = pl.program_id(0)

    def head(x1, mix):
        x2 = x1 + _dot(mix, wo_s[...]) + bo_ref[...]
        return x2, _rms(x2, g2_ref[...]).astype(BF16)

    def finish(x3, p):
        gate = _sigmoid(_dot(_rms(x3, gp_ref[...]).astype(BF16), wg_s[...]))
        x4 = x3 + gate * _dot(p.astype(BF16), wp_s[...])
        return _rms(x4, gf_ref[...])

    def out_stage(x1, mix, p):
        x2, h = head(x1, mix)
        return finish(x2 + 0.5 * _swiglu(h, w1_s, w3_s, w2_s), p)

    @pl.when(t < n_tiles)
    def _():
        n = x1_ref.shape[0] // 2
        first, second = slice(0, n), slice(n, 2 * n)
        x2a, ha = head(x1_ref[first, :], mix_ref[first, :])
        x2b, hb = head(x1_ref[second, :], mix_ref[second, :])
        x3a = x2a + 0.5 * _swiglu(ha, w1_s, w3_s, w2_s)

        def before_chunk(c):
            if c == 1:
                y_ref[first, :] = finish(x3a, p_ref[first, :])

        x3b = x2b + 0.5 * _swiglu(hb, w1_s, w3_s, w2_s, before_chunk)
        y_ref[second, :] = finish(x3b, p_ref[second, :])

    @pl.when(t == n_tiles)
    def _():
        slots = N_Q_HEADS * LANES
        mix = jnp.concatenate([_slots_to_heads(mixs_ref[:, :slots]), mixs_ref[:, slots:]], axis=1)
        ys_ref[...] = out_stage(x1s_ref[...], mix, ps_ref[...])


def _out_stage(x1, mix, p, x1s, mixs, ps, wo, bo, g2, w1, w3, w2, gp, wg, wp, gf, tm):
    r = x1.shape[0]
    n = x1s.shape[0]
    nt = r // tm

    def tile(width):
        return pl.BlockSpec((tm, width), lambda step: (jnp.minimum(step, nt - 1), 0))

    consts = (wo, bo, g2, w1, w3, w2, gp, wg, wp, gf, x1s, mixs, ps)
    return pl.pallas_call(
        functools.partial(_out_stage_kernel, nt),
        out_shape=(jax.ShapeDtypeStruct((r, D_MODEL), F32),
                   jax.ShapeDtypeStruct((n, D_MODEL), F32)),
        grid=(nt + 1,),
        in_specs=[tile(D_MODEL), tile(D_MODEL), tile(PLE_DIM)]
        + [_resident(c.shape) for c in consts],
        out_specs=(tile(D_MODEL), pl.BlockSpec((n, D_MODEL), lambda step: (0, 0))),
        compiler_params=pltpu.CompilerParams(
            dimension_semantics=("arbitrary",), vmem_limit_bytes=VMEM_LIMIT_BYTES),
        name="out_stage",
    )(x1, mix, p, wo, bo, g2, w1, w3, w2, gp, wg, wp, gf, x1s, mixs, ps)


PROMPT_TILE = 512
SAMPLE_BLOCK = 32


def kernel(x_prompt, x_sample, cache_k_win, cache_v_win, p_prompt, p_sample, g_ffn1, w1_ffn1, w3_ffn1, w2_ffn1, g_mix, w_in, b_in, attn_sinks, gm_ln_g, gm_ln_b, gm_w_s, gm_b_s, w_out, b_out, g_ffn2, w1_ffn2, w3_ffn2, w2_ffn2, g_ple, w_ple_gate, w_ple_proj, g_final):
    depth = g_ffn1.shape[0]
    assert depth == 1
    i = 0
    nb, seq, _ = x_prompt.shape
    nd = x_sample.shape[0]

    row = lambda a: a.reshape(1, -1)
    g1, gmx, g2, gp, gf = row(g_ffn1[i]), row(g_mix[i]), row(g_ffn2[i]), row(g_ple[i]), row(g_final)
    ln_g, ln_b = row(gm_ln_g[i]), row(gm_ln_b[i])
    sinks = attn_sinks[i]
    out_weights = (w_out[i], w1_ffn2[i], w3_ffn2[i], w2_ffn2[i], w_ple_gate[i], w_ple_proj[i])
    x1, mix, kv_last, gmv_p, x1s, q8, kvs, uvgs, wo, w1b, w3b, w2b, wg, wp = _in_mix(
        x_prompt, x_sample.reshape(nd, D_MODEL), g1, w1_ffn1[i], w3_ffn1[i], w2_ffn1[i], gmx,
        w_in[i], row(b_in[i]), sinks, ln_g, ln_b, gm_w_s[i], gm_b_s[i].T,
        out_weights, PROMPT_TILE)
    k_win_p = kv_last[:, :, :KV_WIDTH].reshape(1, nb, WINDOW, N_KV_HEADS, HEAD_DIM)
    v_win_p = kv_last[:, :, KV_WIDTH:].reshape(1, nb, WINDOW, N_KV_HEADS, HEAD_DIM)

    def position_minor(c):
        return jnp.transpose(c, (0, 2, 3, 1)).reshape(nd, KV_WIDTH, WINDOW)

    def position_major(c):
        c = jnp.transpose(c.reshape(nd, N_KV_HEADS, HEAD_DIM, WINDOW), (0, 3, 1, 2))
        return c.reshape(1, nd, WINDOW, N_KV_HEADS, HEAD_DIM)

    kvs3 = kvs.reshape(nd, 1, 2 * KV_WIDTH)
    o8, gm_s, gmv_s, kw_s, vw_s = _mix_sample(
        q8.reshape(nd, N_Q_HEADS, LANES), kvs3[:, :, :KV_WIDTH], kvs3[:, :, KV_WIDTH:], kvs,
        position_minor(cache_k_win[i]), position_minor(cache_v_win[i]),
        sinks.reshape(N_Q_HEADS, 1), uvgs, ln_g, ln_b,
        jnp.repeat(gm_w_s[i][:, 0, 0], GM_DIM).reshape(1, GM_WIDTH),
        jnp.repeat(gm_b_s[i][:, 0], GM_DIM).reshape(1, GM_WIDTH), SAMPLE_BLOCK)
    mix_s = jnp.concatenate([o8.reshape(nd, N_Q_HEADS * LANES), gm_s], axis=-1)

    y_prompt, y_sample = _out_stage(
        x1, mix, p_prompt[i].reshape(nb * seq, PLE_DIM), x1s, mix_s,
        p_sample[i].reshape(nd, PLE_DIM), wo, row(b_out[i]),
        g2, w1b, w3b, w2b, gp, wg, wp, gf, 2 * PROMPT_TILE)

    return (y_prompt.reshape(nb, seq, D_MODEL), y_sample.reshape(nd, 1, D_MODEL), k_win_p, v_win_p,
            position_major(kw_s), position_major(vw_s),
            gmv_p.reshape(1, nb, CHUNK, GM_WIDTH), gmv_s.reshape(1, nd, 1, GM_WIDTH))
```

```python
import functools
import math

import jax
import jax.numpy as jnp
from jax import lax
from jax.experimental import pallas as pl
from jax.experimental.pallas import tpu as pltpu

F32 = jnp.float32
BF16 = jnp.bfloat16

D_MODEL = 1024
HEAD_DIM = 64
N_Q_HEADS = 8
N_KV_HEADS = 2
ATTN_WIDTH = N_Q_HEADS * HEAD_DIM
KV_WIDTH = N_KV_HEADS * HEAD_DIM
WINDOW = 128
GM_WIDTH = 512
GM_GROUPS = 4
GM_DIM = 128
CHUNK = 128
D_FF = 2816
PLE_DIM = 256
RMS_EPS = 1e-6
LN_EPS = 1e-5
ATTN_SCALE = HEAD_DIM ** -0.5

LANES = 128
FF_CHUNK = 512
FF_BOUNDS = tuple((lo, min(lo + FF_CHUNK, D_FF)) for lo in range(0, D_FF, FF_CHUNK))
N_FF_CHUNKS = len(FF_BOUNDS)
VMEM_LIMIT_BYTES = 62 * 1024 * 1024

_GELU_C = math.sqrt(2.0 / math.pi)


def _rms(x, g):
    return x * lax.rsqrt(jnp.mean(x * x, axis=-1, keepdims=True) + RMS_EPS) * g


def _sigmoid(x):
    return 1.0 / (1.0 + jnp.exp(-x))


def _gelu(x):
    return 0.5 * x * (1.0 + jnp.tanh(_GELU_C * (x + 0.044715 * (x * x * x))))


def _dot(a, b):
    return jnp.dot(a, b, preferred_element_type=F32)


def _swiglu(h, w1_ref, w3_ref, w2_ref, before_chunk=None):
    acc = jnp.zeros((h.shape[0], D_MODEL), F32)
    for c, (lo, hi) in enumerate(FF_BOUNDS):
        if before_chunk is not None:
            before_chunk(c)
        cols = slice(lo, hi)
        a = _dot(h, w1_ref[:, cols])
        b = _dot(h, w3_ref[:, cols])
        act = (a * _sigmoid(a) * b).astype(BF16)
        acc = acc + _dot(act, w2_ref[cols, :])
    return acc


def _resident(shape):
    zeros = (0,) * len(shape)
    return pl.BlockSpec(shape, lambda *_: zeros, pipeline_mode=pl.Buffered(1))


CAST_STEPS = 4
CAST_BUFFERS = 1
BF16_SUBLANES = 16


def _cast_spec(w):
    rows = w.shape[0] // CAST_STEPS
    assert rows * CAST_STEPS == w.shape[0] and rows % BF16_SUBLANES == 0
    return pl.BlockSpec((rows, w.shape[1]), lambda s: (jnp.minimum(s, CAST_STEPS - 1), 0),
                        pipeline_mode=pl.Buffered(CAST_BUFFERS))


def _trickle_rows(w, max_chunks):
    rows = BF16_SUBLANES
    while w.shape[0] % rows or w.shape[0] // rows > max_chunks:
        rows += BF16_SUBLANES
    return rows


def _trickle_spec(w, max_chunks):
    rows = _trickle_rows(w, max_chunks)
    last = w.shape[0] // rows - 1
    return pl.BlockSpec((rows, w.shape[1]), lambda s: (jnp.clip(s - CAST_STEPS, 0, last), 0))


def _cast_chunk(s, src_ref, dst_ref):
    rows = src_ref.shape[0]
    start = pl.multiple_of(s * rows, BF16_SUBLANES)
    dst_ref[pl.ds(start, rows), :] = src_ref[...].astype(BF16)


def _bf16_scratch(w):
    return pltpu.VMEM(w.shape, BF16)


def _gm_norm(vg, ln_g, ln_b):
    a = _gelu(vg)
    mu = jnp.mean(a, axis=-1, keepdims=True)
    ac = a - mu
    return ac * lax.rsqrt(jnp.mean(ac * ac, axis=-1, keepdims=True) + LN_EPS) * ln_g + ln_b


def _mix_units(blocks, first_of_seq, sinks_ref, q_ref, kv_ref, kvp_ref, uvg_ref, lng_ref,
               lnb_ref, ws_ref, bst_ref, mix_ref, gmv_ref):
    lane = lax.broadcasted_iota(jnp.int32, (2 * WINDOW, LANES), 1)
    lo = lane < HEAD_DIM
    qi = lax.broadcasted_iota(jnp.int32, (WINDOW, 2 * WINDOW), 0)
    kj = lax.broadcasted_iota(jnp.int32, (WINDOW, 2 * WINDOW), 1)
    band = (kj >= qi) & (kj <= qi + WINDOW)
    ti = lax.broadcasted_iota(jnp.int32, (CHUNK, CHUNK), 0)
    si = lax.broadcasted_iota(jnp.int32, (CHUNK, CHUNK), 1)
    tril = si <= ti

    block_cache = {}

    def block_operands(blk):
        if blk not in block_cache:
            if blk == 0:
                kvp = jnp.where(first_of_seq, 0.0, kvp_ref[...])
                kv2 = jnp.concatenate([kvp, kv_ref[:WINDOW, :]], axis=0)
                mask = band & (kj >= jnp.where(first_of_seq, WINDOW, 0))
            else:
                kv2 = kv_ref[(blk - 1) * WINDOW:(blk + 1) * WINDOW, :]
                mask = band
            kt = kv2[:, :KV_WIDTH].T.astype(BF16)
            v2 = kv2[:, KV_WIDTH:].astype(BF16)
            v2r = pltpu.roll(v2, HEAD_DIM, 1)
            zero_v = jnp.zeros_like(v2)
            zero_k = jnp.zeros((HEAD_DIM, 2 * WINDOW), BF16)
            heads = [kt[h * HEAD_DIM:(h + 1) * HEAD_DIM, :] for h in range(N_KV_HEADS)]
            k_lo = [jnp.concatenate([kh, zero_k], axis=0) for kh in heads]
            k_hi = [jnp.concatenate([zero_k, kh], axis=0) for kh in heads]
            v_lo = (jnp.where(lo, v2, zero_v), jnp.where(lo, v2r, zero_v))
            v_hi = (jnp.where(lo, zero_v, v2r), jnp.where(lo, zero_v, v2))
            mask2 = jnp.concatenate([mask, mask], axis=0)
            block_cache[blk] = (mask2, k_lo, k_hi, v_lo, v_hi)
        return block_cache[blk]

    upper = lax.broadcasted_iota(jnp.int32, (2 * WINDOW, 1), 0) < WINDOW

    def attn_unit(blk, h):
        rows = slice(blk * WINDOW, (blk + 1) * WINDOW)
        mask2, k_lo, k_hi, v_lo, v_hi = block_operands(blk)
        slots = (2 * h, 2 * h + 1)
        q2 = jnp.concatenate([q_ref[rows, s * LANES:(s + 1) * LANES] for s in slots], axis=0)
        scores = [_dot(q2, kk) for kk in (k_lo[h], k_hi[h])]
        yield
        out = None
        for half, vv in enumerate((v_lo[h], v_hi[h])):
            sink = jnp.where(upper, sinks_ref[2 * slots[0] + half], sinks_ref[2 * slots[1] + half])
            s = jnp.where(mask2, scores[half], -jnp.inf)
            m = jnp.maximum(jnp.max(s, axis=-1, keepdims=True), sink)
            e = jnp.exp(s - m)
            denom = jnp.sum(e, axis=-1, keepdims=True) + jnp.exp(sink - m)
            p = (e * (1.0 / denom)).astype(BF16)
            o = _dot(p, vv)
            out = o if out is None else out + o
        out = out.astype(BF16)
        mix_ref[rows, slots[0] * LANES:(slots[0] + 1) * LANES] = out[:WINDOW]
        mix_ref[rows, slots[1] * LANES:(slots[1] + 1) * LANES] = out[WINDOW:]

    def gm_unit(blk0):
        pair = (blk0, blk0 + 1)
        rows = [slice(blk * WINDOW, (blk + 1) * WINDOW) for blk in pair]
        vgn = [_gm_norm(uvg_ref[r, GM_WIDTH:], lng_ref[...], lnb_ref[...]) for r in rows]
        if pair[1] == blocks - 1:
            gmv_ref[...] = vgn[1]
        vgb = [v.astype(BF16) for v in vgn]
        gu = [_gelu(uvg_ref[r, :GM_WIDTH]) for r in rows]
        yield
        for g in range(GM_GROUPS):
            cols = slice(g * GM_DIM, (g + 1) * GM_DIM)
            w = jnp.where(tril, ws_ref[g], 0.0).astype(BF16)
            sm = _dot(w, jnp.concatenate([v[:, cols] for v in vgb], axis=1)) + bst_ref[:, g:g + 1]
            for i, r in enumerate(rows):
                mix_ref[r, ATTN_WIDTH + g * GM_DIM:ATTN_WIDTH + (g + 1) * GM_DIM] = (
                    gu[i][:, cols] * sm[:, i * GM_DIM:(i + 1) * GM_DIM]).astype(BF16)

    assert blocks % 2 == 0
    units = []
    for blk0 in range(0, blocks, 2):
        units += [attn_unit(blk, h) for blk in (blk0, blk0 + 1) for h in range(N_KV_HEADS)]
        units.append(gm_unit(blk0))
    return units


KV_COL = ATTN_WIDTH
UVG_COL = ATTN_WIDTH + 2 * KV_WIDTH


def _heads_to_slots(q):
    lo = lax.broadcasted_iota(jnp.int32, (q.shape[0], LANES), 1) < HEAD_DIM
    slots = []
    for r in range(N_Q_HEADS):
        src = q[:, (r // 2) * LANES:(r // 2 + 1) * LANES]
        if r % 2 != r // (N_Q_HEADS // N_KV_HEADS):
            src = pltpu.roll(src, HEAD_DIM, 1)
        keep = lo if r // (N_Q_HEADS // N_KV_HEADS) == 0 else ~lo
        slots.append(jnp.where(keep, src, jnp.zeros_like(src)))
    return jnp.concatenate(slots, axis=1)


def _slots_to_heads(o):
    lo = lax.broadcasted_iota(jnp.int32, (o.shape[0], LANES), 1) < HEAD_DIM
    chunks = []
    for j in range(N_Q_HEADS // 2):
        halves = []
        for half, r in enumerate((2 * j, 2 * j + 1)):
            src = o[:, r * LANES:(r + 1) * LANES]
            if r // (N_Q_HEADS // N_KV_HEADS) != half:
                src = pltpu.roll(src, HEAD_DIM, 1)
            halves.append(src)
        chunks.append(jnp.where(lo, halves[0], halves[1]))
    return jnp.concatenate(chunks, axis=1)


N_NEXT = 6


def _in_mix_kernel(tiles_per_seq, blocks, n_tiles,
                   x_ref, g1_ref, w1c_ref, w3c_ref, w2c_ref, gm_ref, winc_ref, bin_ref,
                   sinks_ref, lng_ref, lnb_ref, ws_ref, bst_ref, xs_ref, *rest):
    next_f32, rest = rest[:N_NEXT], rest[N_NEXT:]
    (x1_ref, mix_ref, kvw_ref, gmv_ref, x1s_ref, q8s_ref, kvs_ref, uvgs_ref), rest = (
        rest[:8], rest[8:])
    next_bf16, rest = rest[:N_NEXT], rest[N_NEXT:]
    w1_s, w3_s, w2_s, win_s, q_s, kv_s, uvg_s, kvp_s = rest
    s = pl.program_id(0)
    t = s - CAST_STEPS

    @pl.when(s < CAST_STEPS)
    def _():
        for src, dst in ((w1c_ref, w1_s), (w3c_ref, w3_s), (w2c_ref, w2_s), (winc_ref, win_s)):
            _cast_chunk(s, src, dst)

    @pl.when(t == 0)
    def _():
        q_s[...] = jnp.zeros_like(q_s)
        kv_s[...] = jnp.zeros_like(kv_s)
        uvg_s[...] = jnp.zeros_like(uvg_s)
        kvp_s[...] = jnp.zeros_like(kvp_s)

    def ffn(x, before_chunk=None):
        h = _rms(x, g1_ref[...]).astype(BF16)
        return x + 0.5 * _swiglu(h, w1_s, w3_s, w2_s, before_chunk)

    def project(h2, cols):
        return _dot(h2, win_s[:, cols]) + bin_ref[:, cols]

    def ffn_with_previous_mix(x):
        first_of_seq = (jnp.maximum(t - 1, 0) & (tiles_per_seq - 1)) == 0
        units = _mix_units(blocks, first_of_seq, sinks_ref, q_s, kv_s, kvp_s, uvg_s, lng_ref,
                           lnb_ref, ws_ref, bst_ref, mix_ref, gmv_ref)
        in_flight = []

        def before_chunk(c):
            if len(in_flight) == 2:
                for unit in in_flight.pop(0):
                    next(unit, None)
            first, last = c * len(units) // N_FF_CHUNKS, (c + 1) * len(units) // N_FF_CHUNKS
            in_flight.append(units[first:last])
            for unit in in_flight[-1]:
                next(unit)

        x1 = ffn(x, before_chunk)
        for group in in_flight:
            for unit in group:
                next(unit, None)
        return x1

    @pl.when((t >= 0) & (t < n_tiles))
    def _():
        for src, dst in zip(next_f32, next_bf16):
            dst[...] = src[...].astype(BF16)
        x1 = ffn_with_previous_mix(x_ref[...])
        kvp_s[...] = kv_s[(blocks - 1) * WINDOW:, :]
        x1_ref[...] = x1
        h2 = _rms(x1, gm_ref[...]).astype(BF16)
        q_s[...] = (project(h2, slice(0, KV_COL)) * ATTN_SCALE).astype(BF16)
        kv = project(h2, slice(KV_COL, UVG_COL))
        kv_s[...] = kv
        kvw_ref[...] = kv[(blocks - 1) * WINDOW:, :]
        uvg_s[...] = project(h2, slice(UVG_COL, None))

    @pl.when(t == n_tiles)
    def _():
        x1 = ffn_with_previous_mix(xs_ref[...])
        x1s_ref[...] = x1
        h2 = _rms(x1, gm_ref[...]).astype(BF16)
        q8s_ref[...] = _heads_to_slots((project(h2, slice(0, KV_COL)) * ATTN_SCALE).astype(BF16))
        kvs_ref[...] = project(h2, slice(KV_COL, UVG_COL))
        uvgs_ref[...] = project(h2, slice(UVG_COL, None))


def _in_mix(x, xs, g1, w1, w3, w2, gmix, w_in, b_in, sinks, ln_g, ln_b, w_s, b_st,
            next_weights, tm):
    assert len(next_weights) == N_NEXT
    b, s, _ = x.shape
    n = xs.shape[0]
    nj = s // tm
    nt = b * nj
    blocks = tm // WINDOW
    seq_shift = nj.bit_length() - 1
    assert nj == 1 << seq_shift
    x = x.reshape(b * s, D_MODEL)

    def cur(step):
        return jnp.clip(step - CAST_STEPS, 0, nt - 1)

    def prev(step):
        return jnp.clip(step - CAST_STEPS - 1, 0, nt - 1)

    def tile(width, which):
        return pl.BlockSpec((tm, width), lambda step: (which(step), 0))

    def per_seq(rows, width, which):
        return pl.BlockSpec((None, rows, width),
                            lambda step: (lax.shift_right_logical(which(step), seq_shift), 0, 0))

    def whole(shape):
        return pl.BlockSpec(shape, lambda step: (0,) * len(shape))

    gconsts = (ln_g, ln_b, w_s, b_st)
    sample_out = ((n, D_MODEL), F32), ((n, N_Q_HEADS * LANES), BF16), ((n, 2 * KV_WIDTH), F32), (
        (n, 2 * GM_WIDTH), F32)
    return pl.pallas_call(
        functools.partial(_in_mix_kernel, nj, blocks, nt),
        out_shape=(jax.ShapeDtypeStruct((b * s, D_MODEL), F32),
                   jax.ShapeDtypeStruct((b * s, D_MODEL), BF16),
                   jax.ShapeDtypeStruct((b, WINDOW, 2 * KV_WIDTH), F32),
                   jax.ShapeDtypeStruct((b, CHUNK, GM_WIDTH), F32))
        + tuple(jax.ShapeDtypeStruct(shape, dtype) for shape, dtype in sample_out)
        + tuple(jax.ShapeDtypeStruct(w.shape, BF16) for w in next_weights),
        grid=(CAST_STEPS + nt + 1,),
        in_specs=[tile(D_MODEL, cur), _resident(g1.shape), _cast_spec(w1), _cast_spec(w3),
                  _cast_spec(w2), _resident(gmix.shape), _cast_spec(w_in), _resident(b_in.shape),
                  pl.BlockSpec(memory_space=pltpu.SMEM)]
        + [_resident(c.shape) for c in gconsts + (xs,)]
        + [_trickle_spec(w, nt) for w in next_weights],
        out_specs=(tile(D_MODEL, cur), tile(D_MODEL, prev),
                   per_seq(WINDOW, 2 * KV_WIDTH, cur), per_seq(CHUNK, GM_WIDTH, prev))
        + tuple(whole(shape) for shape, _ in sample_out)
        + tuple(_trickle_spec(w, nt) for w in next_weights),
        scratch_shapes=[_bf16_scratch(w1), _bf16_scratch(w3), _bf16_scratch(w2),
                        _bf16_scratch(w_in),
                        pltpu.VMEM((tm, ATTN_WIDTH), BF16), pltpu.VMEM((tm, 2 * KV_WIDTH), F32),
                        pltpu.VMEM((tm, 2 * GM_WIDTH), F32),
                        pltpu.VMEM((WINDOW, 2 * KV_WIDTH), F32)],
        compiler_params=pltpu.CompilerParams(
            dimension_semantics=("arbitrary",), vmem_limit_bytes=VMEM_LIMIT_BYTES),
        name="in_mix",
    )(x, g1, w1, w3, w2, gmix, w_in, b_in, sinks, *gconsts, xs, *next_weights)


def _mix_sample_kernel(q8_ref, kn_ref, vn_ref, kvn_ref, kc_ref, vc_ref, sinks_ref, uvg_ref, lng_ref,
                       lnb_ref, w00_ref, b0_ref, o8_ref, gm_ref, gmv_ref, kw_ref, vw_ref):
    bb = q8_ref.shape[0]
    q8 = q8_ref[...]
    kc = kc_ref[...]
    vc = vc_ref[...]
    kn = kn_ref[...]
    vn = vn_ref[...]
    sink = sinks_ref[...]
    s_c = jnp.einsum("bqd,bdk->bqk", q8, kc.astype(BF16), preferred_element_type=F32)
    s_n = jnp.sum(q8.astype(F32) * kn.astype(BF16).astype(F32), axis=-1, keepdims=True)
    m = jnp.maximum(jnp.maximum(jnp.max(s_c, axis=-1, keepdims=True), s_n), sink)
    e_c = jnp.exp(s_c - m)
    e_n = jnp.exp(s_n - m)
    inv = 1.0 / (jnp.sum(e_c, axis=-1, keepdims=True) + e_n + jnp.exp(sink - m))
    p_c = (e_c * inv).astype(BF16)
    p_n = (e_n * inv).astype(BF16).astype(F32)
    o = jnp.einsum("bqk,bdk->bqd", p_c, vc.astype(BF16), preferred_element_type=F32)
    o = o + p_n * vn.astype(BF16).astype(F32)
    row = lax.broadcasted_iota(jnp.int32, o.shape, 1)
    lane = lax.broadcasted_iota(jnp.int32, o.shape, 2)
    own = (lane < HEAD_DIM) == (row < N_Q_HEADS // N_KV_HEADS)
    o8_ref[...] = jnp.where(own, o, 0.0).astype(BF16)

    pad = jnp.zeros((LANES - bb, 2 * KV_WIDTH), F32)
    new_t = jnp.concatenate([kvn_ref[...], pad], axis=0).T
    last = lax.broadcasted_iota(jnp.int32, (KV_WIDTH, WINDOW), 1) == WINDOW - 1
    for j in range(bb):
        col = new_t[:, j:j + 1]
        kw_ref[j] = jnp.where(last, col[:KV_WIDTH], pltpu.roll(kc[j], WINDOW - 1, 1))
        vw_ref[j] = jnp.where(last, col[KV_WIDTH:], pltpu.roll(vc[j], WINDOW - 1, 1))

    vgn = _gm_norm(uvg_ref[:, GM_WIDTH:], lng_ref[...], lnb_ref[...])
    gmv_ref[...] = vgn
    sm = w00_ref[...].astype(BF16).astype(F32) * vgn.astype(BF16).astype(F32) + b0_ref[...]
    gm_ref[...] = (_gelu(uvg_ref[:, :GM_WIDTH]) * sm).astype(BF16)


def _mix_sample(q8, kn, vn, kvn, kc, vc, sinks, uvg, ln_g, ln_b, w00, b0, bb):
    n = q8.shape[0]

    def b3(d1, d2):
        return pl.BlockSpec((bb, d1, d2), lambda i: (i, 0, 0))

    def b2(d):
        return pl.BlockSpec((bb, d), lambda i: (i, 0))

    return pl.pallas_call(
        _mix_sample_kernel,
        out_shape=(jax.ShapeDtypeStruct((n, N_Q_HEADS, LANES), BF16),
                   jax.ShapeDtypeStruct((n, GM_WIDTH), BF16),
                   jax.ShapeDtypeStruct((n, GM_WIDTH), F32),
                   jax.ShapeDtypeStruct((n, KV_WIDTH, WINDOW), F32),
                   jax.ShapeDtypeStruct((n, KV_WIDTH, WINDOW), F32)),
        grid=(n // bb,),
        in_specs=[b3(N_Q_HEADS, LANES), b3(1, KV_WIDTH), b3(1, KV_WIDTH), b2(2 * KV_WIDTH),
                  b3(KV_WIDTH, WINDOW), b3(KV_WIDTH, WINDOW),
                  _resident(sinks.shape), b2(2 * GM_WIDTH), _resident(ln_g.shape),
                  _resident(ln_b.shape), _resident(w00.shape), _resident(b0.shape)],
        out_specs=(b3(N_Q_HEADS, LANES), b2(GM_WIDTH), b2(GM_WIDTH),
                   b3(KV_WIDTH, WINDOW), b3(KV_WIDTH, WINDOW)),
        compiler_params=pltpu.CompilerParams(
            dimension_semantics=("arbitrary",), vmem_limit_bytes=VMEM_LIMIT_BYTES),
        name="mix_sample",
    )(q8, kn, vn, kvn, kc, vc, sinks, uvg, ln_g, ln_b, w00, b0)


def _out_stage_kernel(n_tiles, x1_ref, mix_ref, p_ref, wo_s, bo_ref, g2_ref, w1_s, w3_s, w2_s,
                      gp_ref, wg_s, wp_s, gf_ref, x1s_ref, mixs_ref, ps_ref,
                      y_ref, ys_ref):
    t = pl.program_id(0)

    def head(x1, mix):
        x2 = x1 + _dot(mix, wo_s[...]) + bo_ref[...]
        return x2, _rms(x2, g2_ref[...]).astype(BF16)

    def finish(x3, p):
        gate = _sigmoid(_dot(_rms(x3, gp_ref[...]).astype(BF16), wg_s[...]))
        x4 = x3 + gate * _dot(p.astype(BF16), wp_s[...])
        return _rms(x4, gf_ref[...])

    def out_stage(x1, mix, p):
        x2, h = head(x1, mix)
        return finish(x2 + 0.5 * _swiglu(h, w1_s, w3_s, w2_s), p)

    @pl.when(t < n_tiles)
    def _():
        n = x1_ref.shape[0] // 2
        first, second = slice(0, n), slice(n, 2 * n)
        x2a, ha = head(x1_ref[first, :], mix_ref[first, :])
        x2b, hb = head(x1_ref[second, :], mix_ref[second, :])
        x3a = x2a + 0.5 * _swiglu(ha, w1_s, w3_s, w2_s)

        def before_chunk(c):
            if c == 1:
                y_ref[first, :] = finish(x3a, p_ref[first, :])

        x3b = x2b + 0.5 * _swiglu(hb, w1_s, w3_s, w2_s, before_chunk)
        y_ref[second, :] = finish(x3b, p_ref[second, :])

    @pl.when(t == n_tiles)
    def _():
        slots = N_Q_HEADS * LANES
        mix = jnp.concatenate([_slots_to_heads(mixs_ref[:, :slots]), mixs_ref[:, slots:]], axis=1)
        ys_ref[...] = out_stage(x1s_ref[...], mix, ps_ref[...])


def _out_stage(x1, mix, p, x1s, mixs, ps, wo, bo, g2, w1, w3, w2, gp, wg, wp, gf, tm):
    r = x1.shape[0]
    n = x1s.shape[0]
    nt = r // tm

    def tile(width):
        return pl.BlockSpec((tm, width), lambda step: (jnp.minimum(step, nt - 1), 0))

    consts = (wo, bo, g2, w1, w3, w2, gp, wg, wp, gf, x1s, mixs, ps)
    return pl.pallas_call(
        functools.partial(_out_stage_kernel, nt),
        out_shape=(jax.ShapeDtypeStruct((r, D_MODEL), F32),
                   jax.ShapeDtypeStruct((n, D_MODEL), F32)),
        grid=(nt + 1,),
        in_specs=[tile(D_MODEL), tile(D_MODEL), tile(PLE_DIM)]
        + [_resident(c.shape) for c in consts],
        out_specs=(tile(D_MODEL), pl.BlockSpec((n, D_MODEL), lambda step: (0, 0))),
        compiler_params=pltpu.CompilerParams(
            dimension_semantics=("arbitrary",), vmem_limit_bytes=VMEM_LIMIT_BYTES),
        name="out_stage",
    )(x1, mix, p, wo, bo, g2, w1, w3, w2, gp, wg, wp, gf, x1s, mixs, ps)


PROMPT_TILE = 512
SAMPLE_BLOCK = 32


def kernel(x_prompt, x_sample, cache_k_win, cache_v_win, p_prompt, p_sample, g_ffn1, w1_ffn1, w3_ffn1, w2_ffn1, g_mix, w_in, b_in, attn_sinks, gm_ln_g, gm_ln_b, gm_w_s, gm_b_s, w_out, b_out, g_ffn2, w1_ffn2, w3_ffn2, w2_ffn2, g_ple, w_ple_gate, w_ple_proj, g_final):
    depth = g_ffn1.shape[0]
    assert depth == 1
    i = 0
    nb, seq, _ = x_prompt.shape
    nd = x_sample.shape[0]

    row = lambda a: a.reshape(1, -1)
    g1, gmx, g2, gp, gf = row(g_ffn1[i]), row(g_mix[i]), row(g_ffn2[i]), row(g_ple[i]), row(g_final)
    ln_g, ln_b = row(gm_ln_g[i]), row(gm_ln_b[i])
    sinks = attn_sinks[i]
    out_weights = (w_out[i], w1_ffn2[i], w3_ffn2[i], w2_ffn2[i], w_ple_gate[i], w_ple_proj[i])
    x1, mix, kv_last, gmv_p, x1s, q8, kvs, uvgs, wo, w1b, w3b, w2b, wg, wp = _in_mix(
        x_prompt, x_sample.reshape(nd, D_MODEL), g1, w1_ffn1[i], w3_ffn1[i], w2_ffn1[i], gmx,
        w_in[i], row(b_in[i]), sinks, ln_g, ln_b, gm_w_s[i], gm_b_s[i].T,
        out_weights, PROMPT_TILE)
    k_win_p = kv_last[:, :, :KV_WIDTH].reshape(1, nb, WINDOW, N_KV_HEADS, HEAD_DIM)
    v_win_p = kv_last[:, :, KV_WIDTH:].reshape(1, nb, WINDOW, N_KV_HEADS, HEAD_DIM)

    def position_minor(c):
        return jnp.transpose(c, (0, 2, 3, 1)).reshape(nd, KV_WIDTH, WINDOW)

    def position_major(c):
        c = jnp.transpose(c.reshape(nd, N_KV_HEADS, HEAD_DIM, WINDOW), (0, 3, 1, 2))
        return c.reshape(1, nd, WINDOW, N_KV_HEADS, HEAD_DIM)

    kvs3 = kvs.reshape(nd, 1, 2 * KV_WIDTH)
    o8, gm_s, gmv_s, kw_s, vw_s = _mix_sample(
        q8.reshape(nd, N_Q_HEADS, LANES), kvs3[:, :, :KV_WIDTH], kvs3[:, :, KV_WIDTH:], kvs,
        position_minor(cache_k_win[i]), position_minor(cache_v_win[i]),
        sinks.reshape(N_Q_HEADS, 1), uvgs, ln_g, ln_b,
        jnp.repeat(gm_w_s[i][:, 0, 0], GM_DIM).reshape(1, GM_WIDTH),
        jnp.repeat(gm_b_s[i][:, 0], GM_DIM).reshape(1, GM_WIDTH), SAMPLE_BLOCK)
    mix_s = jnp.concatenate([o8.reshape(nd, N_Q_HEADS * LANES), gm_s], axis=-1)

    y_prompt, y_sample = _out_stage(
        x1, mix, p_prompt[i].reshape(nb * seq, PLE_DIM), x1s, mix_s,
        p_sample[i].reshape(nd, PLE_DIM), wo, row(b_out[i]),
        g2, w1b, w3b, w2b, gp, wg, wp, gf, 2 * PROMPT_TILE)

    return (y_prompt.reshape(nb, seq, D_MODEL), y_sample.reshape(nd, 1, D_MODEL), k_win_p, v_win_p,
            position_major(kw_s), position_major(vw_s),
            gmv_p.reshape(1, nb, CHUNK, GM_WIDTH), gmv_s.reshape(1, nd, 1, GM_WIDTH))
```

```python
import functools
import math

import jax
import jax.numpy as jnp
from jax import lax
from jax.experimental import pallas as pl
from jax.experimental.pallas import tpu as pltpu

F32 = jnp.float32
BF16 = jnp.bfloat16

D_MODEL = 1024
HEAD_DIM = 64
N_Q_HEADS = 8
N_KV_HEADS = 2
ATTN_WIDTH = N_Q_HEADS * HEAD_DIM
KV_WIDTH = N_KV_HEADS * HEAD_DIM
WINDOW = 128
GM_WIDTH = 512
GM_GROUPS = 4
GM_DIM = 128
CHUNK = 128
D_FF = 2816
PLE_DIM = 256
RMS_EPS = 1e-6
LN_EPS = 1e-5
ATTN_SCALE = HEAD_DIM ** -0.5

LANES = 128
FF_CHUNK = 512
FF_BOUNDS = tuple((lo, min(lo + FF_CHUNK, D_FF)) for lo in range(0, D_FF, FF_CHUNK))
N_FF_CHUNKS = len(FF_BOUNDS)
OUT_FF_CHUNK = 768
OUT_FF_BOUNDS = tuple((lo, min(lo + OUT_FF_CHUNK, D_FF)) for lo in range(0, D_FF, OUT_FF_CHUNK))
VMEM_LIMIT_BYTES = 62 * 1024 * 1024

_GELU_C = math.sqrt(2.0 / math.pi)


def _rms(x, g):
    return x * lax.rsqrt(jnp.mean(x * x, axis=-1, keepdims=True) + RMS_EPS) * g


def _sigmoid(x):
    return 1.0 / (1.0 + jnp.exp(-x))


def _gelu(x):
    return 0.5 * x * (1.0 + jnp.tanh(_GELU_C * (x + 0.044715 * (x * x * x))))


def _dot(a, b):
    return jnp.dot(a, b, preferred_element_type=F32)


def _swiglu(h, w1_ref, w3_ref, w2_ref, before_chunk=None, bounds=FF_BOUNDS):
    acc = jnp.zeros((h.shape[0], D_MODEL), F32)
    for c, (lo, hi) in enumerate(bounds):
        if before_chunk is not None:
            before_chunk(c)
        cols = slice(lo, hi)
        a = _dot(h, w1_ref[:, cols])
        b = _dot(h, w3_ref[:, cols])
        act = (a * _sigmoid(a) * b).astype(BF16)
        acc = acc + _dot(act, w2_ref[cols, :])
    return acc


def _resident(shape):
    zeros = (0,) * len(shape)
    return pl.BlockSpec(shape, lambda *_: zeros, pipeline_mode=pl.Buffered(1))


CAST_STEPS = 4
CAST_BUFFERS = 1
BF16_SUBLANES = 16


def _cast_spec(w):
    rows = w.shape[0] // CAST_STEPS
    assert rows * CAST_STEPS == w.shape[0] and rows % BF16_SUBLANES == 0
    return pl.BlockSpec((rows, w.shape[1]), lambda s: (jnp.minimum(s, CAST_STEPS - 1), 0),
                        pipeline_mode=pl.Buffered(CAST_BUFFERS))


def _trickle_rows(w, max_chunks):
    rows = BF16_SUBLANES
    while w.shape[0] % rows or w.shape[0] // rows > max_chunks:
        rows += BF16_SUBLANES
    return rows


def _trickle_spec(w, max_chunks):
    rows = _trickle_rows(w, max_chunks)
    last = w.shape[0] // rows - 1
    return pl.BlockSpec((rows, w.shape[1]), lambda s: (jnp.clip(s - CAST_STEPS, 0, last), 0))


def _cast_chunk(s, src_ref, dst_ref):
    rows = src_ref.shape[0]
    start = pl.multiple_of(s * rows, BF16_SUBLANES)
    dst_ref[pl.ds(start, rows), :] = src_ref[...].astype(BF16)


def _bf16_scratch(w):
    return pltpu.VMEM(w.shape, BF16)


def _gm_norm(vg, ln_g, ln_b):
    a = _gelu(vg)
    mu = jnp.mean(a, axis=-1, keepdims=True)
    ac = a - mu
    return ac * lax.rsqrt(jnp.mean(ac * ac, axis=-1, keepdims=True) + LN_EPS) * ln_g + ln_b


def _mix_units(blocks, first_of_seq, sinks_ref, q_ref, kv_ref, kvp_ref, uvg_ref, lng_ref,
               lnb_ref, ws_ref, bst_ref, mix_ref, gmv_ref):
    lane = lax.broadcasted_iota(jnp.int32, (2 * WINDOW, LANES), 1)
    lo = lane < HEAD_DIM
    qi = lax.broadcasted_iota(jnp.int32, (WINDOW, 2 * WINDOW), 0)
    kj = lax.broadcasted_iota(jnp.int32, (WINDOW, 2 * WINDOW), 1)
    band = (kj >= qi) & (kj <= qi + WINDOW)
    ti = lax.broadcasted_iota(jnp.int32, (CHUNK, CHUNK), 0)
    si = lax.broadcasted_iota(jnp.int32, (CHUNK, CHUNK), 1)
    tril = si <= ti

    block_cache = {}

    def block_operands(blk):
        if blk not in block_cache:
            if blk == 0:
                kvp = jnp.where(first_of_seq, 0.0, kvp_ref[...])
                kv2 = jnp.concatenate([kvp, kv_ref[:WINDOW, :]], axis=0)
                mask = band & (kj >= jnp.where(first_of_seq, WINDOW, 0))
            else:
                kv2 = kv_ref[(blk - 1) * WINDOW:(blk + 1) * WINDOW, :]
                mask = band
            kt = kv2[:, :KV_WIDTH].T.astype(BF16)
            v2 = kv2[:, KV_WIDTH:].astype(BF16)
            v2r = pltpu.roll(v2, HEAD_DIM, 1)
            zero_v = jnp.zeros_like(v2)
            zero_k = jnp.zeros((HEAD_DIM, 2 * WINDOW), BF16)
            heads = [kt[h * HEAD_DIM:(h + 1) * HEAD_DIM, :] for h in range(N_KV_HEADS)]
            k_lo = [jnp.concatenate([kh, zero_k], axis=0) for kh in heads]
            k_hi = [jnp.concatenate([zero_k, kh], axis=0) for kh in heads]
            v_lo = (jnp.where(lo, v2, zero_v), jnp.where(lo, v2r, zero_v))
            v_hi = (jnp.where(lo, zero_v, v2r), jnp.where(lo, zero_v, v2))
            mask2 = jnp.concatenate([mask, mask], axis=0)
            block_cache[blk] = (mask2, k_lo, k_hi, v_lo, v_hi)
        return block_cache[blk]

    upper = lax.broadcasted_iota(jnp.int32, (2 * WINDOW, 1), 0) < WINDOW

    def attn_unit(blk, h):
        rows = slice(blk * WINDOW, (blk + 1) * WINDOW)
        mask2, k_lo, k_hi, v_lo, v_hi = block_operands(blk)
        slots = (2 * h, 2 * h + 1)
        q2 = jnp.concatenate([q_ref[rows, s * LANES:(s + 1) * LANES] for s in slots], axis=0)
        scores = [_dot(q2, kk) for kk in (k_lo[h], k_hi[h])]
        yield
        out = None
        for half, vv in enumerate((v_lo[h], v_hi[h])):
            sink = jnp.where(upper, sinks_ref[2 * slots[0] + half], sinks_ref[2 * slots[1] + half])
            s = jnp.where(mask2, scores[half], -jnp.inf)
            m = jnp.maximum(jnp.max(s, axis=-1, keepdims=True), sink)
            e = jnp.exp(s - m)
            denom = jnp.sum(e, axis=-1, keepdims=True) + jnp.exp(sink - m)
            p = (e * (1.0 / denom)).astype(BF16)
            o = _dot(p, vv)
            out = o if out is None else out + o
        out = out.astype(BF16)
        mix_ref[rows, slots[0] * LANES:(slots[0] + 1) * LANES] = out[:WINDOW]
        mix_ref[rows, slots[1] * LANES:(slots[1] + 1) * LANES] = out[WINDOW:]

    def gm_unit(blk0):
        pair = (blk0, blk0 + 1)
        rows = [slice(blk * WINDOW, (blk + 1) * WINDOW) for blk in pair]
        vgn = [_gm_norm(uvg_ref[r, GM_WIDTH:], lng_ref[...], lnb_ref[...]) for r in rows]
        if pair[1] == blocks - 1:
            gmv_ref[...] = vgn[1]
        vgb = [v.astype(BF16) for v in vgn]
        gu = [_gelu(uvg_ref[r, :GM_WIDTH]) for r in rows]
        yield
        for g in range(GM_GROUPS):
            cols = slice(g * GM_DIM, (g + 1) * GM_DIM)
            w = jnp.where(tril, ws_ref[g], 0.0).astype(BF16)
            sm = _dot(w, jnp.concatenate([v[:, cols] for v in vgb], axis=1)) + bst_ref[:, g:g + 1]
            for i, r in enumerate(rows):
                mix_ref[r, ATTN_WIDTH + g * GM_DIM:ATTN_WIDTH + (g + 1) * GM_DIM] = (
                    gu[i][:, cols] * sm[:, i * GM_DIM:(i + 1) * GM_DIM]).astype(BF16)

    assert blocks % 2 == 0
    units = []
    for blk0 in range(0, blocks, 2):
        units += [attn_unit(blk, h) for blk in (blk0, blk0 + 1) for h in range(N_KV_HEADS)]
        units.append(gm_unit(blk0))
    return units


KV_COL = ATTN_WIDTH
UVG_COL = ATTN_WIDTH + 2 * KV_WIDTH


def _heads_to_slots(q):
    lo = lax.broadcasted_iota(jnp.int32, (q.shape[0], LANES), 1) < HEAD_DIM
    slots = []
    for r in range(N_Q_HEADS):
        src = q[:, (r // 2) * LANES:(r // 2 + 1) * LANES]
        if r % 2 != r // (N_Q_HEADS // N_KV_HEADS):
            src = pltpu.roll(src, HEAD_DIM, 1)
        keep = lo if r // (N_Q_HEADS // N_KV_HEADS) == 0 else ~lo
        slots.append(jnp.where(keep, src, jnp.zeros_like(src)))
    return jnp.concatenate(slots, axis=1)


def _slots_to_heads(o):
    lo = lax.broadcasted_iota(jnp.int32, (o.shape[0], LANES), 1) < HEAD_DIM
    chunks = []
    for j in range(N_Q_HEADS // 2):
        halves = []
        for half, r in enumerate((2 * j, 2 * j + 1)):
            src = o[:, r * LANES:(r + 1) * LANES]
            if r // (N_Q_HEADS // N_KV_HEADS) != half:
                src = pltpu.roll(src, HEAD_DIM, 1)
            halves.append(src)
        chunks.append(jnp.where(lo, halves[0], halves[1]))
    return jnp.concatenate(chunks, axis=1)


N_NEXT = 6


def _in_mix_kernel(tiles_per_seq, blocks, n_tiles,
                   x_ref, g1_ref, w1c_ref, w3c_ref, w2c_ref, gm_ref, winc_ref, bin_ref,
                   sinks_ref, lng_ref, lnb_ref, ws_ref, bst_ref, xs_ref, *rest):
    next_f32, rest = rest[:N_NEXT], rest[N_NEXT:]
    (x1_ref, mix_ref, kvw_ref, gmv_ref, x1s_ref, q8s_ref, kvs_ref, uvgs_ref), rest = (
        rest[:8], rest[8:])
    next_bf16, rest = rest[:N_NEXT], rest[N_NEXT:]
    w1_s, w3_s, w2_s, win_s, q_s, kv_s, uvg_s, kvp_s = rest
    s = pl.program_id(0)
    t = s - CAST_STEPS

    @pl.when(s < CAST_STEPS)
    def _():
        for src, dst in ((w1c_ref, w1_s), (w3c_ref, w3_s), (w2c_ref, w2_s), (winc_ref, win_s)):
            _cast_chunk(s, src, dst)

    @pl.when(t == 0)
    def _():
        q_s[...] = jnp.zeros_like(q_s)
        kv_s[...] = jnp.zeros_like(kv_s)
        uvg_s[...] = jnp.zeros_like(uvg_s)
        kvp_s[...] = jnp.zeros_like(kvp_s)

    def ffn(x, before_chunk=None):
        h = _rms(x, g1_ref[...]).astype(BF16)
        return x + 0.5 * _swiglu(h, w1_s, w3_s, w2_s, before_chunk)

    def project(h2, cols):
        return _dot(h2, win_s[:, cols]) + bin_ref[:, cols]

    def ffn_with_previous_mix(x):
        first_of_seq = (jnp.maximum(t - 1, 0) & (tiles_per_seq - 1)) == 0
        units = _mix_units(blocks, first_of_seq, sinks_ref, q_s, kv_s, kvp_s, uvg_s, lng_ref,
                           lnb_ref, ws_ref, bst_ref, mix_ref, gmv_ref)
        in_flight = []

        def before_chunk(c):
            for unit in in_flight:
                next(unit, None)
            first, last = c * len(units) // N_FF_CHUNKS, (c + 1) * len(units) // N_FF_CHUNKS
            in_flight[:] = units[first:last]
            for unit in in_flight:
                next(unit)

        x1 = ffn(x, before_chunk)
        for unit in in_flight:
            next(unit, None)
        return x1

    @pl.when((t >= 0) & (t < n_tiles))
    def _():
        for src, dst in zip(next_f32, next_bf16):
            dst[...] = src[...].astype(BF16)
        x1 = ffn_with_previous_mix(x_ref[...])
        kvp_s[...] = kv_s[(blocks - 1) * WINDOW:, :]
        x1_ref[...] = x1
        h2 = _rms(x1, gm_ref[...]).astype(BF16)
        q_s[...] = (project(h2, slice(0, KV_COL)) * ATTN_SCALE).astype(BF16)
        kv = project(h2, slice(KV_COL, UVG_COL))
        kv_s[...] = kv
        kvw_ref[...] = kv[(blocks - 1) * WINDOW:, :]
        uvg_s[...] = project(h2, slice(UVG_COL, None))

    @pl.when(t == n_tiles)
    def _():
        x1 = ffn_with_previous_mix(xs_ref[...])
        x1s_ref[...] = x1
        h2 = _rms(x1, gm_ref[...]).astype(BF16)
        q8s_ref[...] = _heads_to_slots((project(h2, slice(0, KV_COL)) * ATTN_SCALE).astype(BF16))
        kvs_ref[...] = project(h2, slice(KV_COL, UVG_COL))
        uvgs_ref[...] = project(h2, slice(UVG_COL, None))


def _in_mix(x, xs, g1, w1, w3, w2, gmix, w_in, b_in, sinks, ln_g, ln_b, w_s, b_st,
            next_weights, tm):
    assert len(next_weights) == N_NEXT
    b, s, _ = x.shape
    n = xs.shape[0]
    nj = s // tm
    nt = b * nj
    blocks = tm // WINDOW
    seq_shift = nj.bit_length() - 1
    assert nj == 1 << seq_shift
    x = x.reshape(b * s, D_MODEL)

    def cur(step):
        return jnp.clip(step - CAST_STEPS, 0, nt - 1)

    def prev(step):
        return jnp.clip(step - CAST_STEPS - 1, 0, nt - 1)

    def tile(width, which):
        return pl.BlockSpec((tm, width), lambda step: (which(step), 0))

    def per_seq(rows, width, which):
        return pl.BlockSpec((None, rows, width),
                            lambda step: (lax.shift_right_logical(which(step), seq_shift), 0, 0))

    def whole(shape):
        return pl.BlockSpec(shape, lambda step: (0,) * len(shape))

    gconsts = (ln_g, ln_b, w_s, b_st)
    sample_out = ((n, D_MODEL), F32), ((n, N_Q_HEADS * LANES), BF16), ((n, 2 * KV_WIDTH), F32), (
        (n, 2 * GM_WIDTH), F32)
    return pl.pallas_call(
        functools.partial(_in_mix_kernel, nj, blocks, nt),
        out_shape=(jax.ShapeDtypeStruct((b * s, D_MODEL), F32),
                   jax.ShapeDtypeStruct((b * s, D_MODEL), BF16),
                   jax.ShapeDtypeStruct((b, WINDOW, 2 * KV_WIDTH), F32),
                   jax.ShapeDtypeStruct((b, CHUNK, GM_WIDTH), F32))
        + tuple(jax.ShapeDtypeStruct(shape, dtype) for shape, dtype in sample_out)
        + tuple(jax.ShapeDtypeStruct(w.shape, BF16) for w in next_weights),
        grid=(CAST_STEPS + nt + 1,),
        in_specs=[tile(D_MODEL, cur), _resident(g1.shape), _cast_spec(w1), _cast_spec(w3),
                  _cast_spec(w2), _resident(gmix.shape), _cast_spec(w_in), _resident(b_in.shape),
                  pl.BlockSpec(memory_space=pltpu.SMEM)]
        + [_resident(c.shape) for c in gconsts + (xs,)]
        + [_trickle_spec(w, nt) for w in next_weights],
        out_specs=(tile(D_MODEL, cur), tile(D_MODEL, prev),
                   per_seq(WINDOW, 2 * KV_WIDTH, cur), per_seq(CHUNK, GM_WIDTH, prev))
        + tuple(whole(shape) for shape, _ in sample_out)
        + tuple(_trickle_spec(w, nt) for w in next_weights),
        scratch_shapes=[_bf16_scratch(w1), _bf16_scratch(w3), _bf16_scratch(w2),
                        _bf16_scratch(w_in),
                        pltpu.VMEM((tm, ATTN_WIDTH), BF16), pltpu.VMEM((tm, 2 * KV_WIDTH), F32),
                        pltpu.VMEM((tm, 2 * GM_WIDTH), F32),
                        pltpu.VMEM((WINDOW, 2 * KV_WIDTH), F32)],
        compiler_params=pltpu.CompilerParams(
            dimension_semantics=("arbitrary",), vmem_limit_bytes=VMEM_LIMIT_BYTES),
        name="in_mix",
    )(x, g1, w1, w3, w2, gmix, w_in, b_in, sinks, *gconsts, xs, *next_weights)


def _mix_sample_kernel(q8_ref, kn_ref, vn_ref, kvn_ref, kc_ref, vc_ref, sinks_ref, uvg_ref, lng_ref,
                       lnb_ref, w00_ref, b0_ref, o8_ref, gm_ref, gmv_ref, kw_ref, vw_ref):
    bb = q8_ref.shape[0]
    q8 = q8_ref[...]
    kc = kc_ref[...]
    vc = vc_ref[...]
    kn = kn_ref[...]
    vn = vn_ref[...]
    sink = sinks_ref[...]
    s_c = jnp.einsum("bqd,bdk->bqk", q8, kc.astype(BF16), preferred_element_type=F32)
    s_n = jnp.sum(q8.astype(F32) * kn.astype(BF16).astype(F32), axis=-1, keepdims=True)
    m = jnp.maximum(jnp.maximum(jnp.max(s_c, axis=-1, keepdims=True), s_n), sink)
    e_c = jnp.exp(s_c - m)
    e_n = jnp.exp(s_n - m)
    inv = 1.0 / (jnp.sum(e_c, axis=-1, keepdims=True) + e_n + jnp.exp(sink - m))
    p_c = (e_c * inv).astype(BF16)
    p_n = (e_n * inv).astype(BF16).astype(F32)
    o = jnp.einsum("bqk,bdk->bqd", p_c, vc.astype(BF16), preferred_element_type=F32)
    o = o + p_n * vn.astype(BF16).astype(F32)
    row = lax.broadcasted_iota(jnp.int32, o.shape, 1)
    lane = lax.broadcasted_iota(jnp.int32, o.shape, 2)
    own = (lane < HEAD_DIM) == (row < N_Q_HEADS // N_KV_HEADS)
    o8_ref[...] = jnp.where(own, o, 0.0).astype(BF16)

    pad = jnp.zeros((LANES - bb, 2 * KV_WIDTH), F32)
    new_t = jnp.concatenate([kvn_ref[...], pad], axis=0).T
    last = lax.broadcasted_iota(jnp.int32, (KV_WIDTH, WINDOW), 1) == WINDOW - 1
    for j in range(bb):
        col = new_t[:, j:j + 1]
        kw_ref[j] = jnp.where(last, col[:KV_WIDTH], pltpu.roll(kc[j], WINDOW - 1, 1))
        vw_ref[j] = jnp.where(last, col[KV_WIDTH:], pltpu.roll(vc[j], WINDOW - 1, 1))

    vgn = _gm_norm(uvg_ref[:, GM_WIDTH:], lng_ref[...], lnb_ref[...])
    gmv_ref[...] = vgn
    sm = w00_ref[...].astype(BF16).astype(F32) * vgn.astype(BF16).astype(F32) + b0_ref[...]
    gm_ref[...] = (_gelu(uvg_ref[:, :GM_WIDTH]) * sm).astype(BF16)


def _mix_sample(q8, kn, vn, kvn, kc, vc, sinks, uvg, ln_g, ln_b, w00, b0, bb):
    n = q8.shape[0]

    def b3(d1, d2):
        return pl.BlockSpec((bb, d1, d2), lambda i: (i, 0, 0))

    def b2(d):
        return pl.BlockSpec((bb, d), lambda i: (i, 0))

    return pl.pallas_call(
        _mix_sample_kernel,
        out_shape=(jax.ShapeDtypeStruct((n, N_Q_HEADS, LANES), BF16),
                   jax.ShapeDtypeStruct((n, GM_WIDTH), BF16),
                   jax.ShapeDtypeStruct((n, GM_WIDTH), F32),
                   jax.ShapeDtypeStruct((n, KV_WIDTH, WINDOW), F32),
                   jax.ShapeDtypeStruct((n, KV_WIDTH, WINDOW), F32)),
        grid=(n // bb,),
        in_specs=[b3(N_Q_HEADS, LANES), b3(1, KV_WIDTH), b3(1, KV_WIDTH), b2(2 * KV_WIDTH),
                  b3(KV_WIDTH, WINDOW), b3(KV_WIDTH, WINDOW),
                  _resident(sinks.shape), b2(2 * GM_WIDTH), _resident(ln_g.shape),
                  _resident(ln_b.shape), _resident(w00.shape), _resident(b0.shape)],
        out_specs=(b3(N_Q_HEADS, LANES), b2(GM_WIDTH), b2(GM_WIDTH),
                   b3(KV_WIDTH, WINDOW), b3(KV_WIDTH, WINDOW)),
        compiler_params=pltpu.CompilerParams(
            dimension_semantics=("arbitrary",), vmem_limit_bytes=VMEM_LIMIT_BYTES),
        name="mix_sample",
    )(q8, kn, vn, kvn, kc, vc, sinks, uvg, ln_g, ln_b, w00, b0)


def _out_stage_kernel(n_tiles, x1_ref, mix_ref, p_ref, wo_s, bo_ref, g2_ref, w1_s, w3_s, w2_s,
                      gp_ref, wg_s, wp_s, gf_ref, x1s_ref, mixs_ref, ps_ref,
                      y_ref, ys_ref):
    t = pl.program_id(0)

    def head(x1, mix):
        x2 = x1 + _dot(mix, wo_s[...]) + bo_ref[...]
        return x2, _rms(x2, g2_ref[...]).astype(BF16)

    def finish(x3, p):
        gate = _sigmoid(_dot(_rms(x3, gp_ref[...]).astype(BF16), wg_s[...]))
        x4 = x3 + gate * _dot(p.astype(BF16), wp_s[...])
        return _rms(x4, gf_ref[...])

    def out_stage(x1, mix, p):
        x2, h = head(x1, mix)
        return finish(x2 + 0.5 * _swiglu(h, w1_s, w3_s, w2_s, bounds=OUT_FF_BOUNDS), p)

    @pl.when(t < n_tiles)
    def _():
        n = x1_ref.shape[0] // 2
        first, second = slice(0, n), slice(n, 2 * n)
        x2a, ha = head(x1_ref[first, :], mix_ref[first, :])
        x2b, hb = head(x1_ref[second, :], mix_ref[second, :])
        x3a = x2a + 0.5 * _swiglu(ha, w1_s, w3_s, w2_s, bounds=OUT_FF_BOUNDS)

        def before_chunk(c):
            if c == 1:
                y_ref[first, :] = finish(x3a, p_ref[first, :])

        x3b = x2b + 0.5 * _swiglu(hb, w1_s, w3_s, w2_s, before_chunk, bounds=OUT_FF_BOUNDS)
        y_ref[second, :] = finish(x3b, p_ref[second, :])

    @pl.when(t == n_tiles)
    def _():
        slots = N_Q_HEADS * LANES
        mix = jnp.concatenate([_slots_to_heads(mixs_ref[:, :slots]), mixs_ref[:, slots:]], axis=1)
        ys_ref[...] = out_stage(x1s_ref[...], mix, ps_ref[...])


def _out_stage(x1, mix, p, x1s, mixs, ps, wo, bo, g2, w1, w3, w2, gp, wg, wp, gf, tm):
    r = x1.shape[0]
    n = x1s.shape[0]
    nt = r // tm

    def tile(width):
        return pl.BlockSpec((tm, width), lambda step: (jnp.minimum(step, nt - 1), 0))

    consts = (wo, bo, g2, w1, w3, w2, gp, wg, wp, gf, x1s, mixs, ps)
    return pl.pallas_call(
        functools.partial(_out_stage_kernel, nt),
        out_shape=(jax.ShapeDtypeStruct((r, D_MODEL), F32),
                   jax.ShapeDtypeStruct((n, D_MODEL), F32)),
        grid=(nt + 1,),
        in_specs=[tile(D_MODEL), tile(D_MODEL), tile(PLE_DIM)]
        + [_resident(c.shape) for c in consts],
        out_specs=(tile(D_MODEL), pl.BlockSpec((n, D_MODEL), lambda step: (0, 0))),
        compiler_params=pltpu.CompilerParams(
            dimension_semantics=("arbitrary",), vmem_limit_bytes=VMEM_LIMIT_BYTES),
        name="out_stage",
    )(x1, mix, p, wo, bo, g2, w1, w3, w2, gp, wg, wp, gf, x1s, mixs, ps)


PROMPT_TILE = 512
SAMPLE_BLOCK = 32


def kernel(x_prompt, x_sample, cache_k_win, cache_v_win, p_prompt, p_sample, g_ffn1, w1_ffn1, w3_ffn1, w2_ffn1, g_mix, w_in, b_in, attn_sinks, gm_ln_g, gm_ln_b, gm_w_s, gm_b_s, w_out, b_out, g_ffn2, w1_ffn2, w3_ffn2, w2_ffn2, g_ple, w_ple_gate, w_ple_proj, g_final):
    depth = g_ffn1.shape[0]
    assert depth == 1
    i = 0
    nb, seq, _ = x_prompt.shape
    nd = x_sample.shape[0]

    row = lambda a: a.reshape(1, -1)
    g1, gmx, g2, gp, gf = row(g_ffn1[i]), row(g_mix[i]), row(g_ffn2[i]), row(g_ple[i]), row(g_final)
    ln_g, ln_b = row(gm_ln_g[i]), row(gm_ln_b[i])
    sinks = attn_sinks[i]
    out_weights = (w_out[i], w1_ffn2[i], w3_ffn2[i], w2_ffn2[i], w_ple_gate[i], w_ple_proj[i])
    x1, mix, kv_last, gmv_p, x1s, q8, kvs, uvgs, wo, w1b, w3b, w2b, wg, wp = _in_mix(
        x_prompt, x_sample.reshape(nd, D_MODEL), g1, w1_ffn1[i], w3_ffn1[i], w2_ffn1[i], gmx,
        w_in[i], row(b_in[i]), sinks, ln_g, ln_b, gm_w_s[i], gm_b_s[i].T,
        out_weights, PROMPT_TILE)
    k_win_p = kv_last[:, :, :KV_WIDTH].reshape(1, nb, WINDOW, N_KV_HEADS, HEAD_DIM)
    v_win_p = kv_last[:, :, KV_WIDTH:].reshape(1, nb, WINDOW, N_KV_HEADS, HEAD_DIM)

    def position_minor(c):
        return jnp.transpose(c, (0, 2, 3, 1)).reshape(nd, KV_WIDTH, WINDOW)

    def position_major(c):
        c = jnp.transpose(c.reshape(nd, N_KV_HEADS, HEAD_DIM, WINDOW), (0, 3, 1, 2))
        return c.reshape(1, nd, WINDOW, N_KV_HEADS, HEAD_DIM)

    kvs3 = kvs.reshape(nd, 1, 2 * KV_WIDTH)
    o8, gm_s, gmv_s, kw_s, vw_s = _mix_sample(
        q8.reshape(nd, N_Q_HEADS, LANES), kvs3[:, :, :KV_WIDTH], kvs3[:, :, KV_WIDTH:], kvs,
        position_minor(cache_k_win[i]), position_minor(cache_v_win[i]),
        sinks.reshape(N_Q_HEADS, 1), uvgs, ln_g, ln_b,
        jnp.repeat(gm_w_s[i][:, 0, 0], GM_DIM).reshape(1, GM_WIDTH),
        jnp.repeat(gm_b_s[i][:, 0], GM_DIM).reshape(1, GM_WIDTH), SAMPLE_BLOCK)
    mix_s = jnp.concatenate([o8.reshape(nd, N_Q_HEADS * LANES), gm_s], axis=-1)

    y_prompt, y_sample = _out_stage(
        x1, mix, p_prompt[i].reshape(nb * seq, PLE_DIM), x1s, mix_s,
        p_sample[i].reshape(nd, PLE_DIM), wo, row(b_out[i]),
        g2, w1b, w3b, w2b, gp, wg, wp, gf, 2 * PROMPT_TILE)

    return (y_prompt.reshape(nb, seq, D_MODEL), y_sample.reshape(nd, 1, D_MODEL), k_win_p, v_win_p,
            position_major(kw_s), position_major(vw_s),
            gmv_p.reshape(1, nb, CHUNK, GM_WIDTH), gmv_s.reshape(1, nd, 1, GM_WIDTH))
```
